```python
import math
import jax
import jax.numpy as jnp
from jax import lax
import numpy as np

D_MODEL = 1024
BATCH = 8
SEQ = 4096
DEPTH = 1
DEC_BATCH = 128
DEC_SEQ = 4
PAST_LEN = 8192
PAGE_SIZE = 128

HEAD_DIM = 64
ATTN_WIDTH = D_MODEL // 2
N_HEADS = ATTN_WIDTH // HEAD_DIM
N_KV = 2
GROUP = N_HEADS // N_KV
KV_WIDTH = N_KV * HEAD_DIM
CONV_WIDTH = D_MODEL - ATTN_WIDTH
CONV_K = 31
CMP_LEN = 32
CMP_STRIDE = 16
CMP_HIDDEN = 256
SEL_BLOCK = 64
TOP_N = 16
WINDOW = 512
N_BUCKETS = 32
MAX_DISTANCE = 128
Q_BLOCK = 64
PEER_HEADS = 8
PEER_DQ = 256
N_KEYS = 128
N_EXPERTS = N_KEYS * N_KEYS
PEER_TOPK = 16
PEER_BLOCK = 256
FORCE_SCORE = 1.0e4
EPS = 1e-6
SPLIT_SIZES = (ATTN_WIDTH, KV_WIDTH, KV_WIDTH, KV_WIDTH, KV_WIDTH, KV_WIDTH, KV_WIDTH, 3 * N_HEADS, 2 * CONV_WIDTH)
IN_COLS = sum(SPLIT_SIZES)

kernel_name = 'hymba_nsa_conformer_peer_step'


def rms_norm(x, g):
    xf = x.astype(jnp.float32)
    y = xf * lax.rsqrt(jnp.mean(xf * xf, axis=-1, keepdims=True) + EPS)
    return (y * g.astype(jnp.float32)).astype(x.dtype)


def layer_norm(x, g, b):
    xf = x.astype(jnp.float32)
    mu = jnp.mean(xf, axis=-1, keepdims=True)
    var = jnp.mean(jnp.square(xf - mu), axis=-1, keepdims=True)
    y = (xf - mu) * lax.rsqrt(var + EPS) * g.astype(jnp.float32) + b.astype(jnp.float32)
    return y.astype(x.dtype)


def masked_softmax(logits, mask, axis):
    neg = jnp.where(mask, logits, -jnp.inf)
    m = jnp.max(neg, axis=axis, keepdims=True)
    m = jnp.where(jnp.isfinite(m), m, 0.0)
    e = jnp.where(mask, jnp.exp(neg - m), 0.0)
    d = jnp.sum(e, axis=axis, keepdims=True)
    return e / jnp.where(d > 0.0, d, 1.0)


def t5_bucket(dist):
    n = jnp.maximum(dist, 0)
    exact = N_BUCKETS // 2
    nf = jnp.maximum(n, 1).astype(jnp.float32)
    large = exact + (jnp.log(nf / exact) / math.log(MAX_DISTANCE / exact) * (N_BUCKETS - exact)).astype(jnp.int32)
    return jnp.where(n < exact, n, jnp.minimum(large, N_BUCKETS - 1))


def compress(rows, pe, w1, b1, w2, b2):
    B, L = rows.shape[0], rows.shape[1]
    ratio = CMP_LEN // CMP_STRIDE
    n_chunks = L // CMP_STRIDE
    nc = n_chunks - ratio + 1
    r = rows[:, :n_chunks * CMP_STRIDE].reshape(B, n_chunks, CMP_STRIDE, N_KV, HEAD_DIM)
    w1c = w1.reshape(ratio, CMP_STRIDE, HEAD_DIM, CMP_HIDDEN)
    parts = jnp.einsum('bcsgd,jsdh->jbcgh', r, w1c)
    hid = b1 + pe.reshape(-1) @ w1
    for j in range(ratio):
        hid = hid + parts[j, :, j:j + nc]
    out = jax.nn.gelu(hid) @ w2 + b2
    end = jnp.arange(nc, dtype=jnp.int32) * CMP_STRIDE + (CMP_LEN - 1)
    return out, end


def compress_branch(rows, lp, which):
    return compress(rows, lp['cmp_pe'][which], lp['cmp_w1'][which], lp['cmp_b1'][which],
                    lp['cmp_w2'][which], lp['cmp_b2'][which])


def to_blocks(rows):
    B, L = rows.shape[0], rows.shape[1]
    ns = -(-L // SEL_BLOCK)
    rows = jnp.pad(rows, ((0, 0), (0, ns * SEL_BLOCK - L), (0, 0), (0, 0)))
    return jnp.transpose(rows.reshape(B, ns, SEL_BLOCK, N_KV, HEAD_DIM), (0, 3, 1, 2, 4))


def nsa_block(q, t_pos, gates, kc, vc, kc_end, ks_blk, vs_blk, kw, vw, w_pos, rel_bias):
    B = q.shape[0]
    scale = HEAD_DIM ** -0.5
    tbl = rel_bias.astype(jnp.float32).reshape(N_BUCKETS, N_KV, GROUP)
    lc = jnp.einsum('bqgrd,bcgd->bqgrc', q, kc).astype(jnp.float32) * scale
    dc = t_pos[:, None] - kc_end[None, :]
    lc = lc + jnp.transpose(tbl[t5_bucket(dc)], (0, 2, 3, 1))[None]
    pc = masked_softmax(lc, (dc >= 0)[None, :, None, None, :], axis=-1)
    o_cmp = jnp.einsum('bqgrc,bcgd->bqgrd', pc.astype(vc.dtype), vc)
    nc, ns = kc.shape[1], ks_blk.shape[2]
    ci = jnp.arange(nc)[:, None]
    sj = jnp.arange(ns)[None, :]
    cover = ((ci * CMP_STRIDE < (sj + 1) * SEL_BLOCK) & (ci * CMP_STRIDE + CMP_LEN > sj * SEL_BLOCK)).astype(jnp.float32)
    imp = jnp.einsum('bqgrc,cj->bqgj', pc, cover)
    cur = (t_pos // SEL_BLOCK)[:, None]
    forced = (sj == 0) | (sj == cur) | (sj == cur - 1)
    valid = sj * SEL_BLOCK <= t_pos[:, None]
    imp = jnp.where(forced[None, :, None, :], imp + FORCE_SCORE, imp)
    imp = jnp.where(valid[None, :, None, :], imp, -FORCE_SCORE)
    _, idx = lax.top_k(imp, min(TOP_N, ns))
    bi = jnp.arange(B)[:, None, None, None]
    gi = jnp.arange(N_KV)[None, None, :, None]
    ks_g = ks_blk[bi, gi, idx]
    vs_g = vs_blk[bi, gi, idx]
    ls = jnp.einsum('bqgrd,bqgnkd->bqgrnk', q, ks_g).astype(jnp.float32) * scale
    pos = idx[..., None] * SEL_BLOCK + jnp.arange(SEL_BLOCK, dtype=jnp.int32)
    ds = t_pos[None, :, None, None, None] - pos
    tblk = jnp.transpose(tbl, (1, 0, 2))
    bs = tblk[jnp.arange(N_KV)[None, None, :, None, None], t5_bucket(ds)]
    ls = ls + jnp.moveaxis(bs, -1, 3)
    ps = masked_softmax(ls, (ds >= 0)[:, :, :, None], axis=(-2, -1))
    o_sel = jnp.einsum('bqgrnk,bqgnkd->bqgrd', ps.astype(vs_g.dtype), vs_g)
    lw = jnp.einsum('bqgrd,bwgd->bqgrw', q, kw).astype(jnp.float32) * scale
    dw = t_pos[:, None] - w_pos[None, :]
    lw = lw + jnp.transpose(tbl[t5_bucket(dw)], (0, 2, 3, 1))[None]
    mw = (dw >= 0) & (dw < WINDOW) & (w_pos >= 0)[None, :]
    pw = masked_softmax(lw, mw[None, :, None, None, :], axis=-1)
    o_win = jnp.einsum('bqgrw,bwgd->bqgrd', pw.astype(vw.dtype), vw)
    return gates[..., 0:1] * o_cmp + gates[..., 1:2] * o_sel + gates[..., 2:3] * o_win


def nsa_prompt(q, gates, kc_raw, vc_raw, ks, vs, kw, vw, lp, rel_bias):
    B, T = q.shape[0], q.shape[1]
    kc, kc_end = compress_branch(kc_raw, lp, 0)
    kc = rms_norm(kc, lp['k_norm'][0])
    vc, _ = compress_branch(vc_raw, lp, 1)
    ks_blk, vs_blk = to_blocks(ks), to_blocks(vs)
    pad = ((0, 0), (WINDOW, 0), (0, 0), (0, 0))
    kw_pad, vw_pad = jnp.pad(kw, pad), jnp.pad(vw, pad)
    nq = T // Q_BLOCK
    qb = jnp.moveaxis(q.reshape(B, nq, Q_BLOCK, N_KV, GROUP, HEAD_DIM), 1, 0)
    gb = jnp.moveaxis(gates.reshape(B, nq, Q_BLOCK, N_KV, GROUP, 3), 1, 0)

    def one_block(args):
        i, q_i, g_i = args
        start = i * Q_BLOCK
        t_pos = start + jnp.arange(Q_BLOCK, dtype=jnp.int32)
        kw_i = lax.dynamic_slice_in_dim(kw_pad, start, WINDOW + Q_BLOCK, axis=1)
        vw_i = lax.dynamic_slice_in_dim(vw_pad, start, WINDOW + Q_BLOCK, axis=1)
        w_pos = start - WINDOW + jnp.arange(WINDOW + Q_BLOCK, dtype=jnp.int32)
        return nsa_block(q_i, t_pos, g_i, kc, vc, kc_end, ks_blk, vs_blk, kw_i, vw_i, w_pos, rel_bias)

    o = lax.map(one_block, (jnp.arange(nq, dtype=jnp.int32), qb, gb))
    o = jnp.moveaxis(o, 0, 1).reshape(B, T, ATTN_WIDTH)
    new_win = jnp.stack([kw, vw], axis=2)[:, T - min(WINDOW, T):]
    return o, (jnp.stack([kc_raw, vc_raw], axis=2), jnp.stack([ks, vs], axis=2), new_win)


def nsa_sample(q, gates, kc_raw, vc_raw, ks, vs, kw, vw, cache_cmp, cache_sel, cache_win, page_table, lp, rel_bias):
    Bd, Tn = q.shape[0], q.shape[1]
    past = page_table.shape[1] * cache_cmp.shape[1]
    t_pos = past + jnp.arange(Tn, dtype=jnp.int32)
    past_cmp = cache_cmp[page_table].reshape(Bd, past, 2, N_KV, HEAD_DIM)
    past_sel = cache_sel[page_table].reshape(Bd, past, 2, N_KV, HEAD_DIM)
    kc, kc_end = compress_branch(jnp.concatenate([past_cmp[:, :, 0], kc_raw], axis=1), lp, 0)
    kc = rms_norm(kc, lp['k_norm'][0])
    vc, _ = compress_branch(jnp.concatenate([past_cmp[:, :, 1], vc_raw], axis=1), lp, 1)
    ks_blk = to_blocks(jnp.concatenate([past_sel[:, :, 0], ks], axis=1))
    vs_blk = to_blocks(jnp.concatenate([past_sel[:, :, 1], vs], axis=1))
    wk = cache_win.shape[1]
    win = jnp.concatenate([cache_win, jnp.stack([kw, vw], axis=2)], axis=1)
    w_pos = past - wk + jnp.arange(wk + Tn, dtype=jnp.int32)
    o = nsa_block(q, t_pos, gates, kc, vc, kc_end, ks_blk, vs_blk, win[:, :, 0], win[:, :, 1], w_pos, rel_bias)
    new_win = win[:, wk + Tn - min(WINDOW, wk + Tn):]
    return o.reshape(Bd, Tn, ATTN_WIDTH), (jnp.stack([kc_raw, vc_raw], axis=2), jnp.stack([ks, vs], axis=2), new_win)


def conformer_conv(glu_in, hist, lp):
    u = glu_in[..., :CONV_WIDTH] * jax.nn.sigmoid(glu_in[..., CONV_WIDTH:])
    uh = jnp.concatenate([hist.astype(u.dtype), u], axis=1)
    y = lax.conv_general_dilated(uh, lp['conv_w'][:, None, :], window_strides=(1,), padding='VALID',
                                 dimension_numbers=('NWC', 'WIO', 'NWC'), feature_group_count=CONV_WIDTH)
    y = jax.nn.silu(layer_norm(y + lp['conv_b'], lp['conv_ln_g'], lp['conv_ln_b']))
    return y, uh[:, uh.shape[1] - (CONV_K - 1):]


def peer(h, lp):
    shp = h.shape
    xt = h.reshape(-1, D_MODEL)
    n = xt.shape[0]
    nb = -(-n // PEER_BLOCK)
    xt = jnp.pad(xt, ((0, nb * PEER_BLOCK - n), (0, 0))).reshape(nb, PEER_BLOCK, D_MODEL)
    q_w, sub_keys, eu, ev = lp['peer_q'], lp['peer_keys'], lp['expert_u'], lp['expert_v']

    def one_block(xb):
        q = (xb @ q_w).reshape(PEER_BLOCK, PEER_HEADS, 2, PEER_DQ // 2)
        s = jnp.einsum('thcd,hcnd->thcn', q, sub_keys).astype(jnp.float32)
        s1, i1 = lax.top_k(s[:, :, 0], PEER_TOPK)
        s2, i2 = lax.top_k(s[:, :, 1], PEER_TOPK)
        cand = (s1[..., :, None] + s2[..., None, :]).reshape(PEER_BLOCK, PEER_HEADS, PEER_TOPK * PEER_TOPK)
        cidx = (i1[..., :, None] * N_KEYS + i2[..., None, :]).reshape(PEER_BLOCK, PEER_HEADS, PEER_TOPK * PEER_TOPK)
        top_s, pos = lax.top_k(cand, PEER_TOPK)
        eidx = jnp.take_along_axis(cidx, pos, axis=-1)
        g = jax.nn.softmax(top_s, axis=-1)
        u = eu[eidx]
        v = ev[eidx]
        a = jax.nn.gelu(jnp.einsum('td,thkd->thk', xb, u).astype(jnp.float32))
        return jnp.einsum('thk,thkd->td', (g * a).astype(v.dtype), v)

    y = lax.map(one_block, xt).reshape(-1, D_MODEL)[:n]
    return y.reshape(shp)


def split_points():
    pts, acc = [], 0
    for s in SPLIT_SIZES[:-1]:
        acc += s
        pts.append(acc)
    return pts


def layer(x, c, conv_hist, lp, rel_bias, cache):
    B, T = x.shape[0], x.shape[1]
    mod = (jax.nn.silu(c) @ lp['w_ada'] + lp['b_ada'])[:, None, :]
    sh1, sc1, gt1, sh2, sc2, gt2 = jnp.split(mod, 6, axis=-1)
    h = rms_norm(x, lp['norm1']) * (1 + sc1) + sh1
    zq, zkc, zvc, zks, zvs, zkw, zvw, zg, zglu = jnp.split(h @ lp['w_in'], split_points(), axis=-1)

    def kv(z):
        return z.reshape(B, T, N_KV, HEAD_DIM)

    q = rms_norm(zq.reshape(B, T, N_KV, GROUP, HEAD_DIM), lp['q_norm'])
    ks = rms_norm(kv(zks), lp['k_norm'][1])
    kw = rms_norm(kv(zkw), lp['k_norm'][2])
    gates = jax.nn.sigmoid(zg.reshape(B, T, N_KV, GROUP, 3))
    if cache is None:
        o_attn, attn_state = nsa_prompt(q, gates, kv(zkc), kv(zvc), ks, kv(zvs), kw, kv(zvw), lp, rel_bias)
    else:
        cache_cmp, cache_sel, cache_win, page_table = cache
        o_attn, attn_state = nsa_sample(q, gates, kv(zkc), kv(zvc), ks, kv(zvs), kw, kv(zvw),
                                        cache_cmp, cache_sel, cache_win, page_table, lp, rel_bias)
    o_conv, conv_state = conformer_conv(zglu, conv_hist, lp)
    mix = jnp.concatenate([rms_norm(o_attn, lp['attn_out_norm']), o_conv * lp['conv_out_scale']], axis=-1) @ lp['w_out']
    x = x + gt1 * mix
    h2 = rms_norm(x, lp['norm2']) * (1 + sc2) + sh2
    x = x + gt2 * peer(h2, lp)
    return x, attn_state, conv_state


def setup_inputs(seed: int = 0) -> dict:
    key = jax.random.key(seed)
    k = jax.random.split(key, 36)
    n_pages = PAST_LEN // PAGE_SIZE
    n_used = DEC_BATCH * n_pages
    n_pool = n_used + max(1, n_used // 4)
    win_keep = min(WINDOW, PAST_LEN)
    f32 = jnp.float32

    def nrm(kk, shape, s):
        return jax.random.normal(kk, shape, f32) * s

    def gain(kk, shape):
        return 1.0 + 0.02 * jax.random.normal(kk, shape, f32)

    page_table = jax.random.permutation(k[8], n_pool)[:n_used].reshape(DEC_BATCH, n_pages).astype(jnp.int32)
    return {
        'x_prompt': nrm(k[0], (BATCH, SEQ, D_MODEL), 1.0),
        'x_sample': nrm(k[1], (DEC_BATCH, DEC_SEQ, D_MODEL), 1.0),
        'c_prompt': nrm(k[2], (BATCH, D_MODEL), 1.0),
        'c_sample': nrm(k[3], (DEC_BATCH, D_MODEL), 1.0),
        'cache_cmp_kv': nrm(k[4], (DEPTH, n_pool, PAGE_SIZE, 2, N_KV, HEAD_DIM), 1.0),
        'cache_sel_kv': nrm(k[5], (DEPTH, n_pool, PAGE_SIZE, 2, N_KV, HEAD_DIM), 1.0),
        'cache_win_kv': nrm(k[6], (DEPTH, DEC_BATCH, win_keep, 2, N_KV, HEAD_DIM), 1.0),
        'state_conv': nrm(k[7], (DEPTH, DEC_BATCH, CONV_K - 1, CONV_WIDTH), 0.5),
        'page_table': page_table,
        'rel_bias': nrm(k[9], (N_BUCKETS, N_HEADS), 0.5),
        'w_ada': nrm(k[10], (DEPTH, D_MODEL, 6 * D_MODEL), 0.5 * D_MODEL ** -0.5),
        'b_ada': nrm(k[11], (DEPTH, 6 * D_MODEL), 0.02),
        'norm1': gain(k[12], (DEPTH, D_MODEL)),
        'w_in': nrm(k[13], (DEPTH, D_MODEL, IN_COLS), D_MODEL ** -0.5),
        'q_norm': gain(k[14], (DEPTH, HEAD_DIM)),
        'k_norm': gain(k[15], (DEPTH, 3, HEAD_DIM)),
        'cmp_pe': nrm(k[16], (DEPTH, 2, CMP_LEN, HEAD_DIM), 0.1),
        'cmp_w1': nrm(k[17], (DEPTH, 2, CMP_LEN * HEAD_DIM, CMP_HIDDEN), (CMP_LEN * HEAD_DIM) ** -0.5),
        'cmp_b1': nrm(k[18], (DEPTH, 2, CMP_HIDDEN), 0.02),
        'cmp_w2': nrm(k[19], (DEPTH, 2, CMP_HIDDEN, HEAD_DIM), CMP_HIDDEN ** -0.5),
        'cmp_b2': nrm(k[20], (DEPTH, 2, HEAD_DIM), 0.02),
        'conv_w': nrm(k[21], (DEPTH, CONV_K, CONV_WIDTH), CONV_K ** -0.5),
        'conv_b': nrm(k[22], (DEPTH, CONV_WIDTH), 0.02),
        'conv_ln_g': gain(k[23], (DEPTH, CONV_WIDTH)),
        'conv_ln_b': nrm(k[24], (DEPTH, CONV_WIDTH), 0.02),
        'attn_out_norm': gain(k[25], (DEPTH, ATTN_WIDTH)),
        'conv_out_scale': gain(k[26], (DEPTH, CONV_WIDTH)),
        'w_out': nrm(k[27], (DEPTH, D_MODEL, D_MODEL), D_MODEL ** -0.5),
        'norm2': gain(k[28], (DEPTH, D_MODEL)),
        'peer_q': nrm(k[29], (DEPTH, D_MODEL, PEER_HEADS * PEER_DQ), D_MODEL ** -0.5),
        'peer_keys': nrm(k[30], (DEPTH, PEER_HEADS, 2, N_KEYS, PEER_DQ // 2), (PEER_DQ // 2) ** -0.5),
        'expert_u': nrm(k[31], (DEPTH, N_EXPERTS, D_MODEL), D_MODEL ** -0.5),
        'expert_v': nrm(k[32], (DEPTH, N_EXPERTS, D_MODEL), 1.0),
    }


def reference(x_prompt, x_sample, c_prompt, c_sample, cache_cmp_kv, cache_sel_kv, cache_win_kv, state_conv,
              page_table, rel_bias, w_ada, b_ada, norm1, w_in, q_norm, k_norm, cmp_pe, cmp_w1, cmp_b1, cmp_w2,
              cmp_b2, conv_w, conv_b, conv_ln_g, conv_ln_b, attn_out_norm, conv_out_scale, w_out, norm2,
              peer_q, peer_keys, expert_u, expert_v):
    y_prompt, y_sample = x_prompt, x_sample
    p_cmp, p_sel, p_win, p_conv = [], [], [], []
    s_cmp, s_sel, s_win, s_conv = [], [], [], []
    for l in range(DEPTH):
        lp = {
            'w_ada': w_ada[l], 'b_ada': b_ada[l], 'norm1': norm1[l], 'w_in': w_in[l],
            'q_norm': q_norm[l], 'k_norm': k_norm[l], 'cmp_pe': cmp_pe[l], 'cmp_w1': cmp_w1[l],
            'cmp_b1': cmp_b1[l], 'cmp_w2': cmp_w2[l], 'cmp_b2': cmp_b2[l], 'conv_w': conv_w[l],
            'conv_b': conv_b[l], 'conv_ln_g': conv_ln_g[l], 'conv_ln_b': conv_ln_b[l],
            'attn_out_norm': attn_out_norm[l], 'conv_out_scale': conv_out_scale[l], 'w_out': w_out[l],
            'norm2': norm2[l], 'peer_q': peer_q[l], 'peer_keys': peer_keys[l],
            'expert_u': expert_u[l], 'expert_v': expert_v[l],
        }
        hist0 = jnp.zeros((y_prompt.shape[0], CONV_K - 1, CONV_WIDTH), y_prompt.dtype)
        y_prompt, (a_c, a_s, a_w), cv = layer(y_prompt, c_prompt, hist0, lp, rel_bias, None)
        p_cmp.append(a_c)
        p_sel.append(a_s)
        p_win.append(a_w)
        p_conv.append(cv)
        y_sample, (b_c, b_s, b_w), cvs = layer(y_sample, c_sample, state_conv[l], lp, rel_bias,
                                               (cache_cmp_kv[l], cache_sel_kv[l], cache_win_kv[l], page_table))
        s_cmp.append(b_c)
        s_sel.append(b_s)
        s_win.append(b_w)
        s_conv.append(cvs)
    new_cmp_prompt = jnp.stack(p_cmp)
    new_sel_prompt = jnp.stack(p_sel)
    new_win_prompt = jnp.stack(p_win)
    new_conv_prompt = jnp.stack(p_conv)
    new_cmp_sample = jnp.stack(s_cmp)
    new_sel_sample = jnp.stack(s_sel)
    new_win_sample = jnp.stack(s_win)
    new_conv_sample = jnp.stack(s_conv)
    return (y_prompt, y_sample, new_cmp_prompt, new_sel_prompt, new_win_prompt, new_conv_prompt,
            new_cmp_sample, new_sel_sample, new_win_sample, new_conv_sample)
```

```python
import functools
import math

import jax
import jax.numpy as jnp
from jax import lax
from jax.experimental import pallas as pl
from jax.experimental.pallas import tpu as pltpu

D_MODEL = 1024
HEAD_DIM = 64
ATTN_WIDTH = D_MODEL // 2
N_HEADS = ATTN_WIDTH // HEAD_DIM
N_KV = 2
GROUP = N_HEADS // N_KV
KV_WIDTH = N_KV * HEAD_DIM
CONV_WIDTH = D_MODEL - ATTN_WIDTH
CONV_K = 31
CMP_LEN = 32
CMP_STRIDE = 16
CMP_HIDDEN = 256
SEL_BLOCK = 64
TOP_N = 16
WINDOW = 512
N_BUCKETS = 32
MAX_DISTANCE = 128
Q_BLOCK = 64
PEER_HEADS = 8
PEER_DQ = 256
N_KEYS = 128
PEER_TOPK = 16
PEER_BLOCK = 256
FORCE_SCORE = 1.0e4
EPS = 1e-6
SPLIT_SIZES = (ATTN_WIDTH, KV_WIDTH, KV_WIDTH, KV_WIDTH, KV_WIDTH, KV_WIDTH, KV_WIDTH, 3 * N_HEADS, 2 * CONV_WIDTH)
IN_COLS = sum(SPLIT_SIZES)

LANES = 128
VMEM_LIMIT = 48 * 1024 * 1024


def _matmul_kernel(x_ref, w_ref, o_ref):
    o_ref[...] = jnp.dot(x_ref[...].astype(jnp.bfloat16), w_ref[...],
                         preferred_element_type=jnp.float32)


def matmul(x, w, tm=512):
    m, k = x.shape
    n = w.shape[1]
    n_pad = -(-n // LANES) * LANES
    wb = jnp.pad(w, ((0, 0), (0, n_pad - n))).astype(jnp.bfloat16)
    tm = min(tm, m)
    out = pl.pallas_call(
        _matmul_kernel,
        grid=(m // tm,),
        in_specs=[pl.BlockSpec((tm, k), lambda i: (i, 0)),
                  pl.BlockSpec((k, n_pad), lambda i: (0, 0))],
        out_specs=pl.BlockSpec((tm, n_pad), lambda i: (i, 0)),
        out_shape=jax.ShapeDtypeStruct((m, n_pad), jnp.float32),
        compiler_params=pltpu.CompilerParams(dimension_semantics=("arbitrary",),
                                             vmem_limit_bytes=VMEM_LIMIT),
        name="matmul",
    )(x, wb)
    return out[:, :n]


def rms_norm(x, g):
    y = x * lax.rsqrt(jnp.mean(x * x, axis=-1, keepdims=True) + EPS)
    return y * g


def layer_norm(x, g, b):
    mu = jnp.mean(x, axis=-1, keepdims=True)
    var = jnp.mean(jnp.square(x - mu), axis=-1, keepdims=True)
    return (x - mu) * lax.rsqrt(var + EPS) * g + b


def masked_softmax(logits, mask, axis):
    neg = jnp.where(mask, logits, -jnp.inf)
    m = jnp.max(neg, axis=axis, keepdims=True)
    m = jnp.where(jnp.isfinite(m), m, 0.0)
    e = jnp.where(mask, jnp.exp(neg - m), 0.0)
    d = jnp.sum(e, axis=axis, keepdims=True)
    return e / jnp.where(d > 0.0, d, 1.0)


def t5_bucket(dist):
    n = jnp.maximum(dist, 0)
    exact = N_BUCKETS // 2
    nf = jnp.maximum(n, 1).astype(jnp.float32)
    large = exact + (jnp.log(nf / exact) / math.log(MAX_DISTANCE / exact) * (N_BUCKETS - exact)).astype(jnp.int32)
    return jnp.where(n < exact, n, jnp.minimum(large, N_BUCKETS - 1))


def compress(rows, pe, w1, b1, w2, b2):
    B, L = rows.shape[0], rows.shape[1]
    ratio = CMP_LEN // CMP_STRIDE
    n_chunks = L // CMP_STRIDE
    nc = n_chunks - ratio + 1
    r = rows[:, :n_chunks * CMP_STRIDE].reshape(B, n_chunks, CMP_STRIDE, N_KV, HEAD_DIM)
    w1c = w1.reshape(ratio, CMP_STRIDE, HEAD_DIM, CMP_HIDDEN)
    parts = jnp.einsum('bcsgd,jsdh->jbcgh', r, w1c)
    hid = b1 + pe.reshape(-1) @ w1
    for j in range(ratio):
        hid = hid + parts[j, :, j:j + nc]
    out = jax.nn.gelu(hid) @ w2 + b2
    end = jnp.arange(nc, dtype=jnp.int32) * CMP_STRIDE + (CMP_LEN - 1)
    return out, end


def compress_branch(rows, lp, which):
    return compress(rows, lp['cmp_pe'][which], lp['cmp_w1'][which], lp['cmp_b1'][which],
                    lp['cmp_w2'][which], lp['cmp_b2'][which])


def to_blocks(rows):
    B, L = rows.shape[0], rows.shape[1]
    ns = -(-L // SEL_BLOCK)
    rows = jnp.pad(rows, ((0, 0), (0, ns * SEL_BLOCK - L), (0, 0), (0, 0)))
    return jnp.transpose(rows.reshape(B, ns, SEL_BLOCK, N_KV, HEAD_DIM), (0, 3, 1, 2, 4))


def nsa_block(q, t_pos, gates, kc, vc, kc_end, ks_blk, vs_blk, kw, vw, w_pos, rel_bias):
    B = q.shape[0]
    scale = HEAD_DIM ** -0.5
    tbl = rel_bias.astype(jnp.float32).reshape(N_BUCKETS, N_KV, GROUP)
    lc = jnp.einsum('bqgrd,bcgd->bqgrc', q, kc).astype(jnp.float32) * scale
    dc = t_pos[:, None] - kc_end[None, :]
    lc = lc + jnp.transpose(tbl[t5_bucket(dc)], (0, 2, 3, 1))[None]
    pc = masked_softmax(lc, (dc >= 0)[None, :, None, None, :], axis=-1)
    o_cmp = jnp.einsum('bqgrc,bcgd->bqgrd', pc.astype(vc.dtype), vc)
    nc, ns = kc.shape[1], ks_blk.shape[2]
    ci = jnp.arange(nc)[:, None]
    sj = jnp.arange(ns)[None, :]
    cover = ((ci * CMP_STRIDE < (sj + 1) * SEL_BLOCK) & (ci * CMP_STRIDE + CMP_LEN > sj * SEL_BLOCK)).astype(jnp.float32)
    imp = jnp.einsum('bqgrc,cj->bqgj', pc, cover)
    cur = (t_pos // SEL_BLOCK)[:, None]
    forced = (sj == 0) | (sj == cur) | (sj == cur - 1)
    valid = sj * SEL_BLOCK <= t_pos[:, None]
    imp = jnp.where(forced[None, :, None, :], imp + FORCE_SCORE, imp)
    imp = jnp.where(valid[None, :, None, :], imp, -FORCE_SCORE)
    _, idx = lax.top_k(imp, min(TOP_N, ns))
    bi = jnp.arange(B)[:, None, None, None]
    gi = jnp.arange(N_KV)[None, None, :, None]
    ks_g = ks_blk[bi, gi, idx]
    vs_g = vs_blk[bi, gi, idx]
    ls = jnp.einsum('bqgrd,bqgnkd->bqgrnk', q, ks_g).astype(jnp.float32) * scale
    pos = idx[..., None] * SEL_BLOCK + jnp.arange(SEL_BLOCK, dtype=jnp.int32)
    ds = t_pos[None, :, None, None, None] - pos
    tblk = jnp.transpose(tbl, (1, 0, 2))
    bs = tblk[jnp.arange(N_KV)[None, None, :, None, None], t5_bucket(ds)]
    ls = ls + jnp.moveaxis(bs, -1, 3)
    ps = masked_softmax(ls, (ds >= 0)[:, :, :, None], axis=(-2, -1))
    o_sel = jnp.einsum('bqgrnk,bqgnkd->bqgrd', ps.astype(vs_g.dtype), vs_g)
    lw = jnp.einsum('bqgrd,bwgd->bqgrw', q, kw).astype(jnp.float32) * scale
    dw = t_pos[:, None] - w_pos[None, :]
    lw = lw + jnp.transpose(tbl[t5_bucket(dw)], (0, 2, 3, 1))[None]
    mw = (dw >= 0) & (dw < WINDOW) & (w_pos >= 0)[None, :]
    pw = masked_softmax(lw, mw[None, :, None, None, :], axis=-1)
    o_win = jnp.einsum('bqgrw,bwgd->bqgrd', pw.astype(vw.dtype), vw)
    return gates[..., 0:1] * o_cmp + gates[..., 1:2] * o_sel + gates[..., 2:3] * o_win


def nsa_prompt(q, gates, kc_raw, vc_raw, ks, vs, kw, vw, lp, rel_bias):
    B, T = q.shape[0], q.shape[1]
    kc, kc_end = compress_branch(kc_raw, lp, 0)
    kc = rms_norm(kc, lp['k_norm'][0])
    vc, _ = compress_branch(vc_raw, lp, 1)
    ks_blk, vs_blk = to_blocks(ks), to_blocks(vs)
    pad = ((0, 0), (WINDOW, 0), (0, 0), (0, 0))
    kw_pad, vw_pad = jnp.pad(kw, pad), jnp.pad(vw, pad)
    nq = T // Q_BLOCK
    qb = jnp.moveaxis(q.reshape(B, nq, Q_BLOCK, N_KV, GROUP, HEAD_DIM), 1, 0)
    gb = jnp.moveaxis(gates.reshape(B, nq, Q_BLOCK, N_KV, GROUP, 3), 1, 0)

    def one_block(args):
        i, q_i, g_i = args
        start = i * Q_BLOCK
        t_pos = start + jnp.arange(Q_BLOCK, dtype=jnp.int32)
        kw_i = lax.dynamic_slice_in_dim(kw_pad, start, WINDOW + Q_BLOCK, axis=1)
        vw_i = lax.dynamic_slice_in_dim(vw_pad, start, WINDOW + Q_BLOCK, axis=1)
        w_pos = start - WINDOW + jnp.arange(WINDOW + Q_BLOCK, dtype=jnp.int32)
        return nsa_block(q_i, t_pos, g_i, kc, vc, kc_end, ks_blk, vs_blk, kw_i, vw_i, w_pos, rel_bias)

    o = lax.map(one_block, (jnp.arange(nq, dtype=jnp.int32), qb, gb))
    o = jnp.moveaxis(o, 0, 1).reshape(B, T, ATTN_WIDTH)
    new_win = jnp.stack([kw, vw], axis=2)[:, T - min(WINDOW, T):]
    return o, (jnp.stack([kc_raw, vc_raw], axis=2), jnp.stack([ks, vs], axis=2), new_win)


def nsa_sample(q, gates, kc_raw, vc_raw, ks, vs, kw, vw, cache_cmp, cache_sel, cache_win, page_table, lp, rel_bias):
    Bd, Tn = q.shape[0], q.shape[1]
    past = page_table.shape[1] * cache_cmp.shape[1]
    t_pos = past + jnp.arange(Tn, dtype=jnp.int32)
    past_cmp = cache_cmp[page_table].reshape(Bd, past, 2, N_KV, HEAD_DIM)
    past_sel = cache_sel[page_table].reshape(Bd, past, 2, N_KV, HEAD_DIM)
    kc, kc_end = compress_branch(jnp.concatenate([past_cmp[:, :, 0], kc_raw], axis=1), lp, 0)
    kc = rms_norm(kc, lp['k_norm'][0])
    vc, _ = compress_branch(jnp.concatenate([past_cmp[:, :, 1], vc_raw], axis=1), lp, 1)
    ks_blk = to_blocks(jnp.concatenate([past_sel[:, :, 0], ks], axis=1))
    vs_blk = to_blocks(jnp.concatenate([past_sel[:, :, 1], vs], axis=1))
    wk = cache_win.shape[1]
    win = jnp.concatenate([cache_win, jnp.stack([kw, vw], axis=2)], axis=1)
    w_pos = past - wk + jnp.arange(wk + Tn, dtype=jnp.int32)
    o = nsa_block(q, t_pos, gates, kc, vc, kc_end, ks_blk, vs_blk, win[:, :, 0], win[:, :, 1], w_pos, rel_bias)
    new_win = win[:, wk + Tn - min(WINDOW, wk + Tn):]
    return o.reshape(Bd, Tn, ATTN_WIDTH), (jnp.stack([kc_raw, vc_raw], axis=2), jnp.stack([ks, vs], axis=2), new_win)


def conformer_conv(glu_in, hist, lp):
    u = glu_in[..., :CONV_WIDTH] * jax.nn.sigmoid(glu_in[..., CONV_WIDTH:])
    uh = jnp.concatenate([hist.astype(u.dtype), u], axis=1)
    y = lax.conv_general_dilated(uh, lp['conv_w'][:, None, :], window_strides=(1,), padding='VALID',
                                 dimension_numbers=('NWC', 'WIO', 'NWC'), feature_group_count=CONV_WIDTH)
    y = jax.nn.silu(layer_norm(y + lp['conv_b'], lp['conv_ln_g'], lp['conv_ln_b']))
    return y, uh[:, uh.shape[1] - (CONV_K - 1):]


def peer(h, lp):
    shp = h.shape
    xt = h.reshape(-1, D_MODEL)
    n = xt.shape[0]
    nb = -(-n // PEER_BLOCK)
    q_all = matmul(xt, lp['peer_q'])
    xt = xt.reshape(nb, PEER_BLOCK, D_MODEL)
    q_all = q_all.reshape(nb, PEER_BLOCK, PEER_HEADS * PEER_DQ)
    sub_keys, eu, ev = lp['peer_keys'], lp['expert_u'], lp['expert_v']

    def one_block(args):
        xb, qb = args
        q = qb.reshape(PEER_BLOCK, PEER_HEADS, 2, PEER_DQ // 2)
        s = jnp.einsum('thcd,hcnd->thcn', q, sub_keys).astype(jnp.float32)
        s1, i1 = lax.top_k(s[:, :, 0], PEER_TOPK)
        s2, i2 = lax.top_k(s[:, :, 1], PEER_TOPK)
        cand = (s1[..., :, None] + s2[..., None, :]).reshape(PEER_BLOCK, PEER_HEADS, PEER_TOPK * PEER_TOPK)
        cidx = (i1[..., :, None] * N_KEYS + i2[..., None, :]).reshape(PEER_BLOCK, PEER_HEADS, PEER_TOPK * PEER_TOPK)
        top_s, pos = lax.top_k(cand, PEER_TOPK)
        eidx = jnp.take_along_axis(cidx, pos, axis=-1)
        g = jax.nn.softmax(top_s, axis=-1)
        u = eu[eidx]
        v = ev[eidx]
        a = jax.nn.gelu(jnp.einsum('td,thkd->thk', xb, u).astype(jnp.float32))
        return jnp.einsum('thk,thkd->td', (g * a).astype(v.dtype), v)

    y = lax.map(one_block, (xt, q_all)).reshape(-1, D_MODEL)[:n]
    return y.reshape(shp)


def split_points():
    pts, acc = [], 0
    for s in SPLIT_SIZES[:-1]:
        acc += s
        pts.append(acc)
    return pts


def layer(x, c, conv_hist, lp, rel_bias, cache):
    B, T = x.shape[0], x.shape[1]
    mod = (jax.nn.silu(c) @ lp['w_ada'] + lp['b_ada'])[:, None, :]
    sh1, sc1, gt1, sh2, sc2, gt2 = jnp.split(mod, 6, axis=-1)
    h = rms_norm(x, lp['norm1']) * (1 + sc1) + sh1
    z = matmul(h.reshape(B * T, D_MODEL), lp['w_in']).reshape(B, T, IN_COLS)
    zq, zkc, zvc, zks, zvs, zkw, zvw, zg, zglu = jnp.split(z, split_points(), axis=-1)

    def kv(z):
        return z.reshape(B, T, N_KV, HEAD_DIM)

    q = rms_norm(zq.reshape(B, T, N_KV, GROUP, HEAD_DIM), lp['q_norm'])
    ks = rms_norm(kv(zks), lp['k_norm'][1])
    kw = rms_norm(kv(zkw), lp['k_norm'][2])
    gates = jax.nn.sigmoid(zg.reshape(B, T, N_KV, GROUP, 3))
    if cache is None:
        o_attn, attn_state = nsa_prompt(q, gates, kv(zkc), kv(zvc), ks, kv(zvs), kw, kv(zvw), lp, rel_bias)
    else:
        cache_cmp, cache_sel, cache_win, page_table = cache
        o_attn, attn_state = nsa_sample(q, gates, kv(zkc), kv(zvc), ks, kv(zvs), kw, kv(zvw),
                                        cache_cmp, cache_sel, cache_win, page_table, lp, rel_bias)
    o_conv, conv_state = conformer_conv(zglu, conv_hist, lp)
    cat = jnp.concatenate([rms_norm(o_attn, lp['attn_out_norm']), o_conv * lp['conv_out_scale']], axis=-1)
    mix = matmul(cat.reshape(B * T, D_MODEL), lp['w_out']).reshape(B, T, D_MODEL)
    x = x + gt1 * mix
    h2 = rms_norm(x, lp['norm2']) * (1 + sc2) + sh2
    x = x + gt2 * peer(h2, lp)
    return x, attn_state, conv_state


def kernel(x_prompt, x_sample, c_prompt, c_sample, cache_cmp_kv, cache_sel_kv, cache_win_kv, state_conv, page_table, rel_bias, w_ada, b_ada, norm1, w_in, q_norm, k_norm, cmp_pe, cmp_w1, cmp_b1, cmp_w2, cmp_b2, conv_w, conv_b, conv_ln_g, conv_ln_b, attn_out_norm, conv_out_scale, w_out, norm2, peer_q, peer_keys, expert_u, expert_v):
    lp = {
        'w_ada': w_ada[0], 'b_ada': b_ada[0], 'norm1': norm1[0], 'w_in': w_in[0],
        'q_norm': q_norm[0], 'k_norm': k_norm[0], 'cmp_pe': cmp_pe[0], 'cmp_w1': cmp_w1[0],
        'cmp_b1': cmp_b1[0], 'cmp_w2': cmp_w2[0], 'cmp_b2': cmp_b2[0], 'conv_w': conv_w[0],
        'conv_b': conv_b[0], 'conv_ln_g': conv_ln_g[0], 'conv_ln_b': conv_ln_b[0],
        'attn_out_norm': attn_out_norm[0], 'conv_out_scale': conv_out_scale[0], 'w_out': w_out[0],
        'norm2': norm2[0], 'peer_q': peer_q[0], 'peer_keys': peer_keys[0],
        'expert_u': expert_u[0], 'expert_v': expert_v[0],
    }
    hist0 = jnp.zeros((x_prompt.shape[0], CONV_K - 1, CONV_WIDTH), x_prompt.dtype)
    y_prompt, (a_c, a_s, a_w), cv = layer(x_prompt, c_prompt, hist0, lp, rel_bias, None)
    y_sample, (b_c, b_s, b_w), cvs = layer(x_sample, c_sample, state_conv[0], lp, rel_bias,
                                           (cache_cmp_kv[0], cache_sel_kv[0], cache_win_kv[0], page_table))
    return (y_prompt, y_sample, a_c[None], a_s[None], a_w[None], cv[None],
            b_c[None], b_s[None], b_w[None], cvs[None])
```

```python
import functools
import math

import jax
import jax.numpy as jnp
import numpy as np
from jax import lax
from jax.experimental import pallas as pl
from jax.experimental.pallas import tpu as pltpu

D_MODEL = 1024
HEAD_DIM = 64
ATTN_WIDTH = D_MODEL // 2
N_HEADS = ATTN_WIDTH // HEAD_DIM
N_KV = 2
GROUP = N_HEADS // N_KV
KV_WIDTH = N_KV * HEAD_DIM
CONV_WIDTH = D_MODEL - ATTN_WIDTH
CONV_K = 31
CMP_LEN = 32
CMP_STRIDE = 16
CMP_HIDDEN = 256
SEL_BLOCK = 64
TOP_N = 16
WINDOW = 512
N_BUCKETS = 32
MAX_DISTANCE = 128
Q_BLOCK = 64
PEER_HEADS = 8
PEER_DQ = 256
N_KEYS = 128
PEER_TOPK = 16
PEER_BLOCK = 256
FORCE_SCORE = 1.0e4
EPS = 1e-6
SPLIT_SIZES = (ATTN_WIDTH, KV_WIDTH, KV_WIDTH, KV_WIDTH, KV_WIDTH, KV_WIDTH, KV_WIDTH, 3 * N_HEADS, 2 * CONV_WIDTH)
IN_COLS = sum(SPLIT_SIZES)

LANES = 128
VMEM_LIMIT = 48 * 1024 * 1024


def _matmul_kernel(x_ref, w_ref, o_ref):
    o_ref[...] = jnp.dot(x_ref[...].astype(jnp.bfloat16), w_ref[...],
                         preferred_element_type=jnp.float32)


def matmul(x, w, tm=512):
    m, k = x.shape
    n = w.shape[1]
    n_pad = -(-n // LANES) * LANES
    wb = jnp.pad(w, ((0, 0), (0, n_pad - n))).astype(jnp.bfloat16)
    tm = min(tm, m)
    out = pl.pallas_call(
        _matmul_kernel,
        grid=(m // tm,),
        in_specs=[pl.BlockSpec((tm, k), lambda i: (i, 0)),
                  pl.BlockSpec((k, n_pad), lambda i: (0, 0))],
        out_specs=pl.BlockSpec((tm, n_pad), lambda i: (i, 0)),
        out_shape=jax.ShapeDtypeStruct((m, n_pad), jnp.float32),
        compiler_params=pltpu.CompilerParams(dimension_semantics=("arbitrary",),
                                             vmem_limit_bytes=VMEM_LIMIT),
        name="matmul",
    )(x, wb)
    return out[:, :n]


NEG = -1.0e30
SEL_MASK = 16384.0
FAR_KEYS = 512
NEAR_BLOCKS = 3
WIN_BLOCKS = WINDOW // SEL_BLOCK + 1


def _bucket_np(d):
    n = np.maximum(d, 0)
    exact = N_BUCKETS // 2
    nf = np.maximum(n, 1).astype(np.float32)
    large = exact + (np.log(nf / np.float32(exact)) / np.float32(math.log(MAX_DISTANCE / exact))
                     * np.float32(N_BUCKETS - exact)).astype(np.int32)
    return np.where(n < exact, n, np.minimum(large, N_BUCKETS - 1))


def _bias_table(rel_bias, dist, valid):
    tbl = rel_bias.astype(jnp.float32) - rel_bias[N_BUCKETS - 1].astype(jnp.float32)[None, :]
    b = jnp.transpose(tbl[_bucket_np(dist)], (0, 3, 1, 2))
    return jnp.where(valid[:, None], b, NEG)


def _nsa_prompt_kernel(q_ref, g_ref, kc_ref, vc_ref, ks_ref, vs_ref, kw_ref, vw_ref,
                       bc_ref, bsn_ref, bw_ref, cover_ref, e_ref, o_ref, *, tq):
    f32, bf16 = jnp.float32, jnp.bfloat16
    i = pl.program_id(1)
    rows = N_HEADS * tq
    nt = (((1,), (1,)), ((), ()))

    q = q_ref[0]
    lane_half = lax.broadcasted_iota(jnp.int32, (tq, 2 * HEAD_DIM), 1) // HEAD_DIM
    parts = []
    for h in range(N_HEADS):
        blk = q[:, (h // 2) * 2 * HEAD_DIM:(h // 2 + 1) * 2 * HEAD_DIM]
        grp = h // GROUP
        if h % 2 != grp:
            blk = pltpu.roll(blk, HEAD_DIM, 1)
        parts.append(jnp.where(lane_half == grp, blk, 0.0))
    qpad = jnp.concatenate(parts, axis=0).astype(bf16)

    def heads(x):
        return jnp.concatenate([x[:tq]] * GROUP + [x[tq:]] * GROUP, axis=0)

    tbl_c = bc_ref[0].reshape(rows, bc_ref.shape[-1])
    s = lax.dot_general(qpad, kc_ref[0], nt, preferred_element_type=f32) + tbl_c
    ok = tbl_c > 0.5 * NEG
    m = jnp.max(s, axis=1, keepdims=True)
    e = jnp.where(ok, jnp.exp(s - m), 0.0)
    d = jnp.sum(e, axis=1, keepdims=True)
    pc = e / jnp.where(d > 0.0, d, 1.0)
    o_cmp = jnp.dot(pc.astype(bf16), vc_ref[0], preferred_element_type=f32)

    pg = []
    for grp in range(N_KV):
        acc = pc[grp * GROUP * tq:(grp * GROUP + 1) * tq]
        for r in range(1, GROUP):
            acc = acc + pc[(grp * GROUP + r) * tq:(grp * GROUP + r + 1) * tq]
        pg.append(acc)
    pg = jnp.concatenate(pg, axis=0)
    pg_hi = pg.astype(bf16)
    pg_lo = (pg - pg_hi.astype(f32)).astype(bf16)
    cover = cover_ref[...]
    imp = (jnp.dot(pg_hi, cover, preferred_element_type=f32)
           + jnp.dot(pg_lo, cover, preferred_element_type=f32))
    ns = imp.shape[1]
    j = lax.broadcasted_iota(jnp.int32, imp.shape, 1)
    forced = (j == 0) | (j == i) | (j == i - 1)
    imp = jnp.where(forced, imp + FORCE_SCORE, imp)
    imp = jnp.where(j > i, -FORCE_SCORE, imp)
    cnt = jnp.zeros(imp.shape, jnp.int32)
    for jp in range(ns):
        col = imp[:, jp:jp + 1]
        ahead = (col > imp) | ((col == imp) & (j > jp))
        cnt = cnt + ahead.astype(jnp.int32)
    sel = cnt < TOP_N

    nb0 = jnp.maximum(i - (NEAR_BLOCKS - 1), 0)
    near = NEAR_BLOCKS * SEL_BLOCK
    k0 = pl.multiple_of(nb0 * SEL_BLOCK, SEL_BLOCK)
    sel_nb0 = jnp.sum(jnp.where((j == nb0) & sel, 1.0, 0.0), axis=1, keepdims=True)
    lane = lax.broadcasted_iota(jnp.int32, (rows, near), 1)
    s = lax.dot_general(qpad, ks_ref[0, pl.ds(k0, near), :], nt, preferred_element_type=f32)
    s = s + bsn_ref[0].reshape(rows, near)
    s = s + jnp.where(lane < SEL_BLOCK, (heads(sel_nb0) - 1.0) * SEL_MASK, 0.0)
    m = jnp.max(s, axis=1, keepdims=True)
    p = jnp.exp(s - m)
    l = jnp.sum(p, axis=1, keepdims=True)
    acc = jnp.dot(p.astype(bf16), vs_ref[0, pl.ds(k0, near), :], preferred_element_type=f32)

    sel_far = heads(jnp.where(sel & (j < nb0), 0.0, -1.0)).astype(bf16)
    n_far = (nb0 * SEL_BLOCK + FAR_KEYS - 1) // FAR_KEYS

    def far_body(c, carry):
        m, l, acc = carry
        c0 = pl.multiple_of(c * FAR_KEYS, FAR_KEYS)
        s = lax.dot_general(qpad, ks_ref[0, pl.ds(c0, FAR_KEYS), :], nt, preferred_element_type=f32)
        s = s + jnp.dot(sel_far, e_ref[c], preferred_element_type=f32)
        m_new = jnp.maximum(m, jnp.max(s, axis=1, keepdims=True))
        a = jnp.exp(m - m_new)
        p = jnp.exp(s - m_new)
        l = a * l + jnp.sum(p, axis=1, keepdims=True)
        acc = a * acc + jnp.dot(p.astype(bf16), vs_ref[0, pl.ds(c0, FAR_KEYS), :],
                                preferred_element_type=f32)
        return m_new, l, acc

    m, l, acc = lax.fori_loop(0, n_far, far_body, (m, l, acc))
    o_sel = acc / l

    wk = WIN_BLOCKS * SEL_BLOCK
    w0 = pl.multiple_of(jnp.maximum(i - (WIN_BLOCKS - 1), 0) * SEL_BLOCK, SEL_BLOCK)
    s = lax.dot_general(qpad, kw_ref[0, pl.ds(w0, wk), :], nt, preferred_element_type=f32)
    s = s + bw_ref[0].reshape(rows, wk)
    m = jnp.max(s, axis=1, keepdims=True)
    p = jnp.exp(s - m)
    l = jnp.sum(p, axis=1, keepdims=True)
    o_win = jnp.dot(p.astype(bf16), vw_ref[0, pl.ds(w0, wk), :], preferred_element_type=f32) / l

    gates = g_ref[0]
    for h in range(N_HEADS):
        grp = h // GROUP
        r0, r1 = h * tq, (h + 1) * tq
        c0, c1 = grp * HEAD_DIM, (grp + 1) * HEAD_DIM
        o_h = (gates[:, 3 * h:3 * h + 1] * o_cmp[r0:r1, c0:c1]
               + gates[:, 3 * h + 1:3 * h + 2] * o_sel[r0:r1, c0:c1]
               + gates[:, 3 * h + 2:3 * h + 3] * o_win[r0:r1, c0:c1])
        o_ref[0, :, h * HEAD_DIM:(h + 1) * HEAD_DIM] = o_h


def nsa_prompt_attention(q, gates, kc, vc, ks, vs, kw, vw, rel_bias):
    B, T = q.shape[0], q.shape[1]
    tq = SEL_BLOCK
    nq = T // tq
    ns = T // SEL_BLOCK
    nc = kc.shape[1]
    ncp = -(-nc // LANES) * LANES
    bf16 = jnp.bfloat16
    scale = HEAD_DIM ** -0.5
    assert T % FAR_KEYS == 0 and T >= WIN_BLOCKS * SEL_BLOCK

    tl = np.arange(tq)[None, :, None]
    ii = np.arange(nq)[:, None, None]
    cc = np.arange(ncp)[None, None, :]
    dc = ii * tq + tl - (cc * CMP_STRIDE + CMP_LEN - 1)
    bc = _bias_table(rel_bias, dc, (dc >= 0) & (cc < nc))
    vv = np.arange(NEAR_BLOCKS)[:, None, None]
    dn = vv * tq + tl - np.arange(NEAR_BLOCKS * SEL_BLOCK)[None, None, :]
    bsn = _bias_table(rel_bias, dn, dn >= 0)
    vw_ = np.arange(WIN_BLOCKS)[:, None, None]
    dw = vw_ * tq + tl - np.arange(WIN_BLOCKS * SEL_BLOCK)[None, None, :]
    bw = _bias_table(rel_bias, dw, (dw >= 0) & (dw < WINDOW))

    ci = np.arange(ncp)[:, None]
    sj = np.arange(ns)[None, :]
    cover = ((ci * CMP_STRIDE < (sj + 1) * SEL_BLOCK) & (ci * CMP_STRIDE + CMP_LEN > sj * SEL_BLOCK)
             & (ci < nc))
    cover = jnp.asarray(cover.astype(np.float32), bf16)
    pos = np.arange(T).reshape(T // FAR_KEYS, 1, FAR_KEYS)
    expand = jnp.asarray(np.where(pos // SEL_BLOCK == np.arange(ns)[None, :, None], SEL_MASK, 0.0)
                         .astype(np.float32), bf16)

    padc = ((0, 0), (0, ncp - nc), (0, 0))
    seq = lambda b, i: (b, 0, 0)
    return pl.pallas_call(
        functools.partial(_nsa_prompt_kernel, tq=tq),
        grid=(B, nq),
        in_specs=[
            pl.BlockSpec((1, tq, ATTN_WIDTH), lambda b, i: (b, i, 0)),
            pl.BlockSpec((1, tq, 3 * N_HEADS), lambda b, i: (b, i, 0)),
            pl.BlockSpec((1, ncp, KV_WIDTH), seq),
            pl.BlockSpec((1, ncp, KV_WIDTH), seq),
            pl.BlockSpec((1, T, KV_WIDTH), seq),
            pl.BlockSpec((1, T, KV_WIDTH), seq),
            pl.BlockSpec((1, T, KV_WIDTH), seq),
            pl.BlockSpec((1, T, KV_WIDTH), seq),
            pl.BlockSpec((1, N_HEADS, tq, ncp), lambda b, i: (i, 0, 0, 0)),
            pl.BlockSpec((1, N_HEADS, tq, NEAR_BLOCKS * SEL_BLOCK),
                         lambda b, i: (jnp.minimum(i, NEAR_BLOCKS - 1), 0, 0, 0)),
            pl.BlockSpec((1, N_HEADS, tq, WIN_BLOCKS * SEL_BLOCK),
                         lambda b, i: (jnp.minimum(i, WIN_BLOCKS - 1), 0, 0, 0)),
            pl.BlockSpec((ncp, ns), lambda b, i: (0, 0)),
            pl.BlockSpec((T // FAR_KEYS, ns, FAR_KEYS), lambda b, i: (0, 0, 0)),
        ],
        out_specs=pl.BlockSpec((1, tq, ATTN_WIDTH), lambda b, i: (b, i, 0)),
        out_shape=jax.ShapeDtypeStruct((B, T, ATTN_WIDTH), jnp.float32),
        compiler_params=pltpu.CompilerParams(dimension_semantics=("arbitrary", "arbitrary"),
                                             vmem_limit_bytes=VMEM_LIMIT),
        name="nsa_prompt",
    )(q * scale, gates, jnp.pad(kc, padc).astype(bf16), jnp.pad(vc, padc).astype(bf16),
      ks.astype(bf16), vs.astype(bf16), kw.astype(bf16), vw.astype(bf16), bc, bsn, bw, cover, expand)


def rms_norm(x, g):
    y = x * lax.rsqrt(jnp.mean(x * x, axis=-1, keepdims=True) + EPS)
    return y * g


def layer_norm(x, g, b):
    mu = jnp.mean(x, axis=-1, keepdims=True)
    var = jnp.mean(jnp.square(x - mu), axis=-1, keepdims=True)
    return (x - mu) * lax.rsqrt(var + EPS) * g + b


def masked_softmax(logits, mask, axis):
    neg = jnp.where(mask, logits, -jnp.inf)
    m = jnp.max(neg, axis=axis, keepdims=True)
    m = jnp.where(jnp.isfinite(m), m, 0.0)
    e = jnp.where(mask, jnp.exp(neg - m), 0.0)
    d = jnp.sum(e, axis=axis, keepdims=True)
    return e / jnp.where(d > 0.0, d, 1.0)


def t5_bucket(dist):
    n = jnp.maximum(dist, 0)
    exact = N_BUCKETS // 2
    nf = jnp.maximum(n, 1).astype(jnp.float32)
    large = exact + (jnp.log(nf / exact) / math.log(MAX_DISTANCE / exact) * (N_BUCKETS - exact)).astype(jnp.int32)
    return jnp.where(n < exact, n, jnp.minimum(large, N_BUCKETS - 1))


def compress(rows, pe, w1, b1, w2, b2):
    B, L = rows.shape[0], rows.shape[1]
    ratio = CMP_LEN // CMP_STRIDE
    n_chunks = L // CMP_STRIDE
    nc = n_chunks - ratio + 1
    r = rows[:, :n_chunks * CMP_STRIDE].reshape(B, n_chunks, CMP_STRIDE, N_KV, HEAD_DIM)
    w1c = w1.reshape(ratio, CMP_STRIDE, HEAD_DIM, CMP_HIDDEN)
    parts = jnp.einsum('bcsgd,jsdh->jbcgh', r, w1c)
    hid = b1 + pe.reshape(-1) @ w1
    for j in range(ratio):
        hid = hid + parts[j, :, j:j + nc]
    out = jax.nn.gelu(hid) @ w2 + b2
    end = jnp.arange(nc, dtype=jnp.int32) * CMP_STRIDE + (CMP_LEN - 1)
    return out, end


def compress_branch(rows, lp, which):
    return compress(rows, lp['cmp_pe'][which], lp['cmp_w1'][which], lp['cmp_b1'][which],
                    lp['cmp_w2'][which], lp['cmp_b2'][which])


def to_blocks(rows):
    B, L = rows.shape[0], rows.shape[1]
    ns = -(-L // SEL_BLOCK)
    rows = jnp.pad(rows, ((0, 0), (0, ns * SEL_BLOCK - L), (0, 0), (0, 0)))
    return jnp.transpose(rows.reshape(B, ns, SEL_BLOCK, N_KV, HEAD_DIM), (0, 3, 1, 2, 4))


def nsa_block(q, t_pos, gates, kc, vc, kc_end, ks_blk, vs_blk, kw, vw, w_pos, rel_bias):
    B = q.shape[0]
    scale = HEAD_DIM ** -0.5
    tbl = rel_bias.astype(jnp.float32).reshape(N_BUCKETS, N_KV, GROUP)
    lc = jnp.einsum('bqgrd,bcgd->bqgrc', q, kc).astype(jnp.float32) * scale
    dc = t_pos[:, None] - kc_end[None, :]
    lc = lc + jnp.transpose(tbl[t5_bucket(dc)], (0, 2, 3, 1))[None]
    pc = masked_softmax(lc, (dc >= 0)[None, :, None, None, :], axis=-1)
    o_cmp = jnp.einsum('bqgrc,bcgd->bqgrd', pc.astype(vc.dtype), vc)
    nc, ns = kc.shape[1], ks_blk.shape[2]
    ci = jnp.arange(nc)[:, None]
    sj = jnp.arange(ns)[None, :]
    cover = ((ci * CMP_STRIDE < (sj + 1) * SEL_BLOCK) & (ci * CMP_STRIDE + CMP_LEN > sj * SEL_BLOCK)).astype(jnp.float32)
    imp = jnp.einsum('bqgrc,cj->bqgj', pc, cover)
    cur = (t_pos // SEL_BLOCK)[:, None]
    forced = (sj == 0) | (sj == cur) | (sj == cur - 1)
    valid = sj * SEL_BLOCK <= t_pos[:, None]
    imp = jnp.where(forced[None, :, None, :], imp + FORCE_SCORE, imp)
    imp = jnp.where(valid[None, :, None, :], imp, -FORCE_SCORE)
    _, idx = lax.top_k(imp, min(TOP_N, ns))
    bi = jnp.arange(B)[:, None, None, None]
    gi = jnp.arange(N_KV)[None, None, :, None]
    ks_g = ks_blk[bi, gi, idx]
    vs_g = vs_blk[bi, gi, idx]
    ls = jnp.einsum('bqgrd,bqgnkd->bqgrnk', q, ks_g).astype(jnp.float32) * scale
    pos = idx[..., None] * SEL_BLOCK + jnp.arange(SEL_BLOCK, dtype=jnp.int32)
    ds = t_pos[None, :, None, None, None] - pos
    tblk = jnp.transpose(tbl, (1, 0, 2))
    bs = tblk[jnp.arange(N_KV)[None, None, :, None, None], t5_bucket(ds)]
    ls = ls + jnp.moveaxis(bs, -1, 3)
    ps = masked_softmax(ls, (ds >= 0)[:, :, :, None], axis=(-2, -1))
    o_sel = jnp.einsum('bqgrnk,bqgnkd->bqgrd', ps.astype(vs_g.dtype), vs_g)
    lw = jnp.einsum('bqgrd,bwgd->bqgrw', q, kw).astype(jnp.float32) * scale
    dw = t_pos[:, None] - w_pos[None, :]
    lw = lw + jnp.transpose(tbl[t5_bucket(dw)], (0, 2, 3, 1))[None]
    mw = (dw >= 0) & (dw < WINDOW) & (w_pos >= 0)[None, :]
    pw = masked_softmax(lw, mw[None, :, None, None, :], axis=-1)
    o_win = jnp.einsum('bqgrw,bwgd->bqgrd', pw.astype(vw.dtype), vw)
    return gates[..., 0:1] * o_cmp + gates[..., 1:2] * o_sel + gates[..., 2:3] * o_win


def nsa_prompt(q, gates, kc_raw, vc_raw, ks, vs, kw, vw, lp, rel_bias):
    B, T = q.shape[0], q.shape[1]
    kc, kc_end = compress_branch(kc_raw, lp, 0)
    kc = rms_norm(kc, lp['k_norm'][0])
    vc, _ = compress_branch(vc_raw, lp, 1)
    flat = lambda a: a.reshape(B, a.shape[1], -1)
    o = nsa_prompt_attention(flat(q), flat(gates), flat(kc), flat(vc), flat(ks), flat(vs), flat(kw), flat(vw),
                             rel_bias)
    new_win = jnp.stack([kw, vw], axis=2)[:, T - min(WINDOW, T):]
    return o, (jnp.stack([kc_raw, vc_raw], axis=2), jnp.stack([ks, vs], axis=2), new_win)


def nsa_sample(q, gates, kc_raw, vc_raw, ks, vs, kw, vw, cache_cmp, cache_sel, cache_win, page_table, lp, rel_bias):
    Bd, Tn = q.shape[0], q.shape[1]
    past = page_table.shape[1] * cache_cmp.shape[1]
    t_pos = past + jnp.arange(Tn, dtype=jnp.int32)
    past_cmp = cache_cmp[page_table].reshape(Bd, past, 2, N_KV, HEAD_DIM)
    past_sel = cache_sel[page_table].reshape(Bd, past, 2, N_KV, HEAD_DIM)
    kc, kc_end = compress_branch(jnp.concatenate([past_cmp[:, :, 0], kc_raw], axis=1), lp, 0)
    kc = rms_norm(kc, lp['k_norm'][0])
    vc, _ = compress_branch(jnp.concatenate([past_cmp[:, :, 1], vc_raw], axis=1), lp, 1)
    ks_blk = to_blocks(jnp.concatenate([past_sel[:, :, 0], ks], axis=1))
    vs_blk = to_blocks(jnp.concatenate([past_sel[:, :, 1], vs], axis=1))
    wk = cache_win.shape[1]
    win = jnp.concatenate([cache_win, jnp.stack([kw, vw], axis=2)], axis=1)
    w_pos = past - wk + jnp.arange(wk + Tn, dtype=jnp.int32)
    o = nsa_block(q, t_pos, gates, kc, vc, kc_end, ks_blk, vs_blk, win[:, :, 0], win[:, :, 1], w_pos, rel_bias)
    new_win = win[:, wk + Tn - min(WINDOW, wk + Tn):]
    return o.reshape(Bd, Tn, ATTN_WIDTH), (jnp.stack([kc_raw, vc_raw], axis=2), jnp.stack([ks, vs], axis=2), new_win)


def conformer_conv(glu_in, hist, lp):
    u = glu_in[..., :CONV_WIDTH] * jax.nn.sigmoid(glu_in[..., CONV_WIDTH:])
    uh = jnp.concatenate([hist.astype(u.dtype), u], axis=1)
    y = lax.conv_general_dilated(uh, lp['conv_w'][:, None, :], window_strides=(1,), padding='VALID',
                                 dimension_numbers=('NWC', 'WIO', 'NWC'), feature_group_count=CONV_WIDTH)
    y = jax.nn.silu(layer_norm(y + lp['conv_b'], lp['conv_ln_g'], lp['conv_ln_b']))
    return y, uh[:, uh.shape[1] - (CONV_K - 1):]


def peer(h, lp):
    shp = h.shape
    xt = h.reshape(-1, D_MODEL)
    n = xt.shape[0]
    nb = -(-n // PEER_BLOCK)
    q_all = matmul(xt, lp['peer_q'])
    xt = xt.reshape(nb, PEER_BLOCK, D_MODEL)
    q_all = q_all.reshape(nb, PEER_BLOCK, PEER_HEADS * PEER_DQ)
    sub_keys, eu, ev = lp['peer_keys'], lp['expert_u'], lp['expert_v']

    def one_block(args):
        xb, qb = args
        q = qb.reshape(PEER_BLOCK, PEER_HEADS, 2, PEER_DQ // 2)
        s = jnp.einsum('thcd,hcnd->thcn', q, sub_keys).astype(jnp.float32)
        s1, i1 = lax.top_k(s[:, :, 0], PEER_TOPK)
        s2, i2 = lax.top_k(s[:, :, 1], PEER_TOPK)
        cand = (s1[..., :, None] + s2[..., None, :]).reshape(PEER_BLOCK, PEER_HEADS, PEER_TOPK * PEER_TOPK)
        cidx = (i1[..., :, None] * N_KEYS + i2[..., None, :]).reshape(PEER_BLOCK, PEER_HEADS, PEER_TOPK * PEER_TOPK)
        top_s, pos = lax.top_k(cand, PEER_TOPK)
        eidx = jnp.take_along_axis(cidx, pos, axis=-1)
        g = jax.nn.softmax(top_s, axis=-1)
        u = eu[eidx]
        v = ev[eidx]
        a = jax.nn.gelu(jnp.einsum('td,thkd->thk', xb, u).astype(jnp.float32))
        return jnp.einsum('thk,thkd->td', (g * a).astype(v.dtype), v)

    y = lax.map(one_block, (xt, q_all)).reshape(-1, D_MODEL)[:n]
    return y.reshape(shp)


def split_points():
    pts, acc = [], 0
    for s in SPLIT_SIZES[:-1]:
        acc += s
        pts.append(acc)
    return pts


def layer(x, c, conv_hist, lp, rel_bias, cache):
    B, T = x.shape[0], x.shape[1]
    mod = (jax.nn.silu(c) @ lp['w_ada'] + lp['b_ada'])[:, None, :]
    sh1, sc1, gt1, sh2, sc2, gt2 = jnp.split(mod, 6, axis=-1)
    h = rms_norm(x, lp['norm1']) * (1 + sc1) + sh1
    z = matmul(h.reshape(B * T, D_MODEL), lp['w_in']).reshape(B, T, IN_COLS)
    zq, zkc, zvc, zks, zvs, zkw, zvw, zg, zglu = jnp.split(z, split_points(), axis=-1)

    def kv(z):
        return z.reshape(B, T, N_KV, HEAD_DIM)

    q = rms_norm(zq.reshape(B, T, N_KV, GROUP, HEAD_DIM), lp['q_norm'])
    ks = rms_norm(kv(zks), lp['k_norm'][1])
    kw = rms_norm(kv(zkw), lp['k_norm'][2])
    gates = jax.nn.sigmoid(zg.reshape(B, T, N_KV, GROUP, 3))
    if cache is None:
        o_attn, attn_state = nsa_prompt(q, gates, kv(zkc), kv(zvc), ks, kv(zvs), kw, kv(zvw), lp, rel_bias)
    else:
        cache_cmp, cache_sel, cache_win, page_table = cache
        o_attn, attn_state = nsa_sample(q, gates, kv(zkc), kv(zvc), ks, kv(zvs), kw, kv(zvw),
                                        cache_cmp, cache_sel, cache_win, page_table, lp, rel_bias)
    o_conv, conv_state = conformer_conv(zglu, conv_hist, lp)
    cat = jnp.concatenate([rms_norm(o_attn, lp['attn_out_norm']), o_conv * lp['conv_out_scale']], axis=-1)
    mix = matmul(cat.reshape(B * T, D_MODEL), lp['w_out']).reshape(B, T, D_MODEL)
    x = x + gt1 * mix
    h2 = rms_norm(x, lp['norm2']) * (1 + sc2) + sh2
    x = x + gt2 * peer(h2, lp)
    return x, attn_state, conv_state


def kernel(x_prompt, x_sample, c_prompt, c_sample, cache_cmp_kv, cache_sel_kv, cache_win_kv, state_conv, page_table, rel_bias, w_ada, b_ada, norm1, w_in, q_norm, k_norm, cmp_pe, cmp_w1, cmp_b1, cmp_w2, cmp_b2, conv_w, conv_b, conv_ln_g, conv_ln_b, attn_out_norm, conv_out_scale, w_out, norm2, peer_q, peer_keys, expert_u, expert_v):
    lp = {
        'w_ada': w_ada[0], 'b_ada': b_ada[0], 'norm1': norm1[0], 'w_in': w_in[0],
        'q_norm': q_norm[0], 'k_norm': k_norm[0], 'cmp_pe': cmp_pe[0], 'cmp_w1': cmp_w1[0],
        'cmp_b1': cmp_b1[0], 'cmp_w2': cmp_w2[0], 'cmp_b2': cmp_b2[0], 'conv_w': conv_w[0],
        'conv_b': conv_b[0], 'conv_ln_g': conv_ln_g[0], 'conv_ln_b': conv_ln_b[0],
        'attn_out_norm': attn_out_norm[0], 'conv_out_scale': conv_out_scale[0], 'w_out': w_out[0],
        'norm2': norm2[0], 'peer_q': peer_q[0], 'peer_keys': peer_keys[0],
        'expert_u': expert_u[0], 'expert_v': expert_v[0],
    }
    hist0 = jnp.zeros((x_prompt.shape[0], CONV_K - 1, CONV_WIDTH), x_prompt.dtype)
    y_prompt, (a_c, a_s, a_w), cv = layer(x_prompt, c_prompt, hist0, lp, rel_bias, None)
    y_sample, (b_c, b_s, b_w), cvs = layer(x_sample, c_sample, state_conv[0], lp, rel_bias,
                                           (cache_cmp_kv[0], cache_sel_kv[0], cache_win_kv[0], page_table))
    return (y_prompt, y_sample, a_c[None], a_s[None], a_w[None], cv[None],
            b_c[None], b_s[None], b_w[None], cvs[None])
```

```python
import functools
import math

import jax
import jax.numpy as jnp
import numpy as np
from jax import lax
from jax.experimental import pallas as pl
from jax.experimental.pallas import tpu as pltpu

D_MODEL = 1024
HEAD_DIM = 64
ATTN_WIDTH = D_MODEL // 2
N_HEADS = ATTN_WIDTH // HEAD_DIM
N_KV = 2
GROUP = N_HEADS // N_KV
KV_WIDTH = N_KV * HEAD_DIM
CONV_WIDTH = D_MODEL - ATTN_WIDTH
CONV_K = 31
CMP_LEN = 32
CMP_STRIDE = 16
CMP_HIDDEN = 256
SEL_BLOCK = 64
TOP_N = 16
WINDOW = 512
N_BUCKETS = 32
MAX_DISTANCE = 128
Q_BLOCK = 64
PEER_HEADS = 8
PEER_DQ = 256
N_KEYS = 128
PEER_TOPK = 16
PEER_BLOCK = 256
FORCE_SCORE = 1.0e4
EPS = 1e-6
SPLIT_SIZES = (ATTN_WIDTH, KV_WIDTH, KV_WIDTH, KV_WIDTH, KV_WIDTH, KV_WIDTH, KV_WIDTH, 3 * N_HEADS, 2 * CONV_WIDTH)
IN_COLS = sum(SPLIT_SIZES)

LANES = 128
VMEM_LIMIT = 48 * 1024 * 1024


def _matmul_kernel(x_ref, w_ref, o_ref):
    o_ref[...] = jnp.dot(x_ref[...].astype(jnp.bfloat16), w_ref[...],
                         preferred_element_type=jnp.float32)


def matmul(x, w, tm=512):
    m, k = x.shape
    n = w.shape[1]
    n_pad = -(-n // LANES) * LANES
    wb = jnp.pad(w, ((0, 0), (0, n_pad - n))).astype(jnp.bfloat16)
    tm = min(tm, m)
    out = pl.pallas_call(
        _matmul_kernel,
        grid=(m // tm,),
        in_specs=[pl.BlockSpec((tm, k), lambda i: (i, 0)),
                  pl.BlockSpec((k, n_pad), lambda i: (0, 0))],
        out_specs=pl.BlockSpec((tm, n_pad), lambda i: (i, 0)),
        out_shape=jax.ShapeDtypeStruct((m, n_pad), jnp.float32),
        compiler_params=pltpu.CompilerParams(dimension_semantics=("arbitrary",),
                                             vmem_limit_bytes=VMEM_LIMIT),
        name="matmul",
    )(x, wb)
    return out[:, :n]


NEG = -1.0e30
SEL_MASK = 16384.0
FAR_KEYS = 512
NEAR_BLOCKS = 3
WIN_BLOCKS = WINDOW // SEL_BLOCK + 1


def _bucket_np(d):
    n = np.maximum(d, 0)
    exact = N_BUCKETS // 2
    nf = np.maximum(n, 1).astype(np.float32)
    large = exact + (np.log(nf / np.float32(exact)) / np.float32(math.log(MAX_DISTANCE / exact))
                     * np.float32(N_BUCKETS - exact)).astype(np.int32)
    return np.where(n < exact, n, np.minimum(large, N_BUCKETS - 1))


def _bias_table(rel_bias, dist, valid):
    tbl = rel_bias.astype(jnp.float32) - rel_bias[N_BUCKETS - 1].astype(jnp.float32)[None, :]
    b = jnp.transpose(tbl[_bucket_np(dist)], (0, 3, 1, 2))
    return jnp.where(valid[:, None], b, NEG)


def _nsa_prompt_kernel(q_ref, g_ref, kc_ref, vc_ref, ks_ref, vs_ref, kw_ref, vw_ref,
                       bc_ref, bsn_ref, bw_ref, cover_ref, e_ref, o_ref, *, tq):
    f32, bf16 = jnp.float32, jnp.bfloat16
    i = pl.program_id(1)
    rows = N_HEADS * tq
    nt = (((1,), (1,)), ((), ()))

    q = q_ref[0]
    lane_half = lax.broadcasted_iota(jnp.int32, (tq, 2 * HEAD_DIM), 1) // HEAD_DIM
    parts = []
    for h in range(N_HEADS):
        blk = q[:, (h // 2) * 2 * HEAD_DIM:(h // 2 + 1) * 2 * HEAD_DIM]
        grp = h // GROUP
        if h % 2 != grp:
            blk = pltpu.roll(blk, HEAD_DIM, 1)
        parts.append(jnp.where(lane_half == grp, blk, 0.0))
    qpad = jnp.concatenate(parts, axis=0).astype(bf16)

    def heads(x):
        return jnp.concatenate([x[:tq]] * GROUP + [x[tq:]] * GROUP, axis=0)

    tbl_c = bc_ref[0].reshape(rows, bc_ref.shape[-1])
    s = lax.dot_general(qpad, kc_ref[0], nt, preferred_element_type=f32) + tbl_c
    ok = tbl_c > 0.5 * NEG
    m = jnp.max(s, axis=1, keepdims=True)
    e = jnp.where(ok, jnp.exp(s - m), 0.0)
    d = jnp.sum(e, axis=1, keepdims=True)
    pc = e / jnp.where(d > 0.0, d, 1.0)
    o_cmp = jnp.dot(pc.astype(bf16), vc_ref[0], preferred_element_type=f32)

    pg = []
    for grp in range(N_KV):
        acc = pc[grp * GROUP * tq:(grp * GROUP + 1) * tq]
        for r in range(1, GROUP):
            acc = acc + pc[(grp * GROUP + r) * tq:(grp * GROUP + r + 1) * tq]
        pg.append(acc)
    pg = jnp.concatenate(pg, axis=0)
    pg_hi = pg.astype(bf16)
    pg_lo = (pg - pg_hi.astype(f32)).astype(bf16)
    cover = cover_ref[...]
    imp = (jnp.dot(pg_hi, cover, preferred_element_type=f32)
           + jnp.dot(pg_lo, cover, preferred_element_type=f32))
    ns = imp.shape[1]
    j = lax.broadcasted_iota(jnp.int32, imp.shape, 1)
    forced = (j == 0) | (j == i) | (j == i - 1)
    imp = jnp.where(forced, imp + FORCE_SCORE, imp)
    imp = jnp.where(j > i, -FORCE_SCORE, imp)
    cnt = jnp.zeros(imp.shape, jnp.int32)
    for jp in range(ns):
        col = imp[:, jp:jp + 1]
        ahead = (col > imp) | ((col == imp) & (j > jp))
        cnt = cnt + ahead.astype(jnp.int32)
    sel = cnt < TOP_N

    nb0 = jnp.maximum(i - (NEAR_BLOCKS - 1), 0)
    near = NEAR_BLOCKS * SEL_BLOCK
    k0 = pl.multiple_of(nb0 * SEL_BLOCK, SEL_BLOCK)
    sel_nb0 = jnp.sum(jnp.where((j == nb0) & sel, 1.0, 0.0), axis=1, keepdims=True)
    lane = lax.broadcasted_iota(jnp.int32, (rows, near), 1)
    s = lax.dot_general(qpad, ks_ref[0, pl.ds(k0, near), :], nt, preferred_element_type=f32)
    s = s + bsn_ref[0].reshape(rows, near)
    s = s + jnp.where(lane < SEL_BLOCK, (heads(sel_nb0) - 1.0) * SEL_MASK, 0.0)
    m = jnp.max(s, axis=1, keepdims=True)
    p = jnp.exp(s - m)
    l = jnp.sum(p, axis=1, keepdims=True)
    acc = jnp.dot(p.astype(bf16), vs_ref[0, pl.ds(k0, near), :], preferred_element_type=f32)

    sel_far = heads(jnp.where(sel & (j < nb0), 0.0, -1.0)).astype(bf16)
    n_far = (nb0 * SEL_BLOCK + FAR_KEYS - 1) // FAR_KEYS

    def far_body(c, carry):
        m, l, acc = carry
        c0 = pl.multiple_of(c * FAR_KEYS, FAR_KEYS)
        s = lax.dot_general(qpad, ks_ref[0, pl.ds(c0, FAR_KEYS), :], nt, preferred_element_type=f32)
        s = s + jnp.dot(sel_far, e_ref[c], preferred_element_type=f32)
        m_new = jnp.maximum(m, jnp.max(s, axis=1, keepdims=True))
        a = jnp.exp(m - m_new)
        p = jnp.exp(s - m_new)
        l = a * l + jnp.sum(p, axis=1, keepdims=True)
        acc = a * acc + jnp.dot(p.astype(bf16), vs_ref[0, pl.ds(c0, FAR_KEYS), :],
                                preferred_element_type=f32)
        return m_new, l, acc

    m, l, acc = lax.fori_loop(0, n_far, far_body, (m, l, acc))
    o_sel = acc / l

    wk = WIN_BLOCKS * SEL_BLOCK
    w0 = pl.multiple_of(jnp.maximum(i - (WIN_BLOCKS - 1), 0) * SEL_BLOCK, SEL_BLOCK)
    s = lax.dot_general(qpad, kw_ref[0, pl.ds(w0, wk), :], nt, preferred_element_type=f32)
    s = s + bw_ref[0].reshape(rows, wk)
    m = jnp.max(s, axis=1, keepdims=True)
    p = jnp.exp(s - m)
    l = jnp.sum(p, axis=1, keepdims=True)
    o_win = jnp.dot(p.astype(bf16), vw_ref[0, pl.ds(w0, wk), :], preferred_element_type=f32) / l

    gates = g_ref[0]
    for h in range(N_HEADS):
        grp = h // GROUP
        r0, r1 = h * tq, (h + 1) * tq
        c0, c1 = grp * HEAD_DIM, (grp + 1) * HEAD_DIM
        o_h = (gates[:, 3 * h:3 * h + 1] * o_cmp[r0:r1, c0:c1]
               + gates[:, 3 * h + 1:3 * h + 2] * o_sel[r0:r1, c0:c1]
               + gates[:, 3 * h + 2:3 * h + 3] * o_win[r0:r1, c0:c1])
        o_ref[0, :, h * HEAD_DIM:(h + 1) * HEAD_DIM] = o_h


def nsa_prompt_attention(q, gates, kc, vc, ks, vs, kw, vw, rel_bias):
    B, T = q.shape[0], q.shape[1]
    tq = SEL_BLOCK
    nq = T // tq
    ns = T // SEL_BLOCK
    nc = kc.shape[1]
    ncp = -(-nc // LANES) * LANES
    bf16 = jnp.bfloat16
    scale = HEAD_DIM ** -0.5
    assert T % FAR_KEYS == 0 and T >= WIN_BLOCKS * SEL_BLOCK

    tl = np.arange(tq)[None, :, None]
    r = tq // CMP_STRIDE
    span = r * (nq - 1) + ncp
    y = np.arange(span)[None, None, :]
    dy = tl + CMP_STRIDE * (y - (ncp - 1)) - (CMP_LEN - 1)
    master = _bias_table(rel_bias, dy, dy >= 0)[0][:, :, ::-1]
    pad_col = jnp.arange(ncp) < nc
    bc = jnp.stack([jnp.where(pad_col, master[:, :, r * (nq - 1 - i):r * (nq - 1 - i) + ncp], NEG)
                    for i in range(nq)])
    vv = np.arange(NEAR_BLOCKS)[:, None, None]
    dn = vv * tq + tl - np.arange(NEAR_BLOCKS * SEL_BLOCK)[None, None, :]
    bsn = _bias_table(rel_bias, dn, dn >= 0)
    vw_ = np.arange(WIN_BLOCKS)[:, None, None]
    dw = vw_ * tq + tl - np.arange(WIN_BLOCKS * SEL_BLOCK)[None, None, :]
    bw = _bias_table(rel_bias, dw, (dw >= 0) & (dw < WINDOW))

    ci = np.arange(ncp)[:, None]
    sj = np.arange(ns)[None, :]
    cover = ((ci * CMP_STRIDE < (sj + 1) * SEL_BLOCK) & (ci * CMP_STRIDE + CMP_LEN > sj * SEL_BLOCK)
             & (ci < nc))
    cover = jnp.asarray(cover.astype(np.float32), bf16)
    pos = np.arange(T).reshape(T // FAR_KEYS, 1, FAR_KEYS)
    expand = jnp.asarray(np.where(pos // SEL_BLOCK == np.arange(ns)[None, :, None], SEL_MASK, 0.0)
                         .astype(np.float32), bf16)

    padc = ((0, 0), (0, ncp - nc), (0, 0))
    seq = lambda b, i: (b, 0, 0)
    return pl.pallas_call(
        functools.partial(_nsa_prompt_kernel, tq=tq),
        grid=(B, nq),
        in_specs=[
            pl.BlockSpec((1, tq, ATTN_WIDTH), lambda b, i: (b, i, 0)),
            pl.BlockSpec((1, tq, 3 * N_HEADS), lambda b, i: (b, i, 0)),
            pl.BlockSpec((1, ncp, KV_WIDTH), seq),
            pl.BlockSpec((1, ncp, KV_WIDTH), seq),
            pl.BlockSpec((1, T, KV_WIDTH), seq),
            pl.BlockSpec((1, T, KV_WIDTH), seq),
            pl.BlockSpec((1, T, KV_WIDTH), seq),
            pl.BlockSpec((1, T, KV_WIDTH), seq),
            pl.BlockSpec((1, N_HEADS, tq, ncp), lambda b, i: (i, 0, 0, 0)),
            pl.BlockSpec((1, N_HEADS, tq, NEAR_BLOCKS * SEL_BLOCK),
                         lambda b, i: (jnp.minimum(i, NEAR_BLOCKS - 1), 0, 0, 0)),
            pl.BlockSpec((1, N_HEADS, tq, WIN_BLOCKS * SEL_BLOCK),
                         lambda b, i: (jnp.minimum(i, WIN_BLOCKS - 1), 0, 0, 0)),
            pl.BlockSpec((ncp, ns), lambda b, i: (0, 0)),
            pl.BlockSpec((T // FAR_KEYS, ns, FAR_KEYS), lambda b, i: (0, 0, 0)),
        ],
        out_specs=pl.BlockSpec((1, tq, ATTN_WIDTH), lambda b, i: (b, i, 0)),
        out_shape=jax.ShapeDtypeStruct((B, T, ATTN_WIDTH), jnp.float32),
        compiler_params=pltpu.CompilerParams(dimension_semantics=("arbitrary", "arbitrary"),
                                             vmem_limit_bytes=VMEM_LIMIT),
        name="nsa_prompt",
    )(q * scale, gates, jnp.pad(kc, padc).astype(bf16), jnp.pad(vc, padc).astype(bf16),
      ks.astype(bf16), vs.astype(bf16), kw.astype(bf16), vw.astype(bf16), bc, bsn, bw, cover, expand)


N_SUB = 2 * PEER_HEADS
HALF_DQ = PEER_DQ // 2
N_PAIRS = PEER_HEADS * PEER_TOPK


def _top_rounds(s, row, n):
    width = s.shape[0]
    k = lax.broadcasted_iota(jnp.int32, (n, s.shape[1]), 0)
    vals = jnp.zeros((n, s.shape[1]), jnp.float32)
    ids = jnp.zeros((n, s.shape[1]), jnp.float32)
    for a in range(n):
        m = jnp.max(s, axis=0, keepdims=True)
        ix = jnp.min(jnp.where(s == m, row, float(width)), axis=0, keepdims=True)
        vals = jnp.where(k == a, m, vals)
        ids = jnp.where(k == a, ix, ids)
        s = jnp.where(row == ix, -jnp.inf, s)
    return vals, ids


def _peer_route_kernel(h_ref, wq_ref, keys_ref, idx_ref, g_ref, q_scr, e_scr, w_scr):
    f32, bf16 = jnp.float32, jnp.bfloat16
    tm = h_ref.shape[0]
    nt = (((1,), (1,)), ((), ()))
    q = jnp.dot(h_ref[...].astype(bf16), wq_ref[...], preferred_element_type=f32)
    for k in range(N_SUB):
        q_scr[k] = q[:, k * HALF_DQ:(k + 1) * HALF_DQ].astype(bf16)
    row = lax.broadcasted_iota(jnp.int32, (N_KEYS, tm), 0).astype(f32)
    n_cand = PEER_TOPK * PEER_TOPK
    row2 = lax.broadcasted_iota(jnp.int32, (n_cand, tm), 0).astype(f32)

    def head(h, carry):
        s1, i1 = _top_rounds(lax.dot_general(keys_ref[2 * h], q_scr[2 * h], nt, preferred_element_type=f32),
                             row, PEER_TOPK)
        s2, i2 = _top_rounds(lax.dot_general(keys_ref[2 * h + 1], q_scr[2 * h + 1], nt,
                                             preferred_element_type=f32), row, PEER_TOPK)
        cand = jnp.concatenate([s1[a:a + 1] + s2 for a in range(PEER_TOPK)], axis=0)
        cidx = jnp.concatenate([i1[a:a + 1] * float(N_KEYS) + i2 for a in range(PEER_TOPK)], axis=0)
        k = lax.broadcasted_iota(jnp.int32, (PEER_TOPK, tm), 0)
        top = jnp.zeros((PEER_TOPK, tm), f32)
        eid = jnp.zeros((PEER_TOPK, tm), f32)
        for a in range(PEER_TOPK):
            m = jnp.max(cand, axis=0, keepdims=True)
            pos = jnp.min(jnp.where(cand == m, row2, float(n_cand)), axis=0, keepdims=True)
            hit = row2 == pos
            e = jnp.max(jnp.where(hit, cidx, -1.0), axis=0, keepdims=True)
            top = jnp.where(k == a, m, top)
            eid = jnp.where(k == a, e, eid)
            cand = jnp.where(hit, -jnp.inf, cand)
        ex = jnp.exp(top - top[0:1])
        r0 = pl.multiple_of(h * PEER_TOPK, PEER_TOPK)
        e_scr[pl.ds(r0, PEER_TOPK), :] = eid
        w_scr[pl.ds(r0, PEER_TOPK), :] = ex / jnp.sum(ex, axis=0, keepdims=True)
        return carry

    lax.fori_loop(0, PEER_HEADS, head, 0)
    idx_ref[...] = e_scr[...].T.astype(jnp.int32)
    g_ref[...] = w_scr[...].T


def peer_route(h2, peer_q, peer_keys, tm=128):
    n = h2.shape[0]
    tm = min(tm, n)
    bf16 = jnp.bfloat16
    keys = peer_keys.reshape(N_SUB, N_KEYS, HALF_DQ).astype(bf16)
    return pl.pallas_call(
        _peer_route_kernel,
        grid=(n // tm,),
        in_specs=[pl.BlockSpec((tm, D_MODEL), lambda i: (i, 0)),
                  pl.BlockSpec((D_MODEL, PEER_HEADS * PEER_DQ), lambda i: (0, 0)),
                  pl.BlockSpec((N_SUB, N_KEYS, HALF_DQ), lambda i: (0, 0, 0))],
        out_specs=[pl.BlockSpec((tm, N_PAIRS), lambda i: (i, 0)),
                   pl.BlockSpec((tm, N_PAIRS), lambda i: (i, 0))],
        out_shape=[jax.ShapeDtypeStruct((n, N_PAIRS), jnp.int32),
                   jax.ShapeDtypeStruct((n, N_PAIRS), jnp.float32)],
        scratch_shapes=[pltpu.VMEM((N_SUB, tm, HALF_DQ), bf16),
                        pltpu.VMEM((N_PAIRS, tm), jnp.float32),
                        pltpu.VMEM((N_PAIRS, tm), jnp.float32)],
        compiler_params=pltpu.CompilerParams(dimension_semantics=("arbitrary",),
                                             vmem_limit_bytes=VMEM_LIMIT),
        name="peer_route",
    )(h2, peer_q.astype(bf16), keys)


def _gelu_tanh(x):
    return 0.5 * x * (1.0 + jnp.tanh(math.sqrt(2.0 / math.pi) * (x + 0.044715 * (x * x * x))))


def _peer_expert_kernel(idx_ref, x_ref, g_ref, uv_ref, y_ref, buf, sem):
    tb = x_ref.shape[0]

    def row_copy(t, k, slot):
        return pltpu.make_async_copy(uv_ref.at[idx_ref[t, k]], buf.at[slot, k], sem.at[slot])

    def issue(t, slot):
        for k in range(N_PAIRS):
            row_copy(t, k, slot).start()

    def wait_all(slot):
        pltpu.make_async_copy(uv_ref.at[pl.ds(0, N_PAIRS)], buf.at[slot], sem.at[slot]).wait()

    eye = (lax.broadcasted_iota(jnp.int32, (N_PAIRS, N_PAIRS), 0)
           == lax.broadcasted_iota(jnp.int32, (N_PAIRS, N_PAIRS), 1))
    issue(0, 0)

    def token(t, c):
        slot = t % 2

        @pl.when(t + 1 < tb)
        def _():
            issue(t + 1, 1 - slot)

        wait_all(slot)
        x = x_ref[pl.ds(t, 1), :]
        a = jnp.sum(buf[slot, :, :D_MODEL] * x, axis=1, keepdims=True)
        g_col = jnp.sum(jnp.where(eye, g_ref[pl.ds(t, 1), :], 0.0), axis=1, keepdims=True)
        w = g_col * _gelu_tanh(a)
        y_ref[pl.ds(t, 1), :] = jnp.sum(w * buf[slot, :, D_MODEL:], axis=0, keepdims=True)
        return c

    lax.fori_loop(0, tb, token, 0)


def peer_experts(h2, eidx, gate, expert_u, expert_v, tb=64):
    n = h2.shape[0]
    tb = min(tb, n)
    uv = jnp.concatenate([expert_u, expert_v], axis=1)
    return pl.pallas_call(
        _peer_expert_kernel,
        grid=(n // tb,),
        in_specs=[pl.BlockSpec((tb, N_PAIRS), lambda i: (i, 0), memory_space=pltpu.SMEM),
                  pl.BlockSpec((tb, D_MODEL), lambda i: (i, 0)),
                  pl.BlockSpec((tb, N_PAIRS), lambda i: (i, 0)),
                  pl.BlockSpec(memory_space=pl.ANY)],
        out_specs=pl.BlockSpec((tb, D_MODEL), lambda i: (i, 0)),
        out_shape=jax.ShapeDtypeStruct((n, D_MODEL), jnp.float32),
        scratch_shapes=[pltpu.VMEM((2, N_PAIRS, 2 * D_MODEL), jnp.float32),
                        pltpu.SemaphoreType.DMA((2,))],
        compiler_params=pltpu.CompilerParams(dimension_semantics=("arbitrary",),
                                             vmem_limit_bytes=VMEM_LIMIT),
        name="peer_experts",
    )(eidx, h2, gate, uv)


def rms_norm(x, g):
    y = x * lax.rsqrt(jnp.mean(x * x, axis=-1, keepdims=True) + EPS)
    return y * g


def layer_norm(x, g, b):
    mu = jnp.mean(x, axis=-1, keepdims=True)
    var = jnp.mean(jnp.square(x - mu), axis=-1, keepdims=True)
    return (x - mu) * lax.rsqrt(var + EPS) * g + b


def masked_softmax(logits, mask, axis):
    neg = jnp.where(mask, logits, -jnp.inf)
    m = jnp.max(neg, axis=axis, keepdims=True)
    m = jnp.where(jnp.isfinite(m), m, 0.0)
    e = jnp.where(mask, jnp.exp(neg - m), 0.0)
    d = jnp.sum(e, axis=axis, keepdims=True)
    return e / jnp.where(d > 0.0, d, 1.0)


def t5_bucket(dist):
    n = jnp.maximum(dist, 0)
    exact = N_BUCKETS // 2
    nf = jnp.maximum(n, 1).astype(jnp.float32)
    large = exact + (jnp.log(nf / exact) / math.log(MAX_DISTANCE / exact) * (N_BUCKETS - exact)).astype(jnp.int32)
    return jnp.where(n < exact, n, jnp.minimum(large, N_BUCKETS - 1))


def compress(rows, pe, w1, b1, w2, b2):
    B, L = rows.shape[0], rows.shape[1]
    ratio = CMP_LEN // CMP_STRIDE
    n_chunks = L // CMP_STRIDE
    nc = n_chunks - ratio + 1
    r = rows[:, :n_chunks * CMP_STRIDE].reshape(B, n_chunks, CMP_STRIDE, N_KV, HEAD_DIM)
    w1c = w1.reshape(ratio, CMP_STRIDE, HEAD_DIM, CMP_HIDDEN)
    parts = jnp.einsum('bcsgd,jsdh->jbcgh', r, w1c)
    hid = b1 + pe.reshape(-1) @ w1
    for j in range(ratio):
        hid = hid + parts[j, :, j:j + nc]
    out = jax.nn.gelu(hid) @ w2 + b2
    end = jnp.arange(nc, dtype=jnp.int32) * CMP_STRIDE + (CMP_LEN - 1)
    return out, end


def compress_branch(rows, lp, which):
    return compress(rows, lp['cmp_pe'][which], lp['cmp_w1'][which], lp['cmp_b1'][which],
                    lp['cmp_w2'][which], lp['cmp_b2'][which])


def to_blocks(rows):
    B, L = rows.shape[0], rows.shape[1]
    ns = -(-L // SEL_BLOCK)
    rows = jnp.pad(rows, ((0, 0), (0, ns * SEL_BLOCK - L), (0, 0), (0, 0)))
    return jnp.transpose(rows.reshape(B, ns, SEL_BLOCK, N_KV, HEAD_DIM), (0, 3, 1, 2, 4))


def nsa_block(q, t_pos, gates, kc, vc, kc_end, ks_blk, vs_blk, kw, vw, w_pos, rel_bias):
    B = q.shape[0]
    scale = HEAD_DIM ** -0.5
    tbl = rel_bias.astype(jnp.float32).reshape(N_BUCKETS, N_KV, GROUP)
    lc = jnp.einsum('bqgrd,bcgd->bqgrc', q, kc).astype(jnp.float32) * scale
    dc = t_pos[:, None] - kc_end[None, :]
    lc = lc + jnp.transpose(tbl[t5_bucket(dc)], (0, 2, 3, 1))[None]
    pc = masked_softmax(lc, (dc >= 0)[None, :, None, None, :], axis=-1)
    o_cmp = jnp.einsum('bqgrc,bcgd->bqgrd', pc.astype(vc.dtype), vc)
    nc, ns = kc.shape[1], ks_blk.shape[2]
    ci = jnp.arange(nc)[:, None]
    sj = jnp.arange(ns)[None, :]
    cover = ((ci * CMP_STRIDE < (sj + 1) * SEL_BLOCK) & (ci * CMP_STRIDE + CMP_LEN > sj * SEL_BLOCK)).astype(jnp.float32)
    imp = jnp.einsum('bqgrc,cj->bqgj', pc, cover)
    cur = (t_pos // SEL_BLOCK)[:, None]
    forced = (sj == 0) | (sj == cur) | (sj == cur - 1)
    valid = sj * SEL_BLOCK <= t_pos[:, None]
    imp = jnp.where(forced[None, :, None, :], imp + FORCE_SCORE, imp)
    imp = jnp.where(valid[None, :, None, :], imp, -FORCE_SCORE)
    _, idx = lax.top_k(imp, min(TOP_N, ns))
    bi = jnp.arange(B)[:, None, None, None]
    gi = jnp.arange(N_KV)[None, None, :, None]
    ks_g = ks_blk[bi, gi, idx]
    vs_g = vs_blk[bi, gi, idx]
    ls = jnp.einsum('bqgrd,bqgnkd->bqgrnk', q, ks_g).astype(jnp.float32) * scale
    pos = idx[..., None] * SEL_BLOCK + jnp.arange(SEL_BLOCK, dtype=jnp.int32)
    ds = t_pos[None, :, None, None, None] - pos
    tblk = jnp.transpose(tbl, (1, 0, 2))
    bs = tblk[jnp.arange(N_KV)[None, None, :, None, None], t5_bucket(ds)]
    ls = ls + jnp.moveaxis(bs, -1, 3)
    ps = masked_softmax(ls, (ds >= 0)[:, :, :, None], axis=(-2, -1))
    o_sel = jnp.einsum('bqgrnk,bqgnkd->bqgrd', ps.astype(vs_g.dtype), vs_g)
    lw = jnp.einsum('bqgrd,bwgd->bqgrw', q, kw).astype(jnp.float32) * scale
    dw = t_pos[:, None] - w_pos[None, :]
    lw = lw + jnp.transpose(tbl[t5_bucket(dw)], (0, 2, 3, 1))[None]
    mw = (dw >= 0) & (dw < WINDOW) & (w_pos >= 0)[None, :]
    pw = masked_softmax(lw, mw[None, :, None, None, :], axis=-1)
    o_win = jnp.einsum('bqgrw,bwgd->bqgrd', pw.astype(vw.dtype), vw)
    return gates[..., 0:1] * o_cmp + gates[..., 1:2] * o_sel + gates[..., 2:3] * o_win


def nsa_prompt(q, gates, kc_raw, vc_raw, ks, vs, kw, vw, lp, rel_bias):
    B, T = q.shape[0], q.shape[1]
    kc, kc_end = compress_branch(kc_raw, lp, 0)
    kc = rms_norm(kc, lp['k_norm'][0])
    vc, _ = compress_branch(vc_raw, lp, 1)
    flat = lambda a: a.reshape(B, a.shape[1], -1)
    o = nsa_prompt_attention(flat(q), flat(gates), flat(kc), flat(vc), flat(ks), flat(vs), flat(kw), flat(vw),
                             rel_bias)
    new_win = jnp.stack([kw, vw], axis=2)[:, T - min(WINDOW, T):]
    return o, (jnp.stack([kc_raw, vc_raw], axis=2), jnp.stack([ks, vs], axis=2), new_win)


def nsa_sample(q, gates, kc_raw, vc_raw, ks, vs, kw, vw, cache_cmp, cache_sel, cache_win, page_table, lp, rel_bias):
    Bd, Tn = q.shape[0], q.shape[1]
    past = page_table.shape[1] * cache_cmp.shape[1]
    t_pos = past + jnp.arange(Tn, dtype=jnp.int32)
    past_cmp = cache_cmp[page_table].reshape(Bd, past, 2, N_KV, HEAD_DIM)
    past_sel = cache_sel[page_table].reshape(Bd, past, 2, N_KV, HEAD_DIM)
    kc, kc_end = compress_branch(jnp.concatenate([past_cmp[:, :, 0], kc_raw], axis=1), lp, 0)
    kc = rms_norm(kc, lp['k_norm'][0])
    vc, _ = compress_branch(jnp.concatenate([past_cmp[:, :, 1], vc_raw], axis=1), lp, 1)
    ks_blk = to_blocks(jnp.concatenate([past_sel[:, :, 0], ks], axis=1))
    vs_blk = to_blocks(jnp.concatenate([past_sel[:, :, 1], vs], axis=1))
    wk = cache_win.shape[1]
    win = jnp.concatenate([cache_win, jnp.stack([kw, vw], axis=2)], axis=1)
    w_pos = past - wk + jnp.arange(wk + Tn, dtype=jnp.int32)
    o = nsa_block(q, t_pos, gates, kc, vc, kc_end, ks_blk, vs_blk, win[:, :, 0], win[:, :, 1], w_pos, rel_bias)
    new_win = win[:, wk + Tn - min(WINDOW, wk + Tn):]
    return o.reshape(Bd, Tn, ATTN_WIDTH), (jnp.stack([kc_raw, vc_raw], axis=2), jnp.stack([ks, vs], axis=2), new_win)


def conformer_conv(glu_in, hist, lp):
    u = glu_in[..., :CONV_WIDTH] * jax.nn.sigmoid(glu_in[..., CONV_WIDTH:])
    uh = jnp.concatenate([hist.astype(u.dtype), u], axis=1)
    y = lax.conv_general_dilated(uh, lp['conv_w'][:, None, :], window_strides=(1,), padding='VALID',
                                 dimension_numbers=('NWC', 'WIO', 'NWC'), feature_group_count=CONV_WIDTH)
    y = jax.nn.silu(layer_norm(y + lp['conv_b'], lp['conv_ln_g'], lp['conv_ln_b']))
    return y, uh[:, uh.shape[1] - (CONV_K - 1):]


def peer(h, lp):
    shp = h.shape
    xt = h.reshape(-1, D_MODEL)
    n = xt.shape[0]
    eidx, gate = peer_route(xt, lp['peer_q'], lp['peer_keys'])
    y = peer_experts(xt, eidx, gate, lp['expert_u'], lp['expert_v'])
    return y.reshape(shp)


def split_points():
    pts, acc = [], 0
    for s in SPLIT_SIZES[:-1]:
        acc += s
        pts.append(acc)
    return pts


def layer(x, c, conv_hist, lp, rel_bias, cache):
    B, T = x.shape[0], x.shape[1]
    mod = (jax.nn.silu(c) @ lp['w_ada'] + lp['b_ada'])[:, None, :]
    sh1, sc1, gt1, sh2, sc2, gt2 = jnp.split(mod, 6, axis=-1)
    h = rms_norm(x, lp['norm1']) * (1 + sc1) + sh1
    z = matmul(h.reshape(B * T, D_MODEL), lp['w_in']).reshape(B, T, IN_COLS)
    zq, zkc, zvc, zks, zvs, zkw, zvw, zg, zglu = jnp.split(z, split_points(), axis=-1)

    def kv(z):
        return z.reshape(B, T, N_KV, HEAD_DIM)

    q = rms_norm(zq.reshape(B, T, N_KV, GROUP, HEAD_DIM), lp['q_norm'])
    ks = rms_norm(kv(zks), lp['k_norm'][1])
    kw = rms_norm(kv(zkw), lp['k_norm'][2])
    gates = jax.nn.sigmoid(zg.reshape(B, T, N_KV, GROUP, 3))
    if cache is None:
        o_attn, attn_state = nsa_prompt(q, gates, kv(zkc), kv(zvc), ks, kv(zvs), kw, kv(zvw), lp, rel_bias)
    else:
        cache_cmp, cache_sel, cache_win, page_table = cache
        o_attn, attn_state = nsa_sample(q, gates, kv(zkc), kv(zvc), ks, kv(zvs), kw, kv(zvw),
                                        cache_cmp, cache_sel, cache_win, page_table, lp, rel_bias)
    o_conv, conv_state = conformer_conv(zglu, conv_hist, lp)
    cat = jnp.concatenate([rms_norm(o_attn, lp['attn_out_norm']), o_conv * lp['conv_out_scale']], axis=-1)
    mix = matmul(cat.reshape(B * T, D_MODEL), lp['w_out']).reshape(B, T, D_MODEL)
    x = x + gt1 * mix
    h2 = rms_norm(x, lp['norm2']) * (1 + sc2) + sh2
    x = x + gt2 * peer(h2, lp)
    return x, attn_state, conv_state


def kernel(x_prompt, x_sample, c_prompt, c_sample, cache_cmp_kv, cache_sel_kv, cache_win_kv, state_conv, page_table, rel_bias, w_ada, b_ada, norm1, w_in, q_norm, k_norm, cmp_pe, cmp_w1, cmp_b1, cmp_w2, cmp_b2, conv_w, conv_b, conv_ln_g, conv_ln_b, attn_out_norm, conv_out_scale, w_out, norm2, peer_q, peer_keys, expert_u, expert_v):
    lp = {
        'w_ada': w_ada[0], 'b_ada': b_ada[0], 'norm1': norm1[0], 'w_in': w_in[0],
        'q_norm': q_norm[0], 'k_norm': k_norm[0], 'cmp_pe': cmp_pe[0], 'cmp_w1': cmp_w1[0],
        'cmp_b1': cmp_b1[0], 'cmp_w2': cmp_w2[0], 'cmp_b2': cmp_b2[0], 'conv_w': conv_w[0],
        'conv_b': conv_b[0], 'conv_ln_g': conv_ln_g[0], 'conv_ln_b': conv_ln_b[0],
        'attn_out_norm': attn_out_norm[0], 'conv_out_scale': conv_out_scale[0], 'w_out': w_out[0],
        'norm2': norm2[0], 'peer_q': peer_q[0], 'peer_keys': peer_keys[0],
        'expert_u': expert_u[0], 'expert_v': expert_v[0],
    }
    hist0 = jnp.zeros((x_prompt.shape[0], CONV_K - 1, CONV_WIDTH), x_prompt.dtype)
    y_prompt, (a_c, a_s, a_w), cv = layer(x_prompt, c_prompt, hist0, lp, rel_bias, None)
    y_sample, (b_c, b_s, b_w), cvs = layer(x_sample, c_sample, state_conv[0], lp, rel_bias,
                                           (cache_cmp_kv[0], cache_sel_kv[0], cache_win_kv[0], page_table))
    return (y_prompt, y_sample, a_c[None], a_s[None], a_w[None], cv[None],
            b_c[None], b_s[None], b_w[None], cvs[None])
```

```python
import functools
import math

import jax
import jax.numpy as jnp
import numpy as np
from jax import lax
from jax.experimental import pallas as pl
from jax.experimental.pallas import tpu as pltpu

D_MODEL = 1024
HEAD_DIM = 64
ATTN_WIDTH = D_MODEL // 2
N_HEADS = ATTN_WIDTH // HEAD_DIM
N_KV = 2
GROUP = N_HEADS // N_KV
KV_WIDTH = N_KV * HEAD_DIM
CONV_WIDTH = D_MODEL - ATTN_WIDTH
CONV_K = 31
CMP_LEN = 32
CMP_STRIDE = 16
CMP_HIDDEN = 256
SEL_BLOCK = 64
TOP_N = 16
WINDOW = 512
N_BUCKETS = 32
MAX_DISTANCE = 128
Q_BLOCK = 64
PEER_HEADS = 8
PEER_DQ = 256
N_KEYS = 128
PEER_TOPK = 16
PEER_BLOCK = 256
FORCE_SCORE = 1.0e4
EPS = 1e-6
SPLIT_SIZES = (ATTN_WIDTH, KV_WIDTH, KV_WIDTH, KV_WIDTH, KV_WIDTH, KV_WIDTH, KV_WIDTH, 3 * N_HEADS, 2 * CONV_WIDTH)
IN_COLS = sum(SPLIT_SIZES)

LANES = 128
VMEM_LIMIT = 48 * 1024 * 1024


def _matmul_kernel(x_ref, w_ref, o_ref):
    o_ref[...] = jnp.dot(x_ref[...].astype(jnp.bfloat16), w_ref[...],
                         preferred_element_type=jnp.float32)


def matmul(x, w, tm=512):
    m, k = x.shape
    n = w.shape[1]
    n_pad = -(-n // LANES) * LANES
    wb = jnp.pad(w, ((0, 0), (0, n_pad - n))).astype(jnp.bfloat16)
    tm = min(tm, m)
    out = pl.pallas_call(
        _matmul_kernel,
        grid=(m // tm,),
        in_specs=[pl.BlockSpec((tm, k), lambda i: (i, 0)),
                  pl.BlockSpec((k, n_pad), lambda i: (0, 0))],
        out_specs=pl.BlockSpec((tm, n_pad), lambda i: (i, 0)),
        out_shape=jax.ShapeDtypeStruct((m, n_pad), jnp.float32),
        compiler_params=pltpu.CompilerParams(dimension_semantics=("arbitrary",),
                                             vmem_limit_bytes=VMEM_LIMIT),
        name="matmul",
    )(x, wb)
    return out[:, :n]


NEG = -1.0e30
SEL_MASK = 16384.0
FAR_KEYS = 512
NEAR_BLOCKS = 3
WIN_BLOCKS = WINDOW // SEL_BLOCK + 1


def _bucket_np(d):
    n = np.maximum(d, 0)
    exact = N_BUCKETS // 2
    nf = np.maximum(n, 1).astype(np.float32)
    large = exact + (np.log(nf / np.float32(exact)) / np.float32(math.log(MAX_DISTANCE / exact))
                     * np.float32(N_BUCKETS - exact)).astype(np.int32)
    return np.where(n < exact, n, np.minimum(large, N_BUCKETS - 1))


def _bias_table(rel_bias, dist, valid):
    tbl = rel_bias.astype(jnp.float32) - rel_bias[N_BUCKETS - 1].astype(jnp.float32)[None, :]
    b = jnp.transpose(tbl[_bucket_np(dist)], (0, 3, 1, 2))
    return jnp.where(valid[:, None], b, NEG)


def _nsa_prompt_kernel(q_ref, g_ref, kc_ref, vc_ref, ks_ref, vs_ref, kw_ref, vw_ref,
                       bc_ref, bsn_ref, bw_ref, cover_ref, e_ref, o_ref, *, tq):
    f32, bf16 = jnp.float32, jnp.bfloat16
    i = pl.program_id(1)
    rows = N_HEADS * tq
    nt = (((1,), (1,)), ((), ()))

    q = q_ref[0]
    lane_half = lax.broadcasted_iota(jnp.int32, (tq, 2 * HEAD_DIM), 1) // HEAD_DIM
    parts = []
    for h in range(N_HEADS):
        blk = q[:, (h // 2) * 2 * HEAD_DIM:(h // 2 + 1) * 2 * HEAD_DIM]
        grp = h // GROUP
        if h % 2 != grp:
            blk = pltpu.roll(blk, HEAD_DIM, 1)
        parts.append(jnp.where(lane_half == grp, blk, 0.0))
    qpad = jnp.concatenate(parts, axis=0).astype(bf16)

    def heads(x):
        return jnp.concatenate([x[:tq]] * GROUP + [x[tq:]] * GROUP, axis=0)

    tbl_c = bc_ref[0].reshape(rows, bc_ref.shape[-1])
    s = lax.dot_general(qpad, kc_ref[0], nt, preferred_element_type=f32) + tbl_c
    ok = tbl_c > 0.5 * NEG
    m = jnp.max(s, axis=1, keepdims=True)
    e = jnp.where(ok, jnp.exp(s - m), 0.0)
    d = jnp.sum(e, axis=1, keepdims=True)
    pc = e / jnp.where(d > 0.0, d, 1.0)
    o_cmp = jnp.dot(pc.astype(bf16), vc_ref[0], preferred_element_type=f32)

    pg = []
    for grp in range(N_KV):
        acc = pc[grp * GROUP * tq:(grp * GROUP + 1) * tq]
        for r in range(1, GROUP):
            acc = acc + pc[(grp * GROUP + r) * tq:(grp * GROUP + r + 1) * tq]
        pg.append(acc)
    pg = jnp.concatenate(pg, axis=0)
    pg_hi = pg.astype(bf16)
    pg_lo = (pg - pg_hi.astype(f32)).astype(bf16)
    cover = cover_ref[...]
    imp = (jnp.dot(pg_hi, cover, preferred_element_type=f32)
           + jnp.dot(pg_lo, cover, preferred_element_type=f32))
    ns = imp.shape[1]
    j = lax.broadcasted_iota(jnp.int32, imp.shape, 1)
    forced = (j == 0) | (j == i) | (j == i - 1)
    imp = jnp.where(forced, imp + FORCE_SCORE, imp)
    imp = jnp.where(j > i, -FORCE_SCORE, imp)
    cnt = jnp.zeros(imp.shape, jnp.int32)
    for jp in range(ns):
        col = imp[:, jp:jp + 1]
        ahead = (col > imp) | ((col == imp) & (j > jp))
        cnt = cnt + ahead.astype(jnp.int32)
    sel = cnt < TOP_N

    nb0 = jnp.maximum(i - (NEAR_BLOCKS - 1), 0)
    near = NEAR_BLOCKS * SEL_BLOCK
    k0 = pl.multiple_of(nb0 * SEL_BLOCK, SEL_BLOCK)
    sel_nb0 = jnp.sum(jnp.where((j == nb0) & sel, 1.0, 0.0), axis=1, keepdims=True)
    lane = lax.broadcasted_iota(jnp.int32, (rows, near), 1)
    s = lax.dot_general(qpad, ks_ref[0, pl.ds(k0, near), :], nt, preferred_element_type=f32)
    s = s + bsn_ref[0].reshape(rows, near)
    s = s + jnp.where(lane < SEL_BLOCK, (heads(sel_nb0) - 1.0) * SEL_MASK, 0.0)
    m = jnp.max(s, axis=1, keepdims=True)
    p = jnp.exp(s - m)
    l = jnp.sum(p, axis=1, keepdims=True)
    acc = jnp.dot(p.astype(bf16), vs_ref[0, pl.ds(k0, near), :], preferred_element_type=f32)

    sel_far = heads(jnp.where(sel & (j < nb0), 0.0, -1.0)).astype(bf16)
    n_far = (nb0 * SEL_BLOCK + FAR_KEYS - 1) // FAR_KEYS

    def far_body(c, carry):
        m, l, acc = carry
        c0 = pl.multiple_of(c * FAR_KEYS, FAR_KEYS)
        s = lax.dot_general(qpad, ks_ref[0, pl.ds(c0, FAR_KEYS), :], nt, preferred_element_type=f32)
        s = s + jnp.dot(sel_far, e_ref[c], preferred_element_type=f32)
        m_new = jnp.maximum(m, jnp.max(s, axis=1, keepdims=True))
        a = jnp.exp(m - m_new)
        p = jnp.exp(s - m_new)
        l = a * l + jnp.sum(p, axis=1, keepdims=True)
        acc = a * acc + jnp.dot(p.astype(bf16), vs_ref[0, pl.ds(c0, FAR_KEYS), :],
                                preferred_element_type=f32)
        return m_new, l, acc

    m, l, acc = lax.fori_loop(0, n_far, far_body, (m, l, acc))
    o_sel = acc / l

    wk = WIN_BLOCKS * SEL_BLOCK
    w0 = pl.multiple_of(jnp.maximum(i - (WIN_BLOCKS - 1), 0) * SEL_BLOCK, SEL_BLOCK)
    s = lax.dot_general(qpad, kw_ref[0, pl.ds(w0, wk), :], nt, preferred_element_type=f32)
    s = s + bw_ref[0].reshape(rows, wk)
    m = jnp.max(s, axis=1, keepdims=True)
    p = jnp.exp(s - m)
    l = jnp.sum(p, axis=1, keepdims=True)
    o_win = jnp.dot(p.astype(bf16), vw_ref[0, pl.ds(w0, wk), :], preferred_element_type=f32) / l

    gates = g_ref[0]
    for h in range(N_HEADS):
        grp = h // GROUP
        r0, r1 = h * tq, (h + 1) * tq
        c0, c1 = grp * HEAD_DIM, (grp + 1) * HEAD_DIM
        o_h = (gates[:, 3 * h:3 * h + 1] * o_cmp[r0:r1, c0:c1]
               + gates[:, 3 * h + 1:3 * h + 2] * o_sel[r0:r1, c0:c1]
               + gates[:, 3 * h + 2:3 * h + 3] * o_win[r0:r1, c0:c1])
        o_ref[0, :, h * HEAD_DIM:(h + 1) * HEAD_DIM] = o_h


def nsa_prompt_attention(q, gates, kc, vc, ks, vs, kw, vw, rel_bias):
    B, T = q.shape[0], q.shape[1]
    tq = SEL_BLOCK
    nq = T // tq
    ns = T // SEL_BLOCK
    nc = kc.shape[1]
    ncp = -(-nc // LANES) * LANES
    bf16 = jnp.bfloat16
    scale = HEAD_DIM ** -0.5
    assert T % FAR_KEYS == 0 and T >= WIN_BLOCKS * SEL_BLOCK

    tl = np.arange(tq)[None, :, None]
    r = tq // CMP_STRIDE
    span = r * (nq - 1) + ncp
    y = np.arange(span)[None, None, :]
    dy = tl + CMP_STRIDE * (y - (ncp - 1)) - (CMP_LEN - 1)
    master = _bias_table(rel_bias, dy, dy >= 0)[0][:, :, ::-1]
    pad_col = jnp.arange(ncp) < nc
    bc = jnp.stack([jnp.where(pad_col, master[:, :, r * (nq - 1 - i):r * (nq - 1 - i) + ncp], NEG)
                    for i in range(nq)])
    vv = np.arange(NEAR_BLOCKS)[:, None, None]
    dn = vv * tq + tl - np.arange(NEAR_BLOCKS * SEL_BLOCK)[None, None, :]
    bsn = _bias_table(rel_bias, dn, dn >= 0)
    vw_ = np.arange(WIN_BLOCKS)[:, None, None]
    dw = vw_ * tq + tl - np.arange(WIN_BLOCKS * SEL_BLOCK)[None, None, :]
    bw = _bias_table(rel_bias, dw, (dw >= 0) & (dw < WINDOW))

    ci = np.arange(ncp)[:, None]
    sj = np.arange(ns)[None, :]
    cover = ((ci * CMP_STRIDE < (sj + 1) * SEL_BLOCK) & (ci * CMP_STRIDE + CMP_LEN > sj * SEL_BLOCK)
             & (ci < nc))
    cover = jnp.asarray(cover.astype(np.float32), bf16)
    pos = np.arange(T).reshape(T // FAR_KEYS, 1, FAR_KEYS)
    expand = jnp.asarray(np.where(pos // SEL_BLOCK == np.arange(ns)[None, :, None], SEL_MASK, 0.0)
                         .astype(np.float32), bf16)

    padc = ((0, 0), (0, ncp - nc), (0, 0))
    seq = lambda b, i: (b, 0, 0)
    return pl.pallas_call(
        functools.partial(_nsa_prompt_kernel, tq=tq),
        grid=(B, nq),
        in_specs=[
            pl.BlockSpec((1, tq, ATTN_WIDTH), lambda b, i: (b, i, 0)),
            pl.BlockSpec((1, tq, 3 * N_HEADS), lambda b, i: (b, i, 0)),
            pl.BlockSpec((1, ncp, KV_WIDTH), seq),
            pl.BlockSpec((1, ncp, KV_WIDTH), seq),
            pl.BlockSpec((1, T, KV_WIDTH), seq),
            pl.BlockSpec((1, T, KV_WIDTH), seq),
            pl.BlockSpec((1, T, KV_WIDTH), seq),
            pl.BlockSpec((1, T, KV_WIDTH), seq),
            pl.BlockSpec((1, N_HEADS, tq, ncp), lambda b, i: (i, 0, 0, 0)),
            pl.BlockSpec((1, N_HEADS, tq, NEAR_BLOCKS * SEL_BLOCK),
                         lambda b, i: (jnp.minimum(i, NEAR_BLOCKS - 1), 0, 0, 0)),
            pl.BlockSpec((1, N_HEADS, tq, WIN_BLOCKS * SEL_BLOCK),
                         lambda b, i: (jnp.minimum(i, WIN_BLOCKS - 1), 0, 0, 0)),
            pl.BlockSpec((ncp, ns), lambda b, i: (0, 0)),
            pl.BlockSpec((T // FAR_KEYS, ns, FAR_KEYS), lambda b, i: (0, 0, 0)),
        ],
        out_specs=pl.BlockSpec((1, tq, ATTN_WIDTH), lambda b, i: (b, i, 0)),
        out_shape=jax.ShapeDtypeStruct((B, T, ATTN_WIDTH), jnp.float32),
        compiler_params=pltpu.CompilerParams(dimension_semantics=("arbitrary", "arbitrary"),
                                             vmem_limit_bytes=VMEM_LIMIT),
        name="nsa_prompt",
    )(q * scale, gates, jnp.pad(kc, padc).astype(bf16), jnp.pad(vc, padc).astype(bf16),
      ks.astype(bf16), vs.astype(bf16), kw.astype(bf16), vw.astype(bf16), bc, bsn, bw, cover, expand)


N_SUB = 2 * PEER_HEADS
HALF_DQ = PEER_DQ // 2
N_PAIRS = PEER_HEADS * PEER_TOPK


def _top_rounds(s, row, n):
    width = s.shape[0]
    k = lax.broadcasted_iota(jnp.int32, (n, s.shape[1]), 0)
    vals = jnp.zeros((n, s.shape[1]), jnp.float32)
    ids = jnp.zeros((n, s.shape[1]), jnp.float32)
    for a in range(n):
        m = jnp.max(s, axis=0, keepdims=True)
        ix = jnp.min(jnp.where(s == m, row, float(width)), axis=0, keepdims=True)
        vals = jnp.where(k == a, m, vals)
        ids = jnp.where(k == a, ix, ids)
        s = jnp.where(row == ix, -jnp.inf, s)
    return vals, ids


def _peer_route_kernel(h_ref, wq_ref, keys_ref, idx_ref, g_ref, q_scr, e_scr, w_scr):
    f32, bf16 = jnp.float32, jnp.bfloat16
    tm = h_ref.shape[0]
    nt = (((1,), (1,)), ((), ()))
    q = jnp.dot(h_ref[...].astype(bf16), wq_ref[...], preferred_element_type=f32)
    for k in range(N_SUB):
        q_scr[k] = q[:, k * HALF_DQ:(k + 1) * HALF_DQ].astype(bf16)
    row = lax.broadcasted_iota(jnp.int32, (N_KEYS, tm), 0).astype(f32)
    n_cand = PEER_TOPK * PEER_TOPK
    row2 = lax.broadcasted_iota(jnp.int32, (n_cand, tm), 0).astype(f32)

    def head(h, carry):
        s1, i1 = _top_rounds(lax.dot_general(keys_ref[2 * h], q_scr[2 * h], nt, preferred_element_type=f32),
                             row, PEER_TOPK)
        s2, i2 = _top_rounds(lax.dot_general(keys_ref[2 * h + 1], q_scr[2 * h + 1], nt,
                                             preferred_element_type=f32), row, PEER_TOPK)
        cand = jnp.concatenate([s1[a:a + 1] + s2 for a in range(PEER_TOPK)], axis=0)
        cidx = jnp.concatenate([i1[a:a + 1] * float(N_KEYS) + i2 for a in range(PEER_TOPK)], axis=0)
        k = lax.broadcasted_iota(jnp.int32, (PEER_TOPK, tm), 0)
        top = jnp.zeros((PEER_TOPK, tm), f32)
        eid = jnp.zeros((PEER_TOPK, tm), f32)
        for a in range(PEER_TOPK):
            m = jnp.max(cand, axis=0, keepdims=True)
            pos = jnp.min(jnp.where(cand == m, row2, float(n_cand)), axis=0, keepdims=True)
            hit = row2 == pos
            e = jnp.max(jnp.where(hit, cidx, -1.0), axis=0, keepdims=True)
            top = jnp.where(k == a, m, top)
            eid = jnp.where(k == a, e, eid)
            cand = jnp.where(hit, -jnp.inf, cand)
        ex = jnp.exp(top - top[0:1])
        r0 = pl.multiple_of(h * PEER_TOPK, PEER_TOPK)
        e_scr[pl.ds(r0, PEER_TOPK), :] = eid
        w_scr[pl.ds(r0, PEER_TOPK), :] = ex / jnp.sum(ex, axis=0, keepdims=True)
        return carry

    lax.fori_loop(0, PEER_HEADS, head, 0)
    idx_ref[...] = e_scr[...].T.astype(jnp.int32)
    g_ref[...] = w_scr[...].T


def peer_route(h2, peer_q, peer_keys, tm=128):
    n = h2.shape[0]
    tm = min(tm, n)
    bf16 = jnp.bfloat16
    keys = peer_keys.reshape(N_SUB, N_KEYS, HALF_DQ).astype(bf16)
    return pl.pallas_call(
        _peer_route_kernel,
        grid=(n // tm,),
        in_specs=[pl.BlockSpec((tm, D_MODEL), lambda i: (i, 0)),
                  pl.BlockSpec((D_MODEL, PEER_HEADS * PEER_DQ), lambda i: (0, 0)),
                  pl.BlockSpec((N_SUB, N_KEYS, HALF_DQ), lambda i: (0, 0, 0))],
        out_specs=[pl.BlockSpec((tm, N_PAIRS), lambda i: (i, 0)),
                   pl.BlockSpec((tm, N_PAIRS), lambda i: (i, 0))],
        out_shape=[jax.ShapeDtypeStruct((n, N_PAIRS), jnp.int32),
                   jax.ShapeDtypeStruct((n, N_PAIRS), jnp.float32)],
        scratch_shapes=[pltpu.VMEM((N_SUB, tm, HALF_DQ), bf16),
                        pltpu.VMEM((N_PAIRS, tm), jnp.float32),
                        pltpu.VMEM((N_PAIRS, tm), jnp.float32)],
        compiler_params=pltpu.CompilerParams(dimension_semantics=("arbitrary",),
                                             vmem_limit_bytes=VMEM_LIMIT),
        name="peer_route",
    )(h2, peer_q.astype(bf16), keys)


def _gelu_tanh(x):
    return 0.5 * x * (1.0 + jnp.tanh(math.sqrt(2.0 / math.pi) * (x + 0.044715 * (x * x * x))))


N_SLOTS = 8
ROW_TILES = 2 * D_MODEL // LANES
U_TILES = D_MODEL // LANES


def _peer_expert_kernel(idx_ref, nxt_ref, x_ref, g_ref, uv_ref, y_ref, *scratch):
    tb = x_ref.shape[0]
    i = pl.program_id(0)
    last = pl.num_programs(0) - 1
    ahead = N_SLOTS - 1
    bufs, sem = scratch[:N_SLOTS], scratch[N_SLOTS]

    def issue(ids, t, slot):
        for k in range(N_PAIRS):
            pltpu.make_async_copy(uv_ref.at[ids[t, k]], bufs[slot].at[k // 8, :, k % 8, :], sem.at[slot]).start()

    def wait_slot(slot):
        pltpu.make_async_copy(bufs[slot], bufs[slot], sem.at[slot]).wait()

    eye = (lax.broadcasted_iota(jnp.int32, (N_PAIRS, N_PAIRS), 0)
           == lax.broadcasted_iota(jnp.int32, (N_PAIRS, N_PAIRS), 1))

    def compute(t, slot):
        buf = bufs[slot]

        def chunk(c):
            return buf[:, c].reshape(N_PAIRS, LANES)

        x = x_ref[t]
        acc = chunk(0) * x[0:1, :]
        for s in range(1, U_TILES):
            acc = acc + chunk(s) * x[s:s + 1, :]
        a = jnp.sum(acc, axis=1, keepdims=True)
        g_col = jnp.sum(jnp.where(eye, g_ref[pl.ds(t, 1), :], 0.0), axis=1, keepdims=True)
        w = g_col * _gelu_tanh(a)
        y = [jnp.sum(w * chunk(U_TILES + s), axis=0, keepdims=True) for s in range(U_TILES)]
        y_ref[pl.ds(t, 1), :] = jnp.concatenate(y, axis=1)

    @pl.when(i == 0)
    def _():
        for t in range(ahead):
            issue(idx_ref, t, t)

    def group(j, c):
        for u in range(N_SLOTS):
            t = j * N_SLOTS + u
            wait_slot(u)
            issue(idx_ref, t + ahead, (u + ahead) % N_SLOTS)
            compute(t, u)
        return c

    n_groups = tb // N_SLOTS
    lax.fori_loop(0, n_groups - 1, group, 0)
    for u in range(N_SLOTS):
        t = tb - N_SLOTS + u
        wait_slot(u)
        if u == 0:
            issue(idx_ref, t + ahead, ahead)
        else:
            issue(nxt_ref, u - 1, u - 1)
        compute(t, u)

    @pl.when(i == last)
    def _():
        for u in range(ahead):
            wait_slot(u)


def peer_experts(h2, eidx, gate, expert_u, expert_v, tb=64):
    n = h2.shape[0]
    tb = min(tb, n)
    nb = n // tb
    e = expert_u.shape[0]
    uv = jnp.concatenate([expert_u.reshape(e, U_TILES, LANES), expert_v.reshape(e, U_TILES, LANES)], axis=1)
    return pl.pallas_call(
        _peer_expert_kernel,
        grid=(nb,),
        in_specs=[pl.BlockSpec((tb, N_PAIRS), lambda i: (i, 0), memory_space=pltpu.SMEM),
                  pl.BlockSpec((tb, N_PAIRS), lambda i: (jnp.minimum(i + 1, nb - 1), 0), memory_space=pltpu.SMEM),
                  pl.BlockSpec((tb, U_TILES, LANES), lambda i: (i, 0, 0)),
                  pl.BlockSpec((tb, N_PAIRS), lambda i: (i, 0)),
                  pl.BlockSpec(memory_space=pl.ANY)],
        out_specs=pl.BlockSpec((tb, D_MODEL), lambda i: (i, 0)),
        out_shape=jax.ShapeDtypeStruct((n, D_MODEL), jnp.float32),
        scratch_shapes=[pltpu.VMEM((N_PAIRS // 8, ROW_TILES, 8, LANES), jnp.float32)] * N_SLOTS
        + [pltpu.SemaphoreType.DMA((N_SLOTS,))],
        compiler_params=pltpu.CompilerParams(dimension_semantics=("arbitrary",),
                                             vmem_limit_bytes=VMEM_LIMIT),
        name="peer_experts",
    )(eidx, eidx, h2.reshape(n, U_TILES, LANES), gate, uv)


def rms_norm(x, g):
    y = x * lax.rsqrt(jnp.mean(x * x, axis=-1, keepdims=True) + EPS)
    return y * g


def layer_norm(x, g, b):
    mu = jnp.mean(x, axis=-1, keepdims=True)
    var = jnp.mean(jnp.square(x - mu), axis=-1, keepdims=True)
    return (x - mu) * lax.rsqrt(var + EPS) * g + b


def masked_softmax(logits, mask, axis):
    neg = jnp.where(mask, logits, -jnp.inf)
    m = jnp.max(neg, axis=axis, keepdims=True)
    m = jnp.where(jnp.isfinite(m), m, 0.0)
    e = jnp.where(mask, jnp.exp(neg - m), 0.0)
    d = jnp.sum(e, axis=axis, keepdims=True)
    return e / jnp.where(d > 0.0, d, 1.0)


def t5_bucket(dist):
    n = jnp.maximum(dist, 0)
    exact = N_BUCKETS // 2
    nf = jnp.maximum(n, 1).astype(jnp.float32)
    large = exact + (jnp.log(nf / exact) / math.log(MAX_DISTANCE / exact) * (N_BUCKETS - exact)).astype(jnp.int32)
    return jnp.where(n < exact, n, jnp.minimum(large, N_BUCKETS - 1))


def compress(rows, pe, w1, b1, w2, b2):
    B, L = rows.shape[0], rows.shape[1]
    ratio = CMP_LEN // CMP_STRIDE
    n_chunks = L // CMP_STRIDE
    nc = n_chunks - ratio + 1
    r = rows[:, :n_chunks * CMP_STRIDE].reshape(B, n_chunks, CMP_STRIDE, N_KV, HEAD_DIM)
    w1c = w1.reshape(ratio, CMP_STRIDE, HEAD_DIM, CMP_HIDDEN)
    parts = jnp.einsum('bcsgd,jsdh->jbcgh', r, w1c)
    hid = b1 + pe.reshape(-1) @ w1
    for j in range(ratio):
        hid = hid + parts[j, :, j:j + nc]
    out = jax.nn.gelu(hid) @ w2 + b2
    end = jnp.arange(nc, dtype=jnp.int32) * CMP_STRIDE + (CMP_LEN - 1)
    return out, end


def compress_branch(rows, lp, which):
    return compress(rows, lp['cmp_pe'][which], lp['cmp_w1'][which], lp['cmp_b1'][which],
                    lp['cmp_w2'][which], lp['cmp_b2'][which])


def to_blocks(rows):
    B, L = rows.shape[0], rows.shape[1]
    ns = -(-L // SEL_BLOCK)
    rows = jnp.pad(rows, ((0, 0), (0, ns * SEL_BLOCK - L), (0, 0), (0, 0)))
    return jnp.transpose(rows.reshape(B, ns, SEL_BLOCK, N_KV, HEAD_DIM), (0, 3, 1, 2, 4))


def nsa_block(q, t_pos, gates, kc, vc, kc_end, ks_blk, vs_blk, kw, vw, w_pos, rel_bias):
    B = q.shape[0]
    scale = HEAD_DIM ** -0.5
    tbl = rel_bias.astype(jnp.float32).reshape(N_BUCKETS, N_KV, GROUP)
    lc = jnp.einsum('bqgrd,bcgd->bqgrc', q, kc).astype(jnp.float32) * scale
    dc = t_pos[:, None] - kc_end[None, :]
    lc = lc + jnp.transpose(tbl[t5_bucket(dc)], (0, 2, 3, 1))[None]
    pc = masked_softmax(lc, (dc >= 0)[None, :, None, None, :], axis=-1)
    o_cmp = jnp.einsum('bqgrc,bcgd->bqgrd', pc.astype(vc.dtype), vc)
    nc, ns = kc.shape[1], ks_blk.shape[2]
    ci = jnp.arange(nc)[:, None]
    sj = jnp.arange(ns)[None, :]
    cover = ((ci * CMP_STRIDE < (sj + 1) * SEL_BLOCK) & (ci * CMP_STRIDE + CMP_LEN > sj * SEL_BLOCK)).astype(jnp.float32)
    imp = jnp.einsum('bqgrc,cj->bqgj', pc, cover)
    cur = (t_pos // SEL_BLOCK)[:, None]
    forced = (sj == 0) | (sj == cur) | (sj == cur - 1)
    valid = sj * SEL_BLOCK <= t_pos[:, None]
    imp = jnp.where(forced[None, :, None, :], imp + FORCE_SCORE, imp)
    imp = jnp.where(valid[None, :, None, :], imp, -FORCE_SCORE)
    _, idx = lax.top_k(imp, min(TOP_N, ns))
    bi = jnp.arange(B)[:, None, None, None]
    gi = jnp.arange(N_KV)[None, None, :, None]
    ks_g = ks_blk[bi, gi, idx]
    vs_g = vs_blk[bi, gi, idx]
    ls = jnp.einsum('bqgrd,bqgnkd->bqgrnk', q, ks_g).astype(jnp.float32) * scale
    pos = idx[..., None] * SEL_BLOCK + jnp.arange(SEL_BLOCK, dtype=jnp.int32)
    ds = t_pos[None, :, None, None, None] - pos
    tblk = jnp.transpose(tbl, (1, 0, 2))
    bs = tblk[jnp.arange(N_KV)[None, None, :, None, None], t5_bucket(ds)]
    ls = ls + jnp.moveaxis(bs, -1, 3)
    ps = masked_softmax(ls, (ds >= 0)[:, :, :, None], axis=(-2, -1))
    o_sel = jnp.einsum('bqgrnk,bqgnkd->bqgrd', ps.astype(vs_g.dtype), vs_g)
    lw = jnp.einsum('bqgrd,bwgd->bqgrw', q, kw).astype(jnp.float32) * scale
    dw = t_pos[:, None] - w_pos[None, :]
    lw = lw + jnp.transpose(tbl[t5_bucket(dw)], (0, 2, 3, 1))[None]
    mw = (dw >= 0) & (dw < WINDOW) & (w_pos >= 0)[None, :]
    pw = masked_softmax(lw, mw[None, :, None, None, :], axis=-1)
    o_win = jnp.einsum('bqgrw,bwgd->bqgrd', pw.astype(vw.dtype), vw)
    return gates[..., 0:1] * o_cmp + gates[..., 1:2] * o_sel + gates[..., 2:3] * o_win


def nsa_prompt(q, gates, kc_raw, vc_raw, ks, vs, kw, vw, lp, rel_bias):
    B, T = q.shape[0], q.shape[1]
    kc, kc_end = compress_branch(kc_raw, lp, 0)
    kc = rms_norm(kc, lp['k_norm'][0])
    vc, _ = compress_branch(vc_raw, lp, 1)
    flat = lambda a: a.reshape(B, a.shape[1], -1)
    o = nsa_prompt_attention(flat(q), flat(gates), flat(kc), flat(vc), flat(ks), flat(vs), flat(kw), flat(vw),
                             rel_bias)
    new_win = jnp.stack([kw, vw], axis=2)[:, T - min(WINDOW, T):]
    return o, (jnp.stack([kc_raw, vc_raw], axis=2), jnp.stack([ks, vs], axis=2), new_win)


def nsa_sample(q, gates, kc_raw, vc_raw, ks, vs, kw, vw, cache_cmp, cache_sel, cache_win, page_table, lp, rel_bias):
    Bd, Tn = q.shape[0], q.shape[1]
    past = page_table.shape[1] * cache_cmp.shape[1]
    t_pos = past + jnp.arange(Tn, dtype=jnp.int32)
    past_cmp = cache_cmp[page_table].reshape(Bd, past, 2, N_KV, HEAD_DIM)
    past_sel = cache_sel[page_table].reshape(Bd, past, 2, N_KV, HEAD_DIM)
    kc, kc_end = compress_branch(jnp.concatenate([past_cmp[:, :, 0], kc_raw], axis=1), lp, 0)
    kc = rms_norm(kc, lp['k_norm'][0])
    vc, _ = compress_branch(jnp.concatenate([past_cmp[:, :, 1], vc_raw], axis=1), lp, 1)
    ks_blk = to_blocks(jnp.concatenate([past_sel[:, :, 0], ks], axis=1))
    vs_blk = to_blocks(jnp.concatenate([past_sel[:, :, 1], vs], axis=1))
    wk = cache_win.shape[1]
    win = jnp.concatenate([cache_win, jnp.stack([kw, vw], axis=2)], axis=1)
    w_pos = past - wk + jnp.arange(wk + Tn, dtype=jnp.int32)
    o = nsa_block(q, t_pos, gates, kc, vc, kc_end, ks_blk, vs_blk, win[:, :, 0], win[:, :, 1], w_pos, rel_bias)
    new_win = win[:, wk + Tn - min(WINDOW, wk + Tn):]
    return o.reshape(Bd, Tn, ATTN_WIDTH), (jnp.stack([kc_raw, vc_raw], axis=2), jnp.stack([ks, vs], axis=2), new_win)


def conformer_conv(glu_in, hist, lp):
    u = glu_in[..., :CONV_WIDTH] * jax.nn.sigmoid(glu_in[..., CONV_WIDTH:])
    uh = jnp.concatenate([hist.astype(u.dtype), u], axis=1)
    y = lax.conv_general_dilated(uh, lp['conv_w'][:, None, :], window_strides=(1,), padding='VALID',
                                 dimension_numbers=('NWC', 'WIO', 'NWC'), feature_group_count=CONV_WIDTH)
    y = jax.nn.silu(layer_norm(y + lp['conv_b'], lp['conv_ln_g'], lp['conv_ln_b']))
    return y, uh[:, uh.shape[1] - (CONV_K - 1):]


def peer(h, lp):
    shp = h.shape
    xt = h.reshape(-1, D_MODEL)
    n = xt.shape[0]
    eidx, gate = peer_route(xt, lp['peer_q'], lp['peer_keys'])
    y = peer_experts(xt, eidx, gate, lp['expert_u'], lp['expert_v'])
    return y.reshape(shp)


def split_points():
    pts, acc = [], 0
    for s in SPLIT_SIZES[:-1]:
        acc += s
        pts.append(acc)
    return pts


def layer(x, c, conv_hist, lp, rel_bias, cache):
    B, T = x.shape[0], x.shape[1]
    mod = (jax.nn.silu(c) @ lp['w_ada'] + lp['b_ada'])[:, None, :]
    sh1, sc1, gt1, sh2, sc2, gt2 = jnp.split(mod, 6, axis=-1)
    h = rms_norm(x, lp['norm1']) * (1 + sc1) + sh1
    z = matmul(h.reshape(B * T, D_MODEL), lp['w_in']).reshape(B, T, IN_COLS)
    zq, zkc, zvc, zks, zvs, zkw, zvw, zg, zglu = jnp.split(z, split_points(), axis=-1)

    def kv(z):
        return z.reshape(B, T, N_KV, HEAD_DIM)

    q = rms_norm(zq.reshape(B, T, N_KV, GROUP, HEAD_DIM), lp['q_norm'])
    ks = rms_norm(kv(zks), lp['k_norm'][1])
    kw = rms_norm(kv(zkw), lp['k_norm'][2])
    gates = jax.nn.sigmoid(zg.reshape(B, T, N_KV, GROUP, 3))
    if cache is None:
        o_attn, attn_state = nsa_prompt(q, gates, kv(zkc), kv(zvc), ks, kv(zvs), kw, kv(zvw), lp, rel_bias)
    else:
        cache_cmp, cache_sel, cache_win, page_table = cache
        o_attn, attn_state = nsa_sample(q, gates, kv(zkc), kv(zvc), ks, kv(zvs), kw, kv(zvw),
                                        cache_cmp, cache_sel, cache_win, page_table, lp, rel_bias)
    o_conv, conv_state = conformer_conv(zglu, conv_hist, lp)
    cat = jnp.concatenate([rms_norm(o_attn, lp['attn_out_norm']), o_conv * lp['conv_out_scale']], axis=-1)
    mix = matmul(cat.reshape(B * T, D_MODEL), lp['w_out']).reshape(B, T, D_MODEL)
    x = x + gt1 * mix
    h2 = rms_norm(x, lp['norm2']) * (1 + sc2) + sh2
    x = x + gt2 * peer(h2, lp)
    return x, attn_state, conv_state


def kernel(x_prompt, x_sample, c_prompt, c_sample, cache_cmp_kv, cache_sel_kv, cache_win_kv, state_conv, page_table, rel_bias, w_ada, b_ada, norm1, w_in, q_norm, k_norm, cmp_pe, cmp_w1, cmp_b1, cmp_w2, cmp_b2, conv_w, conv_b, conv_ln_g, conv_ln_b, attn_out_norm, conv_out_scale, w_out, norm2, peer_q, peer_keys, expert_u, expert_v):
    lp = {
        'w_ada': w_ada[0], 'b_ada': b_ada[0], 'norm1': norm1[0], 'w_in': w_in[0],
        'q_norm': q_norm[0], 'k_norm': k_norm[0], 'cmp_pe': cmp_pe[0], 'cmp_w1': cmp_w1[0],
        'cmp_b1': cmp_b1[0], 'cmp_w2': cmp_w2[0], 'cmp_b2': cmp_b2[0], 'conv_w': conv_w[0],
        'conv_b': conv_b[0], 'conv_ln_g': conv_ln_g[0], 'conv_ln_b': conv_ln_b[0],
        'attn_out_norm': attn_out_norm[0], 'conv_out_scale': conv_out_scale[0], 'w_out': w_out[0],
        'norm2': norm2[0], 'peer_q': peer_q[0], 'peer_keys': peer_keys[0],
        'expert_u': expert_u[0], 'expert_v': expert_v[0],
    }
    hist0 = jnp.zeros((x_prompt.shape[0], CONV_K - 1, CONV_WIDTH), x_prompt.dtype)
    y_prompt, (a_c, a_s, a_w), cv = layer(x_prompt, c_prompt, hist0, lp, rel_bias, None)
    y_sample, (b_c, b_s, b_w), cvs = layer(x_sample, c_sample, state_conv[0], lp, rel_bias,
                                           (cache_cmp_kv[0], cache_sel_kv[0], cache_win_kv[0], page_table))
    return (y_prompt, y_sample, a_c[None], a_s[None], a_w[None], cv[None],
            b_c[None], b_s[None], b_w[None], cvs[None])
```

```python
import functools
import math

import jax
import jax.numpy as jnp
import numpy as np
from jax import lax
from jax.experimental import pallas as pl
from jax.experimental.pallas import tpu as pltpu

D_MODEL = 1024
HEAD_DIM = 64
ATTN_WIDTH = D_MODEL // 2
N_HEADS = ATTN_WIDTH // HEAD_DIM
N_KV = 2
GROUP = N_HEADS // N_KV
KV_WIDTH = N_KV * HEAD_DIM
CONV_WIDTH = D_MODEL - ATTN_WIDTH
CONV_K = 31
CMP_LEN = 32
CMP_STRIDE = 16
CMP_HIDDEN = 256
SEL_BLOCK = 64
TOP_N = 16
WINDOW = 512
N_BUCKETS = 32
MAX_DISTANCE = 128
Q_BLOCK = 64
PEER_HEADS = 8
PEER_DQ = 256
N_KEYS = 128
PEER_TOPK = 16
PEER_BLOCK = 256
FORCE_SCORE = 1.0e4
EPS = 1e-6
SPLIT_SIZES = (ATTN_WIDTH, KV_WIDTH, KV_WIDTH, KV_WIDTH, KV_WIDTH, KV_WIDTH, KV_WIDTH, 3 * N_HEADS, 2 * CONV_WIDTH)
IN_COLS = sum(SPLIT_SIZES)

LANES = 128
VMEM_LIMIT = 48 * 1024 * 1024


def _matmul_kernel(x_ref, w_ref, o_ref):
    o_ref[...] = jnp.dot(x_ref[...].astype(jnp.bfloat16), w_ref[...],
                         preferred_element_type=jnp.float32)


def matmul(x, w, tm=512):
    m, k = x.shape
    n = w.shape[1]
    n_pad = -(-n // LANES) * LANES
    wb = jnp.pad(w, ((0, 0), (0, n_pad - n))).astype(jnp.bfloat16)
    tm = min(tm, m)
    out = pl.pallas_call(
        _matmul_kernel,
        grid=(m // tm,),
        in_specs=[pl.BlockSpec((tm, k), lambda i: (i, 0)),
                  pl.BlockSpec((k, n_pad), lambda i: (0, 0))],
        out_specs=pl.BlockSpec((tm, n_pad), lambda i: (i, 0)),
        out_shape=jax.ShapeDtypeStruct((m, n_pad), jnp.float32),
        compiler_params=pltpu.CompilerParams(dimension_semantics=("arbitrary",),
                                             vmem_limit_bytes=VMEM_LIMIT),
        name="matmul",
    )(x, wb)
    return out[:, :n]


NEG = -1.0e30
SEL_MASK = 16384.0
FAR_KEYS = 512
NEAR_BLOCKS = 3
WIN_BLOCKS = WINDOW // SEL_BLOCK + 1


def _bucket_np(d):
    n = np.maximum(d, 0)
    exact = N_BUCKETS // 2
    nf = np.maximum(n, 1).astype(np.float32)
    large = exact + (np.log(nf / np.float32(exact)) / np.float32(math.log(MAX_DISTANCE / exact))
                     * np.float32(N_BUCKETS - exact)).astype(np.int32)
    return np.where(n < exact, n, np.minimum(large, N_BUCKETS - 1))


def _bias_table(rel_bias, dist, valid):
    tbl = rel_bias.astype(jnp.float32) - rel_bias[N_BUCKETS - 1].astype(jnp.float32)[None, :]
    b = jnp.transpose(tbl[_bucket_np(dist)], (0, 3, 1, 2))
    return jnp.where(valid[:, None], b, NEG)


def _group_sum(x, grp, t):
    acc = x[grp * GROUP * t:(grp * GROUP + 1) * t]
    for r in range(1, GROUP):
        acc = acc + x[(grp * GROUP + r) * t:(grp * GROUP + r + 1) * t]
    return acc


def _nsa_prompt_kernel(q_ref, g_ref, kc_ref, vc_ref, ks_ref, vs_ref, kw_ref, vw_ref,
                       bc_ref, bsn_ref, bw_ref, cover_ref, e_ref, o_ref, *, tq):
    f32, bf16 = jnp.float32, jnp.bfloat16
    i = pl.program_id(1)
    rows = N_HEADS * tq
    nt = (((1,), (1,)), ((), ()))

    q = q_ref[0]
    lane_half = lax.broadcasted_iota(jnp.int32, (tq, 2 * HEAD_DIM), 1) // HEAD_DIM
    parts = []
    for h in range(N_HEADS):
        blk = q[:, (h // 2) * 2 * HEAD_DIM:(h // 2 + 1) * 2 * HEAD_DIM]
        grp = h // GROUP
        if h % 2 != grp:
            blk = pltpu.roll(blk, HEAD_DIM, 1)
        parts.append(jnp.where(lane_half == grp, blk, 0.0))
    qpad = jnp.concatenate(parts, axis=0).astype(bf16)

    def heads(x):
        return jnp.concatenate([x[:tq]] * GROUP + [x[tq:]] * GROUP, axis=0)

    tbl_c = bc_ref[0].reshape(rows, bc_ref.shape[-1])
    s = lax.dot_general(qpad, kc_ref[0], nt, preferred_element_type=f32) + tbl_c
    ok = tbl_c > 0.5 * NEG
    m = jnp.max(s, axis=1, keepdims=True)
    e = jnp.where(ok, jnp.exp(s - m), 0.0)
    d = jnp.sum(e, axis=1, keepdims=True)
    pc = e / jnp.where(d > 0.0, d, 1.0)
    pc16 = pc.astype(bf16)
    o_cmp = jnp.dot(pc16, vc_ref[0], preferred_element_type=f32)

    imp_h = jnp.dot(pc16, cover_ref[...], preferred_element_type=f32)
    imp = jnp.concatenate([_group_sum(imp_h, grp, tq) for grp in range(N_KV)], axis=0)
    ns = imp.shape[1]
    j = lax.broadcasted_iota(jnp.int32, imp.shape, 1)
    forced = (j == 0) | (j == i) | (j == i - 1)
    imp = jnp.where(forced, imp + FORCE_SCORE, imp)
    imp = jnp.where(j > i, -FORCE_SCORE, imp)
    cnt = jnp.zeros(imp.shape, jnp.int32)
    for jp in range(ns):
        col = imp[:, jp:jp + 1]
        ahead = (col > imp) | ((col == imp) & (j > jp))
        cnt = cnt + ahead.astype(jnp.int32)
    sel = cnt < TOP_N

    nb0 = jnp.maximum(i - (NEAR_BLOCKS - 1), 0)
    near = NEAR_BLOCKS * SEL_BLOCK
    k0 = pl.multiple_of(nb0 * SEL_BLOCK, SEL_BLOCK)
    sel_nb0 = jnp.sum(jnp.where((j == nb0) & sel, 1.0, 0.0), axis=1, keepdims=True)
    lane = lax.broadcasted_iota(jnp.int32, (rows, near), 1)
    s = lax.dot_general(qpad, ks_ref[0, pl.ds(k0, near), :], nt, preferred_element_type=f32)
    s = s + bsn_ref[0].reshape(rows, near)
    s = s + jnp.where(lane < SEL_BLOCK, (heads(sel_nb0) - 1.0) * SEL_MASK, 0.0)
    m = jnp.max(s, axis=1, keepdims=True)
    p = jnp.exp(s - m)
    l = jnp.sum(p, axis=1, keepdims=True)
    acc = jnp.dot(p.astype(bf16), vs_ref[0, pl.ds(k0, near), :], preferred_element_type=f32)

    sel_far = heads(jnp.where(sel & (j < nb0), 0.0, -1.0)).astype(bf16)
    n_far = (nb0 * SEL_BLOCK + FAR_KEYS - 1) // FAR_KEYS

    def far_body(c, carry):
        m, l, acc = carry
        c0 = pl.multiple_of(c * FAR_KEYS, FAR_KEYS)
        s = lax.dot_general(qpad, ks_ref[0, pl.ds(c0, FAR_KEYS), :], nt, preferred_element_type=f32)
        s = s + jnp.dot(sel_far, e_ref[c], preferred_element_type=f32)
        m_new = jnp.maximum(m, jnp.max(s, axis=1, keepdims=True))
        a = jnp.exp(m - m_new)
        p = jnp.exp(s - m_new)
        l = a * l + jnp.sum(p, axis=1, keepdims=True)
        acc = a * acc + jnp.dot(p.astype(bf16), vs_ref[0, pl.ds(c0, FAR_KEYS), :],
                                preferred_element_type=f32)
        return m_new, l, acc

    m, l, acc = lax.fori_loop(0, n_far, far_body, (m, l, acc))
    o_sel = acc / l

    wk = WIN_BLOCKS * SEL_BLOCK
    w0 = pl.multiple_of(jnp.maximum(i - (WIN_BLOCKS - 1), 0) * SEL_BLOCK, SEL_BLOCK)
    s = lax.dot_general(qpad, kw_ref[0, pl.ds(w0, wk), :], nt, preferred_element_type=f32)
    s = s + bw_ref[0].reshape(rows, wk)
    m = jnp.max(s, axis=1, keepdims=True)
    p = jnp.exp(s - m)
    l = jnp.sum(p, axis=1, keepdims=True)
    o_win = jnp.dot(p.astype(bf16), vw_ref[0, pl.ds(w0, wk), :], preferred_element_type=f32) / l

    gates = g_ref[0]
    for h in range(N_HEADS):
        grp = h // GROUP
        r0, r1 = h * tq, (h + 1) * tq
        c0, c1 = grp * HEAD_DIM, (grp + 1) * HEAD_DIM
        o_h = (gates[:, 3 * h:3 * h + 1] * o_cmp[r0:r1, c0:c1]
               + gates[:, 3 * h + 1:3 * h + 2] * o_sel[r0:r1, c0:c1]
               + gates[:, 3 * h + 2:3 * h + 3] * o_win[r0:r1, c0:c1])
        o_ref[0, :, h * HEAD_DIM:(h + 1) * HEAD_DIM] = o_h


def nsa_prompt_attention(q, gates, kc, vc, ks, vs, kw, vw, rel_bias):
    B, T = q.shape[0], q.shape[1]
    tq = SEL_BLOCK
    nq = T // tq
    ns = T // SEL_BLOCK
    nc = kc.shape[1]
    ncp = -(-nc // LANES) * LANES
    bf16 = jnp.bfloat16
    scale = HEAD_DIM ** -0.5
    assert T % FAR_KEYS == 0 and T >= WIN_BLOCKS * SEL_BLOCK

    tl = np.arange(tq)[None, :, None]
    r = tq // CMP_STRIDE
    span = r * (nq - 1) + ncp
    y = np.arange(span)[None, None, :]
    dy = tl + CMP_STRIDE * (y - (ncp - 1)) - (CMP_LEN - 1)
    master = _bias_table(rel_bias, dy, dy >= 0)[0][:, :, ::-1]
    pad_col = jnp.arange(ncp) < nc
    bc = jnp.stack([jnp.where(pad_col, master[:, :, r * (nq - 1 - i):r * (nq - 1 - i) + ncp], NEG)
                    for i in range(nq)])
    vv = np.arange(NEAR_BLOCKS)[:, None, None]
    dn = vv * tq + tl - np.arange(NEAR_BLOCKS * SEL_BLOCK)[None, None, :]
    bsn = _bias_table(rel_bias, dn, dn >= 0)
    vw_ = np.arange(WIN_BLOCKS)[:, None, None]
    dw = vw_ * tq + tl - np.arange(WIN_BLOCKS * SEL_BLOCK)[None, None, :]
    bw = _bias_table(rel_bias, dw, (dw >= 0) & (dw < WINDOW))

    ci = np.arange(ncp)[:, None]
    sj = np.arange(ns)[None, :]
    cover = ((ci * CMP_STRIDE < (sj + 1) * SEL_BLOCK) & (ci * CMP_STRIDE + CMP_LEN > sj * SEL_BLOCK)
             & (ci < nc))
    cover = jnp.asarray(cover.astype(np.float32), bf16)
    pos = np.arange(T).reshape(T // FAR_KEYS, 1, FAR_KEYS)
    expand = jnp.asarray(np.where(pos // SEL_BLOCK == np.arange(ns)[None, :, None], SEL_MASK, 0.0)
                         .astype(np.float32), bf16)

    padc = ((0, 0), (0, ncp - nc), (0, 0))
    seq = lambda b, i: (b, 0, 0)
    return pl.pallas_call(
        functools.partial(_nsa_prompt_kernel, tq=tq),
        grid=(B, nq),
        in_specs=[
            pl.BlockSpec((1, tq, ATTN_WIDTH), lambda b, i: (b, i, 0)),
            pl.BlockSpec((1, tq, 3 * N_HEADS), lambda b, i: (b, i, 0)),
            pl.BlockSpec((1, ncp, KV_WIDTH), seq),
            pl.BlockSpec((1, ncp, KV_WIDTH), seq),
            pl.BlockSpec((1, T, KV_WIDTH), seq),
            pl.BlockSpec((1, T, KV_WIDTH), seq),
            pl.BlockSpec((1, T, KV_WIDTH), seq),
            pl.BlockSpec((1, T, KV_WIDTH), seq),
            pl.BlockSpec((1, N_HEADS, tq, ncp), lambda b, i: (i, 0, 0, 0)),
            pl.BlockSpec((1, N_HEADS, tq, NEAR_BLOCKS * SEL_BLOCK),
                         lambda b, i: (jnp.minimum(i, NEAR_BLOCKS - 1), 0, 0, 0)),
            pl.BlockSpec((1, N_HEADS, tq, WIN_BLOCKS * SEL_BLOCK),
                         lambda b, i: (jnp.minimum(i, WIN_BLOCKS - 1), 0, 0, 0)),
            pl.BlockSpec((ncp, ns), lambda b, i: (0, 0)),
            pl.BlockSpec((T // FAR_KEYS, ns, FAR_KEYS), lambda b, i: (0, 0, 0)),
        ],
        out_specs=pl.BlockSpec((1, tq, ATTN_WIDTH), lambda b, i: (b, i, 0)),
        out_shape=jax.ShapeDtypeStruct((B, T, ATTN_WIDTH), jnp.float32),
        compiler_params=pltpu.CompilerParams(dimension_semantics=("arbitrary", "arbitrary"),
                                             vmem_limit_bytes=VMEM_LIMIT),
        name="nsa_prompt",
    )(q * scale, gates, jnp.pad(kc, padc).astype(bf16), jnp.pad(vc, padc).astype(bf16),
      ks.astype(bf16), vs.astype(bf16), kw.astype(bf16), vw.astype(bf16), bc, bsn, bw, cover, expand)


NEW_PAD = 16


def _softmax_parts(s):
    m = jnp.max(s, axis=1, keepdims=True)
    p = jnp.exp(s - m)
    return m, p, jnp.sum(p, axis=1, keepdims=True)


def _nsa_sample_kernel(q_ref, g_ref, kc_ref, vc_ref, ps_ref, nsk_ref, nsv_ref, cw_ref, nwk_ref, nwv_ref,
                       bc_ref, bsl_ref, bnew_ref, bw_ref, cover_ref, e_ref, o_ref, *, tn, ns):
    f32, bf16 = jnp.float32, jnp.bfloat16
    nt = (((1,), (1,)), ((), ()))
    qpad = q_ref[0]
    cur = ns - 1

    def heads(x):
        return jnp.concatenate([x[:tn]] * GROUP + [x[tn:]] * GROUP, axis=0)

    s = lax.dot_general(qpad, kc_ref[0], nt, preferred_element_type=f32) + bc_ref[...]
    _, e, d = _softmax_parts(s)
    pc = e / d
    pc16 = pc.astype(bf16)
    o_cmp = jnp.dot(pc16, vc_ref[0], preferred_element_type=f32)

    imp_h = jnp.dot(pc16, cover_ref[...], preferred_element_type=f32)
    imp = jnp.concatenate([_group_sum(imp_h, grp, tn) for grp in range(N_KV)], axis=0)
    j = lax.broadcasted_iota(jnp.int32, imp.shape, 1)
    forced = (j == 0) | (j == cur) | (j == cur - 1)
    imp = jnp.where(forced, imp + FORCE_SCORE, imp)
    imp = jnp.where(j > cur, -jnp.inf, imp)
    cnt = jnp.zeros(imp.shape, jnp.int32)
    for jp in range(ns):
        col = imp[:, jp:jp + 1]
        ahead = (col > imp) | ((col == imp) & (j > jp))
        cnt = cnt + ahead.astype(jnp.int32)
    sel = cnt < TOP_N
    sel_far = heads(jnp.where(sel & (j < cur), 0.0, -1.0)).astype(bf16)

    bnew = bnew_ref[...]
    m, p, l = _softmax_parts(lax.dot_general(qpad, nsk_ref[0], nt, preferred_element_type=f32) + bnew)
    acc = jnp.dot(p.astype(bf16), nsv_ref[0], preferred_element_type=f32)
    n_far = e_ref.shape[0]

    def far_chunk(c, carry, bias):
        m, l, acc = carry
        kv = ps_ref[0, pl.ds(pl.multiple_of(c * FAR_KEYS, FAR_KEYS), FAR_KEYS), :]
        s = lax.dot_general(qpad, kv[:, :KV_WIDTH].astype(bf16), nt, preferred_element_type=f32)
        s = s + jnp.dot(sel_far, e_ref[c], preferred_element_type=f32)
        if bias is not None:
            s = s + bias
        m_new = jnp.maximum(m, jnp.max(s, axis=1, keepdims=True))
        a = jnp.exp(m - m_new)
        p = jnp.exp(s - m_new)
        l = a * l + jnp.sum(p, axis=1, keepdims=True)
        acc = a * acc + jnp.dot(p.astype(bf16), kv[:, KV_WIDTH:].astype(bf16), preferred_element_type=f32)
        return m_new, l, acc

    carry = lax.fori_loop(0, n_far - 1, lambda c, carry: far_chunk(c, carry, None), (m, l, acc))
    m, l, acc = far_chunk(n_far - 1, carry, bsl_ref[...])
    o_sel = acc / l

    cw = cw_ref[0]
    s_c = lax.dot_general(qpad, cw[:, :KV_WIDTH].astype(bf16), nt, preferred_element_type=f32) + bw_ref[...]
    s_n = lax.dot_general(qpad, nwk_ref[0], nt, preferred_element_type=f32) + bnew
    m = jnp.maximum(jnp.max(s_c, axis=1, keepdims=True), jnp.max(s_n, axis=1, keepdims=True))
    p_c = jnp.exp(s_c - m)
    p_n = jnp.exp(s_n - m)
    l = jnp.sum(p_c, axis=1, keepdims=True) + jnp.sum(p_n, axis=1, keepdims=True)
    o_win = (jnp.dot(p_c.astype(bf16), cw[:, KV_WIDTH:].astype(bf16), preferred_element_type=f32)
             + jnp.dot(p_n.astype(bf16), nwv_ref[0], preferred_element_type=f32)) / l

    g = g_ref[0]
    o_ref[0] = g[:, 0:1] * o_cmp + g[:, 1:2] * o_sel + g[:, 2:3] * o_win


def nsa_sample_attention(q, gates, kc, vc, past_sel, new_sel, cache_win, new_win, rel_bias):
    B, tn = q.shape[0], q.shape[1]
    past, wk, nc = past_sel.shape[1], cache_win.shape[1], kc.shape[1]
    assert past % FAR_KEYS == 0 and past % SEL_BLOCK == 0 and tn <= NEW_PAD and wk == WINDOW
    ns = past // SEL_BLOCK + 1
    nsp = -(-ns // LANES) * LANES
    ncp = -(-nc // LANES) * LANES
    rows = N_HEADS * tn
    bf16 = jnp.bfloat16
    n_far = past // FAR_KEYS

    tl = np.arange(tn)[:, None]
    flat = lambda t: t[0].reshape(rows, t.shape[-1])
    cc = np.arange(ncp)[None, :]
    dc = past + tl - (cc * CMP_STRIDE + CMP_LEN - 1)
    bc = flat(_bias_table(rel_bias, dc[None], ((dc >= 0) & (cc < nc))[None]))
    dl = tl + FAR_KEYS - np.arange(FAR_KEYS)[None, :]
    bsl = flat(_bias_table(rel_bias, dl[None], (dl >= 0)[None]))
    jn = np.arange(NEW_PAD)[None, :]
    dn = tl - jn
    bnew = flat(_bias_table(rel_bias, dn[None], ((dn >= 0) & (jn < tn))[None]))
    dw = wk + tl - np.arange(wk)[None, :]
    bw = flat(_bias_table(rel_bias, dw[None], ((dw >= 0) & (dw < WINDOW))[None]))

    ci = np.arange(ncp)[:, None]
    sj = np.arange(nsp)[None, :]
    cover = ((ci * CMP_STRIDE < (sj + 1) * SEL_BLOCK) & (ci * CMP_STRIDE + CMP_LEN > sj * SEL_BLOCK)
             & (ci < nc) & (sj < ns))
    cover = jnp.asarray(cover.astype(np.float32), bf16)
    pos = np.arange(past).reshape(n_far, 1, FAR_KEYS)
    expand = jnp.asarray(np.where(pos // SEL_BLOCK == np.arange(nsp)[None, :, None], SEL_MASK, 0.0)
                         .astype(np.float32), bf16)

    qh = jnp.transpose((q * HEAD_DIM ** -0.5).reshape(B, tn, N_HEADS, HEAD_DIM), (0, 2, 1, 3))
    zero = jnp.zeros_like(qh)
    grp = (np.arange(N_HEADS) // GROUP)[None, :, None, None]
    qpad = jnp.where(grp == 0, jnp.concatenate([qh, zero], -1), jnp.concatenate([zero, qh], -1))
    qpad = qpad.reshape(B, rows, KV_WIDTH).astype(bf16)
    grow = jnp.transpose(gates.reshape(B, tn, N_HEADS, 3), (0, 2, 1, 3)).reshape(B, rows, 3)
    padn = lambda a: jnp.pad(a, ((0, 0), (0, NEW_PAD - tn), (0, 0))).astype(bf16)
    padc = ((0, 0), (0, ncp - nc), (0, 0))

    seq3 = lambda b: (b, 0, 0)
    const2 = lambda b: (0, 0)
    out = pl.pallas_call(
        functools.partial(_nsa_sample_kernel, tn=tn, ns=ns),
        grid=(B,),
        in_specs=[
            pl.BlockSpec((1, rows, KV_WIDTH), seq3),
            pl.BlockSpec((1, rows, 3), seq3),
            pl.BlockSpec((1, ncp, KV_WIDTH), seq3),
            pl.BlockSpec((1, ncp, KV_WIDTH), seq3),
            pl.BlockSpec((1, past, 2 * KV_WIDTH), seq3),
            pl.BlockSpec((1, NEW_PAD, KV_WIDTH), seq3),
            pl.BlockSpec((1, NEW_PAD, KV_WIDTH), seq3),
            pl.BlockSpec((1, wk, 2 * KV_WIDTH), seq3),
            pl.BlockSpec((1, NEW_PAD, KV_WIDTH), seq3),
            pl.BlockSpec((1, NEW_PAD, KV_WIDTH), seq3),
            pl.BlockSpec((rows, ncp), const2),
            pl.BlockSpec((rows, FAR_KEYS), const2),
            pl.BlockSpec((rows, NEW_PAD), const2),
            pl.BlockSpec((rows, wk), const2),
            pl.BlockSpec((ncp, nsp), const2),
            pl.BlockSpec((n_far, nsp, FAR_KEYS), lambda b: (0, 0, 0)),
        ],
        out_specs=pl.BlockSpec((1, rows, KV_WIDTH), seq3),
        out_shape=jax.ShapeDtypeStruct((B, rows, KV_WIDTH), jnp.float32),
        compiler_params=pltpu.CompilerParams(dimension_semantics=("arbitrary",),
                                             vmem_limit_bytes=VMEM_LIMIT),
        name="nsa_sample",
    )(qpad, grow, jnp.pad(kc, padc).astype(bf16), jnp.pad(vc, padc).astype(bf16), past_sel,
      padn(new_sel[:, :, :KV_WIDTH]), padn(new_sel[:, :, KV_WIDTH:]), cache_win,
      padn(new_win[:, :, :KV_WIDTH]), padn(new_win[:, :, KV_WIDTH:]), bc, bsl, bnew, bw, cover, expand)
    o = out.reshape(B, N_HEADS, tn, 2, HEAD_DIM)
    o = jnp.where(grp[..., None] == 0, o[:, :, :, 0:1], o[:, :, :, 1:2])[:, :, :, 0]
    return jnp.transpose(o, (0, 2, 1, 3)).reshape(B, tn, ATTN_WIDTH)


N_SUB = 2 * PEER_HEADS
HALF_DQ = PEER_DQ // 2
N_PAIRS = PEER_HEADS * PEER_TOPK


def _top_rounds(s, row, n):
    width = s.shape[0]
    k = lax.broadcasted_iota(jnp.int32, (n, s.shape[1]), 0)
    vals = jnp.zeros((n, s.shape[1]), jnp.float32)
    ids = jnp.zeros((n, s.shape[1]), jnp.float32)
    for a in range(n):
        m = jnp.max(s, axis=0, keepdims=True)
        ix = jnp.min(jnp.where(s == m, row, float(width)), axis=0, keepdims=True)
        vals = jnp.where(k == a, m, vals)
        ids = jnp.where(k == a, ix, ids)
        s = jnp.where(row == ix, -jnp.inf, s)
    return vals, ids


def _peer_route_kernel(h_ref, wq_ref, keys_ref, idx_ref, g_ref, q_scr, e_scr, w_scr):
    f32, bf16 = jnp.float32, jnp.bfloat16
    tm = h_ref.shape[0]
    nt = (((1,), (1,)), ((), ()))
    q = jnp.dot(h_ref[...].astype(bf16), wq_ref[...], preferred_element_type=f32)
    for k in range(N_SUB):
        q_scr[k] = q[:, k * HALF_DQ:(k + 1) * HALF_DQ].astype(bf16)
    row = lax.broadcasted_iota(jnp.int32, (N_KEYS, tm), 0).astype(f32)
    n_cand = PEER_TOPK * PEER_TOPK
    row2 = lax.broadcasted_iota(jnp.int32, (n_cand, tm), 0).astype(f32)

    def head(h, carry):
        s1, i1 = _top_rounds(lax.dot_general(keys_ref[2 * h], q_scr[2 * h], nt, preferred_element_type=f32),
                             row, PEER_TOPK)
        s2, i2 = _top_rounds(lax.dot_general(keys_ref[2 * h + 1], q_scr[2 * h + 1], nt,
                                             preferred_element_type=f32), row, PEER_TOPK)
        cand = jnp.concatenate([s1[a:a + 1] + s2 for a in range(PEER_TOPK)], axis=0)
        cidx = jnp.concatenate([i1[a:a + 1] * float(N_KEYS) + i2 for a in range(PEER_TOPK)], axis=0)
        k = lax.broadcasted_iota(jnp.int32, (PEER_TOPK, tm), 0)
        top = jnp.zeros((PEER_TOPK, tm), f32)
        eid = jnp.zeros((PEER_TOPK, tm), f32)
        for a in range(PEER_TOPK):
            m = jnp.max(cand, axis=0, keepdims=True)
            pos = jnp.min(jnp.where(cand == m, row2, float(n_cand)), axis=0, keepdims=True)
            hit = row2 == pos
            e = jnp.max(jnp.where(hit, cidx, -1.0), axis=0, keepdims=True)
            top = jnp.where(k == a, m, top)
            eid = jnp.where(k == a, e, eid)
            cand = jnp.where(hit, -jnp.inf, cand)
        ex = jnp.exp(top - top[0:1])
        r0 = pl.multiple_of(h * PEER_TOPK, PEER_TOPK)
        e_scr[pl.ds(r0, PEER_TOPK), :] = eid
        w_scr[pl.ds(r0, PEER_TOPK), :] = ex / jnp.sum(ex, axis=0, keepdims=True)
        return carry

    lax.fori_loop(0, PEER_HEADS, head, 0)
    idx_ref[...] = e_scr[...].T.astype(jnp.int32)
    g_ref[...] = w_scr[...].T


def peer_route(h2, peer_q, peer_keys, tm=128):
    n = h2.shape[0]
    tm = min(tm, n)
    bf16 = jnp.bfloat16
    keys = peer_keys.reshape(N_SUB, N_KEYS, HALF_DQ).astype(bf16)
    return pl.pallas_call(
        _peer_route_kernel,
        grid=(n // tm,),
        in_specs=[pl.BlockSpec((tm, D_MODEL), lambda i: (i, 0)),
                  pl.BlockSpec((D_MODEL, PEER_HEADS * PEER_DQ), lambda i: (0, 0)),
                  pl.BlockSpec((N_SUB, N_KEYS, HALF_DQ), lambda i: (0, 0, 0))],
        out_specs=[pl.BlockSpec((tm, N_PAIRS), lambda i: (i, 0)),
                   pl.BlockSpec((tm, N_PAIRS), lambda i: (i, 0))],
        out_shape=[jax.ShapeDtypeStruct((n, N_PAIRS), jnp.int32),
                   jax.ShapeDtypeStruct((n, N_PAIRS), jnp.float32)],
        scratch_shapes=[pltpu.VMEM((N_SUB, tm, HALF_DQ), bf16),
                        pltpu.VMEM((N_PAIRS, tm), jnp.float32),
                        pltpu.VMEM((N_PAIRS, tm), jnp.float32)],
        compiler_params=pltpu.CompilerParams(dimension_semantics=("arbitrary",),
                                             vmem_limit_bytes=VMEM_LIMIT),
        name="peer_route",
    )(h2, peer_q.astype(bf16), keys)


def _gelu_tanh(x):
    return 0.5 * x * (1.0 + jnp.tanh(math.sqrt(2.0 / math.pi) * (x + 0.044715 * (x * x * x))))


N_SLOTS = 8
ROW_TILES = 2 * D_MODEL // LANES
U_TILES = D_MODEL // LANES


def _peer_expert_kernel(idx_ref, nxt_ref, x_ref, g_ref, uv_ref, y_ref, *scratch):
    tb = x_ref.shape[0]
    i = pl.program_id(0)
    last = pl.num_programs(0) - 1
    ahead = N_SLOTS - 1
    bufs, sem = scratch[:N_SLOTS], scratch[N_SLOTS]

    def issue(ids, t, slot):
        for k in range(N_PAIRS):
            pltpu.make_async_copy(uv_ref.at[ids[t, k]], bufs[slot].at[k // 8, :, k % 8, :], sem.at[slot]).start()

    def wait_slot(slot):
        pltpu.make_async_copy(bufs[slot], bufs[slot], sem.at[slot]).wait()

    eye = (lax.broadcasted_iota(jnp.int32, (N_PAIRS, N_PAIRS), 0)
           == lax.broadcasted_iota(jnp.int32, (N_PAIRS, N_PAIRS), 1))

    def compute(t, slot):
        buf = bufs[slot]

        def chunk(c):
            return buf[:, c].reshape(N_PAIRS, LANES)

        x = x_ref[t]
        acc = chunk(0) * x[0:1, :]
        for s in range(1, U_TILES):
            acc = acc + chunk(s) * x[s:s + 1, :]
        a = jnp.sum(acc, axis=1, keepdims=True)
        g_col = jnp.sum(jnp.where(eye, g_ref[pl.ds(t, 1), :], 0.0), axis=1, keepdims=True)
        w = g_col * _gelu_tanh(a)
        y = [jnp.sum(w * chunk(U_TILES + s), axis=0, keepdims=True) for s in range(U_TILES)]
        y_ref[pl.ds(t, 1), :] = jnp.concatenate(y, axis=1)

    @pl.when(i == 0)
    def _():
        for t in range(ahead):
            issue(idx_ref, t, t)

    def group(j, c):
        for u in range(N_SLOTS):
            t = j * N_SLOTS + u
            wait_slot(u)
            issue(idx_ref, t + ahead, (u + ahead) % N_SLOTS)
            compute(t, u)
        return c

    n_groups = tb // N_SLOTS
    lax.fori_loop(0, n_groups - 1, group, 0)
    for u in range(N_SLOTS):
        t = tb - N_SLOTS + u
        wait_slot(u)
        if u == 0:
            issue(idx_ref, t + ahead, ahead)
        else:
            issue(nxt_ref, u - 1, u - 1)
        compute(t, u)

    @pl.when(i == last)
    def _():
        for u in range(ahead):
            wait_slot(u)


def peer_experts(h2, eidx, gate, expert_u, expert_v, tb=64):
    n = h2.shape[0]
    tb = min(tb, n)
    nb = n // tb
    e = expert_u.shape[0]
    uv = jnp.concatenate([expert_u.reshape(e, U_TILES, LANES), expert_v.reshape(e, U_TILES, LANES)], axis=1)
    return pl.pallas_call(
        _peer_expert_kernel,
        grid=(nb,),
        in_specs=[pl.BlockSpec((tb, N_PAIRS), lambda i: (i, 0), memory_space=pltpu.SMEM),
                  pl.BlockSpec((tb, N_PAIRS), lambda i: (jnp.minimum(i + 1, nb - 1), 0), memory_space=pltpu.SMEM),
                  pl.BlockSpec((tb, U_TILES, LANES), lambda i: (i, 0, 0)),
                  pl.BlockSpec((tb, N_PAIRS), lambda i: (i, 0)),
                  pl.BlockSpec(memory_space=pl.ANY)],
        out_specs=pl.BlockSpec((tb, D_MODEL), lambda i: (i, 0)),
        out_shape=jax.ShapeDtypeStruct((n, D_MODEL), jnp.float32),
        scratch_shapes=[pltpu.VMEM((N_PAIRS // 8, ROW_TILES, 8, LANES), jnp.float32)] * N_SLOTS
        + [pltpu.SemaphoreType.DMA((N_SLOTS,))],
        compiler_params=pltpu.CompilerParams(dimension_semantics=("arbitrary",),
                                             vmem_limit_bytes=VMEM_LIMIT),
        name="peer_experts",
    )(eidx, eidx, h2.reshape(n, U_TILES, LANES), gate, uv)


def rms_norm(x, g):
    y = x * lax.rsqrt(jnp.mean(x * x, axis=-1, keepdims=True) + EPS)
    return y * g


def layer_norm(x, g, b):
    mu = jnp.mean(x, axis=-1, keepdims=True)
    var = jnp.mean(jnp.square(x - mu), axis=-1, keepdims=True)
    return (x - mu) * lax.rsqrt(var + EPS) * g + b


def masked_softmax(logits, mask, axis):
    neg = jnp.where(mask, logits, -jnp.inf)
    m = jnp.max(neg, axis=axis, keepdims=True)
    m = jnp.where(jnp.isfinite(m), m, 0.0)
    e = jnp.where(mask, jnp.exp(neg - m), 0.0)
    d = jnp.sum(e, axis=axis, keepdims=True)
    return e / jnp.where(d > 0.0, d, 1.0)


def t5_bucket(dist):
    n = jnp.maximum(dist, 0)
    exact = N_BUCKETS // 2
    nf = jnp.maximum(n, 1).astype(jnp.float32)
    large = exact + (jnp.log(nf / exact) / math.log(MAX_DISTANCE / exact) * (N_BUCKETS - exact)).astype(jnp.int32)
    return jnp.where(n < exact, n, jnp.minimum(large, N_BUCKETS - 1))


def compress(rows, pe, w1, b1, w2, b2):
    B, L = rows.shape[0], rows.shape[1]
    ratio = CMP_LEN // CMP_STRIDE
    n_chunks = L // CMP_STRIDE
    nc = n_chunks - ratio + 1
    r = rows[:, :n_chunks * CMP_STRIDE].reshape(B, n_chunks, CMP_STRIDE, N_KV, HEAD_DIM)
    w1c = w1.reshape(ratio, CMP_STRIDE, HEAD_DIM, CMP_HIDDEN)
    parts = jnp.einsum('bcsgd,jsdh->jbcgh', r, w1c)
    hid = b1 + pe.reshape(-1) @ w1
    for j in range(ratio):
        hid = hid + parts[j, :, j:j + nc]
    out = jax.nn.gelu(hid) @ w2 + b2
    end = jnp.arange(nc, dtype=jnp.int32) * CMP_STRIDE + (CMP_LEN - 1)
    return out, end


def compress_branch(rows, lp, which):
    return compress(rows, lp['cmp_pe'][which], lp['cmp_w1'][which], lp['cmp_b1'][which],
                    lp['cmp_w2'][which], lp['cmp_b2'][which])


def to_blocks(rows):
    B, L = rows.shape[0], rows.shape[1]
    ns = -(-L // SEL_BLOCK)
    rows = jnp.pad(rows, ((0, 0), (0, ns * SEL_BLOCK - L), (0, 0), (0, 0)))
    return jnp.transpose(rows.reshape(B, ns, SEL_BLOCK, N_KV, HEAD_DIM), (0, 3, 1, 2, 4))


def nsa_block(q, t_pos, gates, kc, vc, kc_end, ks_blk, vs_blk, kw, vw, w_pos, rel_bias):
    B = q.shape[0]
    scale = HEAD_DIM ** -0.5
    tbl = rel_bias.astype(jnp.float32).reshape(N_BUCKETS, N_KV, GROUP)
    lc = jnp.einsum('bqgrd,bcgd->bqgrc', q, kc).astype(jnp.float32) * scale
    dc = t_pos[:, None] - kc_end[None, :]
    lc = lc + jnp.transpose(tbl[t5_bucket(dc)], (0, 2, 3, 1))[None]
    pc = masked_softmax(lc, (dc >= 0)[None, :, None, None, :], axis=-1)
    o_cmp = jnp.einsum('bqgrc,bcgd->bqgrd', pc.astype(vc.dtype), vc)
    nc, ns = kc.shape[1], ks_blk.shape[2]
    ci = jnp.arange(nc)[:, None]
    sj = jnp.arange(ns)[None, :]
    cover = ((ci * CMP_STRIDE < (sj + 1) * SEL_BLOCK) & (ci * CMP_STRIDE + CMP_LEN > sj * SEL_BLOCK)).astype(jnp.float32)
    imp = jnp.einsum('bqgrc,cj->bqgj', pc, cover)
    cur = (t_pos // SEL_BLOCK)[:, None]
    forced = (sj == 0) | (sj == cur) | (sj == cur - 1)
    valid = sj * SEL_BLOCK <= t_pos[:, None]
    imp = jnp.where(forced[None, :, None, :], imp + FORCE_SCORE, imp)
    imp = jnp.where(valid[None, :, None, :], imp, -FORCE_SCORE)
    _, idx = lax.top_k(imp, min(TOP_N, ns))
    bi = jnp.arange(B)[:, None, None, None]
    gi = jnp.arange(N_KV)[None, None, :, None]
    ks_g = ks_blk[bi, gi, idx]
    vs_g = vs_blk[bi, gi, idx]
    ls = jnp.einsum('bqgrd,bqgnkd->bqgrnk', q, ks_g).astype(jnp.float32) * scale
    pos = idx[..., None] * SEL_BLOCK + jnp.arange(SEL_BLOCK, dtype=jnp.int32)
    ds = t_pos[None, :, None, None, None] - pos
    tblk = jnp.transpose(tbl, (1, 0, 2))
    bs = tblk[jnp.arange(N_KV)[None, None, :, None, None], t5_bucket(ds)]
    ls = ls + jnp.moveaxis(bs, -1, 3)
    ps = masked_softmax(ls, (ds >= 0)[:, :, :, None], axis=(-2, -1))
    o_sel = jnp.einsum('bqgrnk,bqgnkd->bqgrd', ps.astype(vs_g.dtype), vs_g)
    lw = jnp.einsum('bqgrd,bwgd->bqgrw', q, kw).astype(jnp.float32) * scale
    dw = t_pos[:, None] - w_pos[None, :]
    lw = lw + jnp.transpose(tbl[t5_bucket(dw)], (0, 2, 3, 1))[None]
    mw = (dw >= 0) & (dw < WINDOW) & (w_pos >= 0)[None, :]
    pw = masked_softmax(lw, mw[None, :, None, None, :], axis=-1)
    o_win = jnp.einsum('bqgrw,bwgd->bqgrd', pw.astype(vw.dtype), vw)
    return gates[..., 0:1] * o_cmp + gates[..., 1:2] * o_sel + gates[..., 2:3] * o_win


def nsa_prompt(q, gates, kc_raw, vc_raw, ks, vs, kw, vw, lp, rel_bias):
    B, T = q.shape[0], q.shape[1]
    kc, kc_end = compress_branch(kc_raw, lp, 0)
    kc = rms_norm(kc, lp['k_norm'][0])
    vc, _ = compress_branch(vc_raw, lp, 1)
    flat = lambda a: a.reshape(B, a.shape[1], -1)
    o = nsa_prompt_attention(flat(q), flat(gates), flat(kc), flat(vc), flat(ks), flat(vs), flat(kw), flat(vw),
                             rel_bias)
    new_win = jnp.stack([kw, vw], axis=2)[:, T - min(WINDOW, T):]
    return o, (jnp.stack([kc_raw, vc_raw], axis=2), jnp.stack([ks, vs], axis=2), new_win)


def nsa_sample(q, gates, kc_raw, vc_raw, ks, vs, kw, vw, cache_cmp, cache_sel, cache_win, page_table, lp, rel_bias):
    Bd, Tn = q.shape[0], q.shape[1]
    past = page_table.shape[1] * cache_cmp.shape[1]
    t_pos = past + jnp.arange(Tn, dtype=jnp.int32)
    past_cmp = cache_cmp[page_table].reshape(Bd, past, 2, N_KV, HEAD_DIM)
    past_sel = cache_sel[page_table].reshape(Bd, past, 2 * KV_WIDTH)
    kc, kc_end = compress_branch(jnp.concatenate([past_cmp[:, :, 0], kc_raw], axis=1), lp, 0)
    kc = rms_norm(kc, lp['k_norm'][0])
    vc, _ = compress_branch(jnp.concatenate([past_cmp[:, :, 1], vc_raw], axis=1), lp, 1)
    wk = cache_win.shape[1]
    flat = lambda a: a.reshape(Bd, a.shape[1], -1)
    new_sel = jnp.concatenate([flat(ks), flat(vs)], axis=-1)
    new_wkv = jnp.concatenate([flat(kw), flat(vw)], axis=-1)
    o = nsa_sample_attention(flat(q), flat(gates), flat(kc), flat(vc), past_sel, new_sel, flat(cache_win),
                             new_wkv, rel_bias)
    win = jnp.concatenate([cache_win, jnp.stack([kw, vw], axis=2)], axis=1)
    new_win = win[:, wk + Tn - min(WINDOW, wk + Tn):]
    return o.reshape(Bd, Tn, ATTN_WIDTH), (jnp.stack([kc_raw, vc_raw], axis=2), jnp.stack([ks, vs], axis=2), new_win)


def conformer_conv(glu_in, hist, lp):
    u = glu_in[..., :CONV_WIDTH] * jax.nn.sigmoid(glu_in[..., CONV_WIDTH:])
    uh = jnp.concatenate([hist.astype(u.dtype), u], axis=1)
    y = lax.conv_general_dilated(uh, lp['conv_w'][:, None, :], window_strides=(1,), padding='VALID',
                                 dimension_numbers=('NWC', 'WIO', 'NWC'), feature_group_count=CONV_WIDTH)
    y = jax.nn.silu(layer_norm(y + lp['conv_b'], lp['conv_ln_g'], lp['conv_ln_b']))
    return y, uh[:, uh.shape[1] - (CONV_K - 1):]


def peer(h, lp):
    shp = h.shape
    xt = h.reshape(-1, D_MODEL)
    n = xt.shape[0]
    eidx, gate = peer_route(xt, lp['peer_q'], lp['peer_keys'])
    y = peer_experts(xt, eidx, gate, lp['expert_u'], lp['expert_v'])
    return y.reshape(shp)


def split_points():
    pts, acc = [], 0
    for s in SPLIT_SIZES[:-1]:
        acc += s
        pts.append(acc)
    return pts


def layer(x, c, conv_hist, lp, rel_bias, cache):
    B, T = x.shape[0], x.shape[1]
    mod = (jax.nn.silu(c) @ lp['w_ada'] + lp['b_ada'])[:, None, :]
    sh1, sc1, gt1, sh2, sc2, gt2 = jnp.split(mod, 6, axis=-1)
    h = rms_norm(x, lp['norm1']) * (1 + sc1) + sh1
    z = matmul(h.reshape(B * T, D_MODEL), lp['w_in']).reshape(B, T, IN_COLS)
    zq, zkc, zvc, zks, zvs, zkw, zvw, zg, zglu = jnp.split(z, split_points(), axis=-1)

    def kv(z):
        return z.reshape(B, T, N_KV, HEAD_DIM)

    q = rms_norm(zq.reshape(B, T, N_KV, GROUP, HEAD_DIM), lp['q_norm'])
    ks = rms_norm(kv(zks), lp['k_norm'][1])
    kw = rms_norm(kv(zkw), lp['k_norm'][2])
    gates = jax.nn.sigmoid(zg.reshape(B, T, N_KV, GROUP, 3))
    if cache is None:
        o_attn, attn_state = nsa_prompt(q, gates, kv(zkc), kv(zvc), ks, kv(zvs), kw, kv(zvw), lp, rel_bias)
    else:
        cache_cmp, cache_sel, cache_win, page_table = cache
        o_attn, attn_state = nsa_sample(q, gates, kv(zkc), kv(zvc), ks, kv(zvs), kw, kv(zvw),
                                        cache_cmp, cache_sel, cache_win, page_table, lp, rel_bias)
    o_conv, conv_state = conformer_conv(zglu, conv_hist, lp)
    cat = jnp.concatenate([rms_norm(o_attn, lp['attn_out_norm']), o_conv * lp['conv_out_scale']], axis=-1)
    mix = matmul(cat.reshape(B * T, D_MODEL), lp['w_out']).reshape(B, T, D_MODEL)
    x = x + gt1 * mix
    h2 = rms_norm(x, lp['norm2']) * (1 + sc2) + sh2
    x = x + gt2 * peer(h2, lp)
    return x, attn_state, conv_state


def kernel(x_prompt, x_sample, c_prompt, c_sample, cache_cmp_kv, cache_sel_kv, cache_win_kv, state_conv, page_table, rel_bias, w_ada, b_ada, norm1, w_in, q_norm, k_norm, cmp_pe, cmp_w1, cmp_b1, cmp_w2, cmp_b2, conv_w, conv_b, conv_ln_g, conv_ln_b, attn_out_norm, conv_out_scale, w_out, norm2, peer_q, peer_keys, expert_u, expert_v):
    lp = {
        'w_ada': w_ada[0], 'b_ada': b_ada[0], 'norm1': norm1[0], 'w_in': w_in[0],
        'q_norm': q_norm[0], 'k_norm': k_norm[0], 'cmp_pe': cmp_pe[0], 'cmp_w1': cmp_w1[0],
        'cmp_b1': cmp_b1[0], 'cmp_w2': cmp_w2[0], 'cmp_b2': cmp_b2[0], 'conv_w': conv_w[0],
        'conv_b': conv_b[0], 'conv_ln_g': conv_ln_g[0], 'conv_ln_b': conv_ln_b[0],
        'attn_out_norm': attn_out_norm[0], 'conv_out_scale': conv_out_scale[0], 'w_out': w_out[0],
        'norm2': norm2[0], 'peer_q': peer_q[0], 'peer_keys': peer_keys[0],
        'expert_u': expert_u[0], 'expert_v': expert_v[0],
    }
    hist0 = jnp.zeros((x_prompt.shape[0], CONV_K - 1, CONV_WIDTH), x_prompt.dtype)
    y_prompt, (a_c, a_s, a_w), cv = layer(x_prompt, c_prompt, hist0, lp, rel_bias, None)
    y_sample, (b_c, b_s, b_w), cvs = layer(x_sample, c_sample, state_conv[0], lp, rel_bias,
                                           (cache_cmp_kv[0], cache_sel_kv[0], cache_win_kv[0], page_table))
    return (y_prompt, y_sample, a_c[None], a_s[None], a_w[None], cv[None],
            b_c[None], b_s[None], b_w[None], cvs[None])
```

```python
import functools
import math

import jax
import jax.numpy as jnp
import numpy as np
from jax import lax
from jax.experimental import pallas as pl
from jax.experimental.pallas import tpu as pltpu

D_MODEL = 1024
HEAD_DIM = 64
ATTN_WIDTH = D_MODEL // 2
N_HEADS = ATTN_WIDTH // HEAD_DIM
N_KV = 2
GROUP = N_HEADS // N_KV
KV_WIDTH = N_KV * HEAD_DIM
CONV_WIDTH = D_MODEL - ATTN_WIDTH
CONV_K = 31
CMP_LEN = 32
CMP_STRIDE = 16
CMP_HIDDEN = 256
SEL_BLOCK = 64
TOP_N = 16
WINDOW = 512
N_BUCKETS = 32
MAX_DISTANCE = 128
Q_BLOCK = 64
PEER_HEADS = 8
PEER_DQ = 256
N_KEYS = 128
PEER_TOPK = 16
PEER_BLOCK = 256
FORCE_SCORE = 1.0e4
EPS = 1e-6
SPLIT_SIZES = (ATTN_WIDTH, KV_WIDTH, KV_WIDTH, KV_WIDTH, KV_WIDTH, KV_WIDTH, KV_WIDTH, 3 * N_HEADS, 2 * CONV_WIDTH)
IN_COLS = sum(SPLIT_SIZES)

LANES = 128
VMEM_LIMIT = 48 * 1024 * 1024


def _matmul_kernel(x_ref, w_ref, o_ref):
    o_ref[...] = jnp.dot(x_ref[...].astype(jnp.bfloat16), w_ref[...],
                         preferred_element_type=jnp.float32)


def matmul(x, w, tm=512):
    m, k = x.shape
    n = w.shape[1]
    n_pad = -(-n // LANES) * LANES
    wb = jnp.pad(w, ((0, 0), (0, n_pad - n))).astype(jnp.bfloat16)
    tm = min(tm, m)
    out = pl.pallas_call(
        _matmul_kernel,
        grid=(m // tm,),
        in_specs=[pl.BlockSpec((tm, k), lambda i: (i, 0)),
                  pl.BlockSpec((k, n_pad), lambda i: (0, 0))],
        out_specs=pl.BlockSpec((tm, n_pad), lambda i: (i, 0)),
        out_shape=jax.ShapeDtypeStruct((m, n_pad), jnp.float32),
        compiler_params=pltpu.CompilerParams(dimension_semantics=("arbitrary",),
                                             vmem_limit_bytes=VMEM_LIMIT),
        name="matmul",
    )(x, wb)
    return out[:, :n]


NEG = -1.0e30
SEL_MASK = 16384.0
FAR_KEYS = 512
NEAR_BLOCKS = 3
WIN_BLOCKS = WINDOW // SEL_BLOCK + 1


def _bucket_np(d):
    n = np.maximum(d, 0)
    exact = N_BUCKETS // 2
    nf = np.maximum(n, 1).astype(np.float32)
    large = exact + (np.log(nf / np.float32(exact)) / np.float32(math.log(MAX_DISTANCE / exact))
                     * np.float32(N_BUCKETS - exact)).astype(np.int32)
    return np.where(n < exact, n, np.minimum(large, N_BUCKETS - 1))


def _bias_table(rel_bias, dist, valid):
    tbl = rel_bias.astype(jnp.float32) - rel_bias[N_BUCKETS - 1].astype(jnp.float32)[None, :]
    b = jnp.transpose(tbl[_bucket_np(dist)], (0, 3, 1, 2))
    return jnp.where(valid[:, None], b, NEG)


def _group_sum(x, grp, t):
    acc = x[grp * GROUP * t:(grp * GROUP + 1) * t]
    for r in range(1, GROUP):
        acc = acc + x[(grp * GROUP + r) * t:(grp * GROUP + r + 1) * t]
    return acc


def _nsa_prompt_kernel(q_ref, g_ref, kc_ref, vc_ref, ks_ref, vs_ref, kw_ref, vw_ref,
                       bc_ref, bsn_ref, bw_ref, cover_ref, e_ref, o_ref, *, tq):
    f32, bf16 = jnp.float32, jnp.bfloat16
    i = pl.program_id(1)
    rows = N_HEADS * tq
    nt = (((1,), (1,)), ((), ()))

    q = q_ref[0]
    lane_half = lax.broadcasted_iota(jnp.int32, (tq, 2 * HEAD_DIM), 1) // HEAD_DIM
    parts = []
    for h in range(N_HEADS):
        blk = q[:, (h // 2) * 2 * HEAD_DIM:(h // 2 + 1) * 2 * HEAD_DIM]
        grp = h // GROUP
        if h % 2 != grp:
            blk = pltpu.roll(blk, HEAD_DIM, 1)
        parts.append(jnp.where(lane_half == grp, blk, 0.0))
    qpad = jnp.concatenate(parts, axis=0).astype(bf16)

    def heads(x):
        return jnp.concatenate([x[:tq]] * GROUP + [x[tq:]] * GROUP, axis=0)

    tbl_c = bc_ref[0].reshape(rows, bc_ref.shape[-1])
    s = lax.dot_general(qpad, kc_ref[0], nt, preferred_element_type=f32) + tbl_c
    ok = tbl_c > 0.5 * NEG
    m = jnp.max(s, axis=1, keepdims=True)
    e = jnp.where(ok, jnp.exp(s - m), 0.0)
    d = jnp.sum(e, axis=1, keepdims=True)
    pc = e / jnp.where(d > 0.0, d, 1.0)
    pc16 = pc.astype(bf16)
    o_cmp = jnp.dot(pc16, vc_ref[0], preferred_element_type=f32)

    imp_h = jnp.dot(pc16, cover_ref[...], preferred_element_type=f32)
    imp = jnp.concatenate([_group_sum(imp_h, grp, tq) for grp in range(N_KV)], axis=0)
    ns = imp.shape[1]
    j = lax.broadcasted_iota(jnp.int32, imp.shape, 1)
    forced = (j == 0) | (j == i) | (j == i - 1)
    imp = jnp.where(forced, imp + FORCE_SCORE, imp)
    imp = jnp.where(j > i, -FORCE_SCORE, imp)
    cnt = jnp.zeros(imp.shape, jnp.int32)
    for jp in range(ns):
        col = imp[:, jp:jp + 1]
        ahead = (col > imp) | ((col == imp) & (j > jp))
        cnt = cnt + ahead.astype(jnp.int32)
    sel = cnt < TOP_N

    nb0 = jnp.maximum(i - (NEAR_BLOCKS - 1), 0)
    near = NEAR_BLOCKS * SEL_BLOCK
    k0 = pl.multiple_of(nb0 * SEL_BLOCK, SEL_BLOCK)
    sel_nb0 = jnp.sum(jnp.where((j == nb0) & sel, 1.0, 0.0), axis=1, keepdims=True)
    lane = lax.broadcasted_iota(jnp.int32, (rows, near), 1)
    s = lax.dot_general(qpad, ks_ref[0, pl.ds(k0, near), :], nt, preferred_element_type=f32)
    s = s + bsn_ref[0].reshape(rows, near)
    s = s + jnp.where(lane < SEL_BLOCK, (heads(sel_nb0) - 1.0) * SEL_MASK, 0.0)
    m = jnp.max(s, axis=1, keepdims=True)
    p = jnp.exp(s - m)
    l = jnp.sum(p, axis=1, keepdims=True)
    acc = jnp.dot(p.astype(bf16), vs_ref[0, pl.ds(k0, near), :], preferred_element_type=f32)

    sel_far = heads(jnp.where(sel & (j < nb0), 0.0, -1.0)).astype(bf16)
    n_far = (nb0 * SEL_BLOCK + FAR_KEYS - 1) // FAR_KEYS

    def far_body(c, carry):
        m, l, acc = carry
        c0 = pl.multiple_of(c * FAR_KEYS, FAR_KEYS)
        s = lax.dot_general(qpad, ks_ref[0, pl.ds(c0, FAR_KEYS), :], nt, preferred_element_type=f32)
        s = s + jnp.dot(sel_far, e_ref[c], preferred_element_type=f32)
        m_new = jnp.maximum(m, jnp.max(s, axis=1, keepdims=True))
        a = jnp.exp(m - m_new)
        p = jnp.exp(s - m_new)
        l = a * l + jnp.sum(p, axis=1, keepdims=True)
        acc = a * acc + jnp.dot(p.astype(bf16), vs_ref[0, pl.ds(c0, FAR_KEYS), :],
                                preferred_element_type=f32)
        return m_new, l, acc

    m, l, acc = lax.fori_loop(0, n_far, far_body, (m, l, acc))
    o_sel = acc / l

    wk = WIN_BLOCKS * SEL_BLOCK
    w0 = pl.multiple_of(jnp.maximum(i - (WIN_BLOCKS - 1), 0) * SEL_BLOCK, SEL_BLOCK)
    s = lax.dot_general(qpad, kw_ref[0, pl.ds(w0, wk), :], nt, preferred_element_type=f32)
    s = s + bw_ref[0].reshape(rows, wk)
    m = jnp.max(s, axis=1, keepdims=True)
    p = jnp.exp(s - m)
    l = jnp.sum(p, axis=1, keepdims=True)
    o_win = jnp.dot(p.astype(bf16), vw_ref[0, pl.ds(w0, wk), :], preferred_element_type=f32) / l

    gates = g_ref[0]
    for h in range(N_HEADS):
        grp = h // GROUP
        r0, r1 = h * tq, (h + 1) * tq
        c0, c1 = grp * HEAD_DIM, (grp + 1) * HEAD_DIM
        o_h = (gates[:, 3 * h:3 * h + 1] * o_cmp[r0:r1, c0:c1]
               + gates[:, 3 * h + 1:3 * h + 2] * o_sel[r0:r1, c0:c1]
               + gates[:, 3 * h + 2:3 * h + 3] * o_win[r0:r1, c0:c1])
        o_ref[0, :, h * HEAD_DIM:(h + 1) * HEAD_DIM] = o_h


def nsa_prompt_attention(q, gates, kc, vc, ks, vs, kw, vw, rel_bias):
    B, T = q.shape[0], q.shape[1]
    tq = SEL_BLOCK
    nq = T // tq
    ns = T // SEL_BLOCK
    nc = kc.shape[1]
    ncp = -(-nc // LANES) * LANES
    bf16 = jnp.bfloat16
    scale = HEAD_DIM ** -0.5
    assert T % FAR_KEYS == 0 and T >= WIN_BLOCKS * SEL_BLOCK

    tl = np.arange(tq)[None, :, None]
    r = tq // CMP_STRIDE
    span = r * (nq - 1) + ncp
    y = np.arange(span)[None, None, :]
    dy = tl + CMP_STRIDE * (y - (ncp - 1)) - (CMP_LEN - 1)
    master = _bias_table(rel_bias, dy, dy >= 0)[0][:, :, ::-1]
    pad_col = jnp.arange(ncp) < nc
    bc = jnp.stack([jnp.where(pad_col, master[:, :, r * (nq - 1 - i):r * (nq - 1 - i) + ncp], NEG)
                    for i in range(nq)])
    vv = np.arange(NEAR_BLOCKS)[:, None, None]
    dn = vv * tq + tl - np.arange(NEAR_BLOCKS * SEL_BLOCK)[None, None, :]
    bsn = _bias_table(rel_bias, dn, dn >= 0)
    vw_ = np.arange(WIN_BLOCKS)[:, None, None]
    dw = vw_ * tq + tl - np.arange(WIN_BLOCKS * SEL_BLOCK)[None, None, :]
    bw = _bias_table(rel_bias, dw, (dw >= 0) & (dw < WINDOW))

    ci = np.arange(ncp)[:, None]
    sj = np.arange(ns)[None, :]
    cover = ((ci * CMP_STRIDE < (sj + 1) * SEL_BLOCK) & (ci * CMP_STRIDE + CMP_LEN > sj * SEL_BLOCK)
             & (ci < nc))
    cover = jnp.asarray(cover.astype(np.float32), bf16)
    pos = np.arange(T).reshape(T // FAR_KEYS, 1, FAR_KEYS)
    expand = jnp.asarray(np.where(pos // SEL_BLOCK == np.arange(ns)[None, :, None], SEL_MASK, 0.0)
                         .astype(np.float32), bf16)

    padc = ((0, 0), (0, ncp - nc), (0, 0))
    seq = lambda b, i: (b, 0, 0)
    return pl.pallas_call(
        functools.partial(_nsa_prompt_kernel, tq=tq),
        grid=(B, nq),
        in_specs=[
            pl.BlockSpec((1, tq, ATTN_WIDTH), lambda b, i: (b, i, 0)),
            pl.BlockSpec((1, tq, 3 * N_HEADS), lambda b, i: (b, i, 0)),
            pl.BlockSpec((1, ncp, KV_WIDTH), seq),
            pl.BlockSpec((1, ncp, KV_WIDTH), seq),
            pl.BlockSpec((1, T, KV_WIDTH), seq),
            pl.BlockSpec((1, T, KV_WIDTH), seq),
            pl.BlockSpec((1, T, KV_WIDTH), seq),
            pl.BlockSpec((1, T, KV_WIDTH), seq),
            pl.BlockSpec((1, N_HEADS, tq, ncp), lambda b, i: (i, 0, 0, 0)),
            pl.BlockSpec((1, N_HEADS, tq, NEAR_BLOCKS * SEL_BLOCK),
                         lambda b, i: (jnp.minimum(i, NEAR_BLOCKS - 1), 0, 0, 0)),
            pl.BlockSpec((1, N_HEADS, tq, WIN_BLOCKS * SEL_BLOCK),
                         lambda b, i: (jnp.minimum(i, WIN_BLOCKS - 1), 0, 0, 0)),
            pl.BlockSpec((ncp, ns), lambda b, i: (0, 0)),
            pl.BlockSpec((T // FAR_KEYS, ns, FAR_KEYS), lambda b, i: (0, 0, 0)),
        ],
        out_specs=pl.BlockSpec((1, tq, ATTN_WIDTH), lambda b, i: (b, i, 0)),
        out_shape=jax.ShapeDtypeStruct((B, T, ATTN_WIDTH), jnp.float32),
        compiler_params=pltpu.CompilerParams(dimension_semantics=("arbitrary", "arbitrary"),
                                             vmem_limit_bytes=VMEM_LIMIT),
        name="nsa_prompt",
    )(q * scale, gates, jnp.pad(kc, padc).astype(bf16), jnp.pad(vc, padc).astype(bf16),
      ks.astype(bf16), vs.astype(bf16), kw.astype(bf16), vw.astype(bf16), bc, bsn, bw, cover, expand)


NEW_PAD = 16
KV_SLOTS = 3


def _softmax_parts(s):
    m = jnp.max(s, axis=1, keepdims=True)
    p = jnp.exp(s - m)
    return m, p, jnp.sum(p, axis=1, keepdims=True)


def _nsa_sample_kernel(pt_ref, q_ref, g_ref, kc_ref, vc_ref, cache_ref, nsk_ref, nsv_ref, cw_ref, nwk_ref, nwv_ref,
                       bc_ref, bsl_ref, bnew_ref, bw_ref, cover_ref, e_ref, o_ref, kv_buf, sem, *, tn, ns):
    f32, bf16 = jnp.float32, jnp.bfloat16
    nt = (((1,), (1,)), ((), ()))
    b = pl.program_id(0)
    n_pages, page = pt_ref.shape[1], cache_ref.shape[1]
    pages_per_chunk = FAR_KEYS // page
    n_far = n_pages // pages_per_chunk
    total = pl.num_programs(0) * n_far

    def fetch(gc):
        seq, first = gc // n_far, (gc % n_far) * pages_per_chunk
        slot = gc % KV_SLOTS
        for i in range(pages_per_chunk):
            src_page = pt_ref[seq, first + i]
            for kv in range(2):
                for grp in range(N_KV):
                    pltpu.make_async_copy(cache_ref.at[src_page, :, kv, grp, :],
                                          kv_buf.at[slot, kv * N_KV + grp, pl.ds(i * page, page), :],
                                          sem.at[slot]).start()

    @pl.when(b == 0)
    def _():
        for gc in range(KV_SLOTS - 1):
            fetch(gc)
    qpad = q_ref[0]
    cur = ns - 1

    def heads(x):
        return jnp.concatenate([x[:tn]] * GROUP + [x[tn:]] * GROUP, axis=0)

    s = lax.dot_general(qpad, kc_ref[0], nt, preferred_element_type=f32) + bc_ref[...]
    _, e, d = _softmax_parts(s)
    pc = e / d
    pc16 = pc.astype(bf16)
    o_cmp = jnp.dot(pc16, vc_ref[0], preferred_element_type=f32)

    imp_h = jnp.dot(pc16, cover_ref[...], preferred_element_type=f32)
    imp = jnp.concatenate([_group_sum(imp_h, grp, tn) for grp in range(N_KV)], axis=0)
    j = lax.broadcasted_iota(jnp.int32, imp.shape, 1)
    forced = (j == 0) | (j == cur) | (j == cur - 1)
    imp = jnp.where(forced, imp + FORCE_SCORE, imp)
    imp = jnp.where(j > cur, -jnp.inf, imp)
    cnt = jnp.zeros(imp.shape, jnp.int32)
    for jp in range(ns):
        col = imp[:, jp:jp + 1]
        ahead = (col > imp) | ((col == imp) & (j > jp))
        cnt = cnt + ahead.astype(jnp.int32)
    sel = cnt < TOP_N
    sel_far = heads(jnp.where(sel & (j < cur), 0.0, -1.0)).astype(bf16)

    bnew = bnew_ref[...]
    m, p, l = _softmax_parts(lax.dot_general(qpad, nsk_ref[0], nt, preferred_element_type=f32) + bnew)
    acc = jnp.dot(p.astype(bf16), nsv_ref[0], preferred_element_type=f32)
    half = GROUP * tn
    q_g = [qpad[grp * half:(grp + 1) * half, grp * HEAD_DIM:(grp + 1) * HEAD_DIM] for grp in range(N_KV)]
    zero = jnp.zeros((half, HEAD_DIM), f32)

    def far_chunk(c, carry, bias):
        m, l, acc = carry
        gc = b * n_far + c
        slot = gc % KV_SLOTS
        pltpu.make_async_copy(kv_buf.at[slot], kv_buf.at[slot], sem.at[slot]).wait()

        @pl.when(gc + KV_SLOTS - 1 < total)
        def _():
            fetch(gc + KV_SLOTS - 1)

        s = jnp.concatenate([lax.dot_general(q_g[grp], kv_buf[slot, grp].astype(bf16), nt,
                                             preferred_element_type=f32) for grp in range(N_KV)], axis=0)
        s = s + jnp.dot(sel_far, e_ref[c], preferred_element_type=f32)
        if bias is not None:
            s = s + bias
        m_new = jnp.maximum(m, jnp.max(s, axis=1, keepdims=True))
        a = jnp.exp(m - m_new)
        p = jnp.exp(s - m_new)
        l = a * l + jnp.sum(p, axis=1, keepdims=True)
        p = p.astype(bf16)
        pv = [jnp.dot(p[grp * half:(grp + 1) * half], kv_buf[slot, N_KV + grp].astype(bf16),
                      preferred_element_type=f32) for grp in range(N_KV)]
        pv = jnp.concatenate([jnp.concatenate([pv[0], zero], axis=1),
                              jnp.concatenate([zero, pv[1]], axis=1)], axis=0)
        return m_new, l, a * acc + pv

    carry = lax.fori_loop(0, n_far - 1, lambda c, carry: far_chunk(c, carry, None), (m, l, acc))
    m, l, acc = far_chunk(n_far - 1, carry, bsl_ref[...])
    o_sel = acc / l

    cw = cw_ref[0]
    s_c = lax.dot_general(qpad, cw[:, :KV_WIDTH].astype(bf16), nt, preferred_element_type=f32) + bw_ref[...]
    s_n = lax.dot_general(qpad, nwk_ref[0], nt, preferred_element_type=f32) + bnew
    m = jnp.maximum(jnp.max(s_c, axis=1, keepdims=True), jnp.max(s_n, axis=1, keepdims=True))
    p_c = jnp.exp(s_c - m)
    p_n = jnp.exp(s_n - m)
    l = jnp.sum(p_c, axis=1, keepdims=True) + jnp.sum(p_n, axis=1, keepdims=True)
    o_win = (jnp.dot(p_c.astype(bf16), cw[:, KV_WIDTH:].astype(bf16), preferred_element_type=f32)
             + jnp.dot(p_n.astype(bf16), nwv_ref[0], preferred_element_type=f32)) / l

    g = g_ref[0]
    o_ref[0] = g[:, 0:1] * o_cmp + g[:, 1:2] * o_sel + g[:, 2:3] * o_win


def nsa_sample_attention(q, gates, kc, vc, cache_sel, page_table, new_sel, cache_win, new_win, rel_bias):
    B, tn = q.shape[0], q.shape[1]
    past, wk, nc = page_table.shape[1] * cache_sel.shape[1], cache_win.shape[1], kc.shape[1]
    assert past % FAR_KEYS == 0 and past % SEL_BLOCK == 0 and tn <= NEW_PAD and wk == WINDOW
    ns = past // SEL_BLOCK + 1
    nsp = -(-ns // LANES) * LANES
    ncp = -(-nc // LANES) * LANES
    rows = N_HEADS * tn
    bf16 = jnp.bfloat16
    n_far = past // FAR_KEYS

    tl = np.arange(tn)[:, None]
    flat = lambda t: t[0].reshape(rows, t.shape[-1])
    cc = np.arange(ncp)[None, :]
    dc = past + tl - (cc * CMP_STRIDE + CMP_LEN - 1)
    bc = flat(_bias_table(rel_bias, dc[None], ((dc >= 0) & (cc < nc))[None]))
    dl = tl + FAR_KEYS - np.arange(FAR_KEYS)[None, :]
    bsl = flat(_bias_table(rel_bias, dl[None], (dl >= 0)[None]))
    jn = np.arange(NEW_PAD)[None, :]
    dn = tl - jn
    bnew = flat(_bias_table(rel_bias, dn[None], ((dn >= 0) & (jn < tn))[None]))
    dw = wk + tl - np.arange(wk)[None, :]
    bw = flat(_bias_table(rel_bias, dw[None], ((dw >= 0) & (dw < WINDOW))[None]))

    ci = np.arange(ncp)[:, None]
    sj = np.arange(nsp)[None, :]
    cover = ((ci * CMP_STRIDE < (sj + 1) * SEL_BLOCK) & (ci * CMP_STRIDE + CMP_LEN > sj * SEL_BLOCK)
             & (ci < nc) & (sj < ns))
    cover = jnp.asarray(cover.astype(np.float32), bf16)
    pos = np.arange(past).reshape(n_far, 1, FAR_KEYS)
    expand = jnp.asarray(np.where(pos // SEL_BLOCK == np.arange(nsp)[None, :, None], SEL_MASK, 0.0)
                         .astype(np.float32), bf16)

    qh = jnp.transpose((q * HEAD_DIM ** -0.5).reshape(B, tn, N_HEADS, HEAD_DIM), (0, 2, 1, 3))
    zero = jnp.zeros_like(qh)
    grp = (np.arange(N_HEADS) // GROUP)[None, :, None, None]
    qpad = jnp.where(grp == 0, jnp.concatenate([qh, zero], -1), jnp.concatenate([zero, qh], -1))
    qpad = qpad.reshape(B, rows, KV_WIDTH).astype(bf16)
    grow = jnp.transpose(gates.reshape(B, tn, N_HEADS, 3), (0, 2, 1, 3)).reshape(B, rows, 3)
    padn = lambda a: jnp.pad(a, ((0, 0), (0, NEW_PAD - tn), (0, 0))).astype(bf16)
    padc = ((0, 0), (0, ncp - nc), (0, 0))

    seq3 = lambda b, pt: (b, 0, 0)
    const2 = lambda b, pt: (0, 0)
    grid_spec = pltpu.PrefetchScalarGridSpec(
        num_scalar_prefetch=1,
        grid=(B,),
        in_specs=[
            pl.BlockSpec((1, rows, KV_WIDTH), seq3),
            pl.BlockSpec((1, rows, 3), seq3),
            pl.BlockSpec((1, ncp, KV_WIDTH), seq3),
            pl.BlockSpec((1, ncp, KV_WIDTH), seq3),
            pl.BlockSpec(memory_space=pl.ANY),
            pl.BlockSpec((1, NEW_PAD, KV_WIDTH), seq3),
            pl.BlockSpec((1, NEW_PAD, KV_WIDTH), seq3),
            pl.BlockSpec((1, wk, 2 * KV_WIDTH), seq3),
            pl.BlockSpec((1, NEW_PAD, KV_WIDTH), seq3),
            pl.BlockSpec((1, NEW_PAD, KV_WIDTH), seq3),
            pl.BlockSpec((rows, ncp), const2),
            pl.BlockSpec((rows, FAR_KEYS), const2),
            pl.BlockSpec((rows, NEW_PAD), const2),
            pl.BlockSpec((rows, wk), const2),
            pl.BlockSpec((ncp, nsp), const2),
            pl.BlockSpec((n_far, nsp, FAR_KEYS), lambda b, pt: (0, 0, 0)),
        ],
        out_specs=pl.BlockSpec((1, rows, KV_WIDTH), seq3),
        scratch_shapes=[pltpu.VMEM((KV_SLOTS, 2 * N_KV, FAR_KEYS, HEAD_DIM), jnp.float32),
                        pltpu.SemaphoreType.DMA((KV_SLOTS,))],
    )
    out = pl.pallas_call(
        functools.partial(_nsa_sample_kernel, tn=tn, ns=ns),
        grid_spec=grid_spec,
        out_shape=jax.ShapeDtypeStruct((B, rows, KV_WIDTH), jnp.float32),
        compiler_params=pltpu.CompilerParams(dimension_semantics=("arbitrary",),
                                             vmem_limit_bytes=VMEM_LIMIT),
        name="nsa_sample",
    )(page_table, qpad, grow, jnp.pad(kc, padc).astype(bf16), jnp.pad(vc, padc).astype(bf16), cache_sel,
      padn(new_sel[:, :, :KV_WIDTH]), padn(new_sel[:, :, KV_WIDTH:]), cache_win,
      padn(new_win[:, :, :KV_WIDTH]), padn(new_win[:, :, KV_WIDTH:]), bc, bsl, bnew, bw, cover, expand)
    o = out.reshape(B, N_HEADS, tn, 2, HEAD_DIM)
    o = jnp.where(grp[..., None] == 0, o[:, :, :, 0:1], o[:, :, :, 1:2])[:, :, :, 0]
    return jnp.transpose(o, (0, 2, 1, 3)).reshape(B, tn, ATTN_WIDTH)


N_SUB = 2 * PEER_HEADS
HALF_DQ = PEER_DQ // 2
N_PAIRS = PEER_HEADS * PEER_TOPK


def _top_rounds(s, row, n):
    width = s.shape[0]
    k = lax.broadcasted_iota(jnp.int32, (n, s.shape[1]), 0)
    vals = jnp.zeros((n, s.shape[1]), jnp.float32)
    ids = jnp.zeros((n, s.shape[1]), jnp.float32)
    for a in range(n):
        m = jnp.max(s, axis=0, keepdims=True)
        ix = jnp.min(jnp.where(s == m, row, float(width)), axis=0, keepdims=True)
        vals = jnp.where(k == a, m, vals)
        ids = jnp.where(k == a, ix, ids)
        s = jnp.where(row == ix, -jnp.inf, s)
    return vals, ids


def _peer_route_kernel(h_ref, wq_ref, keys_ref, idx_ref, g_ref, q_scr, e_scr, w_scr):
    f32, bf16 = jnp.float32, jnp.bfloat16
    tm = h_ref.shape[0]
    nt = (((1,), (1,)), ((), ()))
    q = jnp.dot(h_ref[...].astype(bf16), wq_ref[...], preferred_element_type=f32)
    for k in range(N_SUB):
        q_scr[k] = q[:, k * HALF_DQ:(k + 1) * HALF_DQ].astype(bf16)
    row = lax.broadcasted_iota(jnp.int32, (N_KEYS, tm), 0).astype(f32)
    n_cand = PEER_TOPK * PEER_TOPK
    row2 = lax.broadcasted_iota(jnp.int32, (n_cand, tm), 0).astype(f32)

    def head(h, carry):
        s1, i1 = _top_rounds(lax.dot_general(keys_ref[2 * h], q_scr[2 * h], nt, preferred_element_type=f32),
                             row, PEER_TOPK)
        s2, i2 = _top_rounds(lax.dot_general(keys_ref[2 * h + 1], q_scr[2 * h + 1], nt,
                                             preferred_element_type=f32), row, PEER_TOPK)
        cand = jnp.concatenate([s1[a:a + 1] + s2 for a in range(PEER_TOPK)], axis=0)
        cidx = jnp.concatenate([i1[a:a + 1] * float(N_KEYS) + i2 for a in range(PEER_TOPK)], axis=0)
        k = lax.broadcasted_iota(jnp.int32, (PEER_TOPK, tm), 0)
        top = jnp.zeros((PEER_TOPK, tm), f32)
        eid = jnp.zeros((PEER_TOPK, tm), f32)
        for a in range(PEER_TOPK):
            m = jnp.max(cand, axis=0, keepdims=True)
            pos = jnp.min(jnp.where(cand == m, row2, float(n_cand)), axis=0, keepdims=True)
            hit = row2 == pos
            e = jnp.max(jnp.where(hit, cidx, -1.0), axis=0, keepdims=True)
            top = jnp.where(k == a, m, top)
            eid = jnp.where(k == a, e, eid)
            cand = jnp.where(hit, -jnp.inf, cand)
        ex = jnp.exp(top - top[0:1])
        r0 = pl.multiple_of(h * PEER_TOPK, PEER_TOPK)
        e_scr[pl.ds(r0, PEER_TOPK), :] = eid
        w_scr[pl.ds(r0, PEER_TOPK), :] = ex / jnp.sum(ex, axis=0, keepdims=True)
        return carry

    lax.fori_loop(0, PEER_HEADS, head, 0)
    idx_ref[...] = e_scr[...].T.astype(jnp.int32)
    g_ref[...] = w_scr[...].T


def peer_route(h2, peer_q, peer_keys, tm=128):
    n = h2.shape[0]
    tm = min(tm, n)
    bf16 = jnp.bfloat16
    keys = peer_keys.reshape(N_SUB, N_KEYS, HALF_DQ).astype(bf16)
    return pl.pallas_call(
        _peer_route_kernel,
        grid=(n // tm,),
        in_specs=[pl.BlockSpec((tm, D_MODEL), lambda i: (i, 0)),
                  pl.BlockSpec((D_MODEL, PEER_HEADS * PEER_DQ), lambda i: (0, 0)),
                  pl.BlockSpec((N_SUB, N_KEYS, HALF_DQ), lambda i: (0, 0, 0))],
        out_specs=[pl.BlockSpec((tm, N_PAIRS), lambda i: (i, 0)),
                   pl.BlockSpec((tm, N_PAIRS), lambda i: (i, 0))],
        out_shape=[jax.ShapeDtypeStruct((n, N_PAIRS), jnp.int32),
                   jax.ShapeDtypeStruct((n, N_PAIRS), jnp.float32)],
        scratch_shapes=[pltpu.VMEM((N_SUB, tm, HALF_DQ), bf16),
                        pltpu.VMEM((N_PAIRS, tm), jnp.float32),
                        pltpu.VMEM((N_PAIRS, tm), jnp.float32)],
        compiler_params=pltpu.CompilerParams(dimension_semantics=("arbitrary",),
                                             vmem_limit_bytes=VMEM_LIMIT),
        name="peer_route",
    )(h2, peer_q.astype(bf16), keys)


def _gelu_tanh(x):
    return 0.5 * x * (1.0 + jnp.tanh(math.sqrt(2.0 / math.pi) * (x + 0.044715 * (x * x * x))))


N_SLOTS = 8
ROW_TILES = 2 * D_MODEL // LANES
U_TILES = D_MODEL // LANES


def _peer_expert_kernel(idx_ref, nxt_ref, x_ref, g_ref, uv_ref, y_ref, *scratch):
    tb = x_ref.shape[0]
    i = pl.program_id(0)
    last = pl.num_programs(0) - 1
    ahead = N_SLOTS - 1
    bufs, sem = scratch[:N_SLOTS], scratch[N_SLOTS]

    def issue(ids, t, slot):
        for k in range(N_PAIRS):
            pltpu.make_async_copy(uv_ref.at[ids[t, k]], bufs[slot].at[k // 8, :, k % 8, :], sem.at[slot]).start()

    def wait_slot(slot):
        pltpu.make_async_copy(bufs[slot], bufs[slot], sem.at[slot]).wait()

    eye = (lax.broadcasted_iota(jnp.int32, (N_PAIRS, N_PAIRS), 0)
           == lax.broadcasted_iota(jnp.int32, (N_PAIRS, N_PAIRS), 1))

    def compute(t, slot):
        buf = bufs[slot]

        def chunk(c):
            return buf[:, c].reshape(N_PAIRS, LANES)

        x = x_ref[t]
        acc = chunk(0) * x[0:1, :]
        for s in range(1, U_TILES):
            acc = acc + chunk(s) * x[s:s + 1, :]
        a = jnp.sum(acc, axis=1, keepdims=True)
        g_col = jnp.sum(jnp.where(eye, g_ref[pl.ds(t, 1), :], 0.0), axis=1, keepdims=True)
        w = g_col * _gelu_tanh(a)
        y = [jnp.sum(w * chunk(U_TILES + s), axis=0, keepdims=True) for s in range(U_TILES)]
        y_ref[pl.ds(t, 1), :] = jnp.concatenate(y, axis=1)

    @pl.when(i == 0)
    def _():
        for t in range(ahead):
            issue(idx_ref, t, t)

    def group(j, c):
        for u in range(N_SLOTS):
            t = j * N_SLOTS + u
            wait_slot(u)
            issue(idx_ref, t + ahead, (u + ahead) % N_SLOTS)
            compute(t, u)
        return c

    n_groups = tb // N_SLOTS
    lax.fori_loop(0, n_groups - 1, group, 0)
    for u in range(N_SLOTS):
        t = tb - N_SLOTS + u
        wait_slot(u)
        if u == 0:
            issue(idx_ref, t + ahead, ahead)
        else:
            issue(nxt_ref, u - 1, u - 1)
        compute(t, u)

    @pl.when(i == last)
    def _():
        for u in range(ahead):
            wait_slot(u)


def peer_experts(h2, eidx, gate, expert_u, expert_v, tb=64):
    n = h2.shape[0]
    tb = min(tb, n)
    nb = n // tb
    e = expert_u.shape[0]
    uv = jnp.concatenate([expert_u.reshape(e, U_TILES, LANES), expert_v.reshape(e, U_TILES, LANES)], axis=1)
    return pl.pallas_call(
        _peer_expert_kernel,
        grid=(nb,),
        in_specs=[pl.BlockSpec((tb, N_PAIRS), lambda i: (i, 0), memory_space=pltpu.SMEM),
                  pl.BlockSpec((tb, N_PAIRS), lambda i: (jnp.minimum(i + 1, nb - 1), 0), memory_space=pltpu.SMEM),
                  pl.BlockSpec((tb, U_TILES, LANES), lambda i: (i, 0, 0)),
                  pl.BlockSpec((tb, N_PAIRS), lambda i: (i, 0)),
                  pl.BlockSpec(memory_space=pl.ANY)],
        out_specs=pl.BlockSpec((tb, D_MODEL), lambda i: (i, 0)),
        out_shape=jax.ShapeDtypeStruct((n, D_MODEL), jnp.float32),
        scratch_shapes=[pltpu.VMEM((N_PAIRS // 8, ROW_TILES, 8, LANES), jnp.float32)] * N_SLOTS
        + [pltpu.SemaphoreType.DMA((N_SLOTS,))],
        compiler_params=pltpu.CompilerParams(dimension_semantics=("arbitrary",),
                                             vmem_limit_bytes=VMEM_LIMIT),
        name="peer_experts",
    )(eidx, eidx, h2.reshape(n, U_TILES, LANES), gate, uv)


def rms_norm(x, g):
    y = x * lax.rsqrt(jnp.mean(x * x, axis=-1, keepdims=True) + EPS)
    return y * g


def layer_norm(x, g, b):
    mu = jnp.mean(x, axis=-1, keepdims=True)
    var = jnp.mean(jnp.square(x - mu), axis=-1, keepdims=True)
    return (x - mu) * lax.rsqrt(var + EPS) * g + b


def masked_softmax(logits, mask, axis):
    neg = jnp.where(mask, logits, -jnp.inf)
    m = jnp.max(neg, axis=axis, keepdims=True)
    m = jnp.where(jnp.isfinite(m), m, 0.0)
    e = jnp.where(mask, jnp.exp(neg - m), 0.0)
    d = jnp.sum(e, axis=axis, keepdims=True)
    return e / jnp.where(d > 0.0, d, 1.0)


def t5_bucket(dist):
    n = jnp.maximum(dist, 0)
    exact = N_BUCKETS // 2
    nf = jnp.maximum(n, 1).astype(jnp.float32)
    large = exact + (jnp.log(nf / exact) / math.log(MAX_DISTANCE / exact) * (N_BUCKETS - exact)).astype(jnp.int32)
    return jnp.where(n < exact, n, jnp.minimum(large, N_BUCKETS - 1))


def compress(rows, pe, w1, b1, w2, b2):
    B, L = rows.shape[0], rows.shape[1]
    ratio = CMP_LEN // CMP_STRIDE
    n_chunks = L // CMP_STRIDE
    nc = n_chunks - ratio + 1
    r = rows[:, :n_chunks * CMP_STRIDE].reshape(B, n_chunks, CMP_STRIDE, N_KV, HEAD_DIM)
    w1c = w1.reshape(ratio, CMP_STRIDE, HEAD_DIM, CMP_HIDDEN)
    parts = jnp.einsum('bcsgd,jsdh->jbcgh', r, w1c)
    hid = b1 + pe.reshape(-1) @ w1
    for j in range(ratio):
        hid = hid + parts[j, :, j:j + nc]
    out = jax.nn.gelu(hid) @ w2 + b2
    end = jnp.arange(nc, dtype=jnp.int32) * CMP_STRIDE + (CMP_LEN - 1)
    return out, end


def compress_branch(rows, lp, which):
    return compress(rows, lp['cmp_pe'][which], lp['cmp_w1'][which], lp['cmp_b1'][which],
                    lp['cmp_w2'][which], lp['cmp_b2'][which])


def to_blocks(rows):
    B, L = rows.shape[0], rows.shape[1]
    ns = -(-L // SEL_BLOCK)
    rows = jnp.pad(rows, ((0, 0), (0, ns * SEL_BLOCK - L), (0, 0), (0, 0)))
    return jnp.transpose(rows.reshape(B, ns, SEL_BLOCK, N_KV, HEAD_DIM), (0, 3, 1, 2, 4))


def nsa_block(q, t_pos, gates, kc, vc, kc_end, ks_blk, vs_blk, kw, vw, w_pos, rel_bias):
    B = q.shape[0]
    scale = HEAD_DIM ** -0.5
    tbl = rel_bias.astype(jnp.float32).reshape(N_BUCKETS, N_KV, GROUP)
    lc = jnp.einsum('bqgrd,bcgd->bqgrc', q, kc).astype(jnp.float32) * scale
    dc = t_pos[:, None] - kc_end[None, :]
    lc = lc + jnp.transpose(tbl[t5_bucket(dc)], (0, 2, 3, 1))[None]
    pc = masked_softmax(lc, (dc >= 0)[None, :, None, None, :], axis=-1)
    o_cmp = jnp.einsum('bqgrc,bcgd->bqgrd', pc.astype(vc.dtype), vc)
    nc, ns = kc.shape[1], ks_blk.shape[2]
    ci = jnp.arange(nc)[:, None]
    sj = jnp.arange(ns)[None, :]
    cover = ((ci * CMP_STRIDE < (sj + 1) * SEL_BLOCK) & (ci * CMP_STRIDE + CMP_LEN > sj * SEL_BLOCK)).astype(jnp.float32)
    imp = jnp.einsum('bqgrc,cj->bqgj', pc, cover)
    cur = (t_pos // SEL_BLOCK)[:, None]
    forced = (sj == 0) | (sj == cur) | (sj == cur - 1)
    valid = sj * SEL_BLOCK <= t_pos[:, None]
    imp = jnp.where(forced[None, :, None, :], imp + FORCE_SCORE, imp)
    imp = jnp.where(valid[None, :, None, :], imp, -FORCE_SCORE)
    _, idx = lax.top_k(imp, min(TOP_N, ns))
    bi = jnp.arange(B)[:, None, None, None]
    gi = jnp.arange(N_KV)[None, None, :, None]
    ks_g = ks_blk[bi, gi, idx]
    vs_g = vs_blk[bi, gi, idx]
    ls = jnp.einsum('bqgrd,bqgnkd->bqgrnk', q, ks_g).astype(jnp.float32) * scale
    pos = idx[..., None] * SEL_BLOCK + jnp.arange(SEL_BLOCK, dtype=jnp.int32)
    ds = t_pos[None, :, None, None, None] - pos
    tblk = jnp.transpose(tbl, (1, 0, 2))
    bs = tblk[jnp.arange(N_KV)[None, None, :, None, None], t5_bucket(ds)]
    ls = ls + jnp.moveaxis(bs, -1, 3)
    ps = masked_softmax(ls, (ds >= 0)[:, :, :, None], axis=(-2, -1))
    o_sel = jnp.einsum('bqgrnk,bqgnkd->bqgrd', ps.astype(vs_g.dtype), vs_g)
    lw = jnp.einsum('bqgrd,bwgd->bqgrw', q, kw).astype(jnp.float32) * scale
    dw = t_pos[:, None] - w_pos[None, :]
    lw = lw + jnp.transpose(tbl[t5_bucket(dw)], (0, 2, 3, 1))[None]
    mw = (dw >= 0) & (dw < WINDOW) & (w_pos >= 0)[None, :]
    pw = masked_softmax(lw, mw[None, :, None, None, :], axis=-1)
    o_win = jnp.einsum('bqgrw,bwgd->bqgrd', pw.astype(vw.dtype), vw)
    return gates[..., 0:1] * o_cmp + gates[..., 1:2] * o_sel + gates[..., 2:3] * o_win


def nsa_prompt(q, gates, kc_raw, vc_raw, ks, vs, kw, vw, lp, rel_bias):
    B, T = q.shape[0], q.shape[1]
    kc, kc_end = compress_branch(kc_raw, lp, 0)
    kc = rms_norm(kc, lp['k_norm'][0])
    vc, _ = compress_branch(vc_raw, lp, 1)
    flat = lambda a: a.reshape(B, a.shape[1], -1)
    o = nsa_prompt_attention(flat(q), flat(gates), flat(kc), flat(vc), flat(ks), flat(vs), flat(kw), flat(vw),
                             rel_bias)
    new_win = jnp.stack([kw, vw], axis=2)[:, T - min(WINDOW, T):]
    return o, (jnp.stack([kc_raw, vc_raw], axis=2), jnp.stack([ks, vs], axis=2), new_win)


def nsa_sample(q, gates, kc_raw, vc_raw, ks, vs, kw, vw, cache_cmp, cache_sel, cache_win, page_table, lp, rel_bias):
    Bd, Tn = q.shape[0], q.shape[1]
    past = page_table.shape[1] * cache_cmp.shape[1]
    t_pos = past + jnp.arange(Tn, dtype=jnp.int32)
    past_cmp = cache_cmp[page_table].reshape(Bd, past, 2, N_KV, HEAD_DIM)
    kc, kc_end = compress_branch(jnp.concatenate([past_cmp[:, :, 0], kc_raw], axis=1), lp, 0)
    kc = rms_norm(kc, lp['k_norm'][0])
    vc, _ = compress_branch(jnp.concatenate([past_cmp[:, :, 1], vc_raw], axis=1), lp, 1)
    wk = cache_win.shape[1]
    flat = lambda a: a.reshape(Bd, a.shape[1], -1)
    new_sel = jnp.concatenate([flat(ks), flat(vs)], axis=-1)
    new_wkv = jnp.concatenate([flat(kw), flat(vw)], axis=-1)
    o = nsa_sample_attention(flat(q), flat(gates), flat(kc), flat(vc), cache_sel, page_table, new_sel,
                             flat(cache_win), new_wkv, rel_bias)
    win = jnp.concatenate([cache_win, jnp.stack([kw, vw], axis=2)], axis=1)
    new_win = win[:, wk + Tn - min(WINDOW, wk + Tn):]
    return o.reshape(Bd, Tn, ATTN_WIDTH), (jnp.stack([kc_raw, vc_raw], axis=2), jnp.stack([ks, vs], axis=2), new_win)


def conformer_conv(glu_in, hist, lp):
    u = glu_in[..., :CONV_WIDTH] * jax.nn.sigmoid(glu_in[..., CONV_WIDTH:])
    uh = jnp.concatenate([hist.astype(u.dtype), u], axis=1)
    y = lax.conv_general_dilated(uh, lp['conv_w'][:, None, :], window_strides=(1,), padding='VALID',
                                 dimension_numbers=('NWC', 'WIO', 'NWC'), feature_group_count=CONV_WIDTH)
    y = jax.nn.silu(layer_norm(y + lp['conv_b'], lp['conv_ln_g'], lp['conv_ln_b']))
    return y, uh[:, uh.shape[1] - (CONV_K - 1):]


def peer(h, lp):
    shp = h.shape
    xt = h.reshape(-1, D_MODEL)
    n = xt.shape[0]
    eidx, gate = peer_route(xt, lp['peer_q'], lp['peer_keys'])
    y = peer_experts(xt, eidx, gate, lp['expert_u'], lp['expert_v'])
    return y.reshape(shp)


def split_points():
    pts, acc = [], 0
    for s in SPLIT_SIZES[:-1]:
        acc += s
        pts.append(acc)
    return pts


def layer(x, c, conv_hist, lp, rel_bias, cache):
    B, T = x.shape[0], x.shape[1]
    mod = (jax.nn.silu(c) @ lp['w_ada'] + lp['b_ada'])[:, None, :]
    sh1, sc1, gt1, sh2, sc2, gt2 = jnp.split(mod, 6, axis=-1)
    h = rms_norm(x, lp['norm1']) * (1 + sc1) + sh1
    z = matmul(h.reshape(B * T, D_MODEL), lp['w_in']).reshape(B, T, IN_COLS)
    zq, zkc, zvc, zks, zvs, zkw, zvw, zg, zglu = jnp.split(z, split_points(), axis=-1)

    def kv(z):
        return z.reshape(B, T, N_KV, HEAD_DIM)

    q = rms_norm(zq.reshape(B, T, N_KV, GROUP, HEAD_DIM), lp['q_norm'])
    ks = rms_norm(kv(zks), lp['k_norm'][1])
    kw = rms_norm(kv(zkw), lp['k_norm'][2])
    gates = jax.nn.sigmoid(zg.reshape(B, T, N_KV, GROUP, 3))
    if cache is None:
        o_attn, attn_state = nsa_prompt(q, gates, kv(zkc), kv(zvc), ks, kv(zvs), kw, kv(zvw), lp, rel_bias)
    else:
        cache_cmp, cache_sel, cache_win, page_table = cache
        o_attn, attn_state = nsa_sample(q, gates, kv(zkc), kv(zvc), ks, kv(zvs), kw, kv(zvw),
                                        cache_cmp, cache_sel, cache_win, page_table, lp, rel_bias)
    o_conv, conv_state = conformer_conv(zglu, conv_hist, lp)
    cat = jnp.concatenate([rms_norm(o_attn, lp['attn_out_norm']), o_conv * lp['conv_out_scale']], axis=-1)
    mix = matmul(cat.reshape(B * T, D_MODEL), lp['w_out']).reshape(B, T, D_MODEL)
    x = x + gt1 * mix
    h2 = rms_norm(x, lp['norm2']) * (1 + sc2) + sh2
    x = x + gt2 * peer(h2, lp)
    return x, attn_state, conv_state


def kernel(x_prompt, x_sample, c_prompt, c_sample, cache_cmp_kv, cache_sel_kv, cache_win_kv, state_conv, page_table, rel_bias, w_ada, b_ada, norm1, w_in, q_norm, k_norm, cmp_pe, cmp_w1, cmp_b1, cmp_w2, cmp_b2, conv_w, conv_b, conv_ln_g, conv_ln_b, attn_out_norm, conv_out_scale, w_out, norm2, peer_q, peer_keys, expert_u, expert_v):
    lp = {
        'w_ada': w_ada[0], 'b_ada': b_ada[0], 'norm1': norm1[0], 'w_in': w_in[0],
        'q_norm': q_norm[0], 'k_norm': k_norm[0], 'cmp_pe': cmp_pe[0], 'cmp_w1': cmp_w1[0],
        'cmp_b1': cmp_b1[0], 'cmp_w2': cmp_w2[0], 'cmp_b2': cmp_b2[0], 'conv_w': conv_w[0],
        'conv_b': conv_b[0], 'conv_ln_g': conv_ln_g[0], 'conv_ln_b': conv_ln_b[0],
        'attn_out_norm': attn_out_norm[0], 'conv_out_scale': conv_out_scale[0], 'w_out': w_out[0],
        'norm2': norm2[0], 'peer_q': peer_q[0], 'peer_keys': peer_keys[0],
        'expert_u': expert_u[0], 'expert_v': expert_v[0],
    }
    hist0 = jnp.zeros((x_prompt.shape[0], CONV_K - 1, CONV_WIDTH), x_prompt.dtype)
    y_prompt, (a_c, a_s, a_w), cv = layer(x_prompt, c_prompt, hist0, lp, rel_bias, None)
    y_sample, (b_c, b_s, b_w), cvs = layer(x_sample, c_sample, state_conv[0], lp, rel_bias,
                                           (cache_cmp_kv[0], cache_sel_kv[0], cache_win_kv[0], page_table))
    return (y_prompt, y_sample, a_c[None], a_s[None], a_w[None], cv[None],
            b_c[None], b_s[None], b_w[None], cvs[None])
```

```python
import functools
import math

import jax
import jax.numpy as jnp
import numpy as np
from jax import lax
from jax.experimental import pallas as pl
from jax.experimental.pallas import tpu as pltpu

D_MODEL = 1024
HEAD_DIM = 64
ATTN_WIDTH = D_MODEL // 2
N_HEADS = ATTN_WIDTH // HEAD_DIM
N_KV = 2
GROUP = N_HEADS // N_KV
KV_WIDTH = N_KV * HEAD_DIM
CONV_WIDTH = D_MODEL - ATTN_WIDTH
CONV_K = 31
CMP_LEN = 32
CMP_STRIDE = 16
CMP_HIDDEN = 256
SEL_BLOCK = 64
TOP_N = 16
WINDOW = 512
N_BUCKETS = 32
MAX_DISTANCE = 128
Q_BLOCK = 64
PEER_HEADS = 8
PEER_DQ = 256
N_KEYS = 128
PEER_TOPK = 16
PEER_BLOCK = 256
FORCE_SCORE = 1.0e4
EPS = 1e-6
SPLIT_SIZES = (ATTN_WIDTH, KV_WIDTH, KV_WIDTH, KV_WIDTH, KV_WIDTH, KV_WIDTH, KV_WIDTH, 3 * N_HEADS, 2 * CONV_WIDTH)
IN_COLS = sum(SPLIT_SIZES)

LANES = 128
VMEM_LIMIT = 48 * 1024 * 1024


def _matmul_kernel(x_ref, w_ref, o_ref):
    o_ref[...] = jnp.dot(x_ref[...].astype(jnp.bfloat16), w_ref[...],
                         preferred_element_type=jnp.float32)


def matmul(x, w, tm=512):
    m, k = x.shape
    n = w.shape[1]
    n_pad = -(-n // LANES) * LANES
    wb = jnp.pad(w, ((0, 0), (0, n_pad - n))).astype(jnp.bfloat16)
    tm = min(tm, m)
    out = pl.pallas_call(
        _matmul_kernel,
        grid=(m // tm,),
        in_specs=[pl.BlockSpec((tm, k), lambda i: (i, 0)),
                  pl.BlockSpec((k, n_pad), lambda i: (0, 0))],
        out_specs=pl.BlockSpec((tm, n_pad), lambda i: (i, 0)),
        out_shape=jax.ShapeDtypeStruct((m, n_pad), jnp.float32),
        compiler_params=pltpu.CompilerParams(dimension_semantics=("arbitrary",),
                                             vmem_limit_bytes=VMEM_LIMIT),
        name="matmul",
    )(x, wb)
    return out[:, :n]


NEG = -1.0e30
SEL_MASK = 16384.0
FAR_KEYS = 512
NEAR_BLOCKS = 3
WIN_BLOCKS = WINDOW // SEL_BLOCK + 1


def _bucket_np(d):
    n = np.maximum(d, 0)
    exact = N_BUCKETS // 2
    nf = np.maximum(n, 1).astype(np.float32)
    large = exact + (np.log(nf / np.float32(exact)) / np.float32(math.log(MAX_DISTANCE / exact))
                     * np.float32(N_BUCKETS - exact)).astype(np.int32)
    return np.where(n < exact, n, np.minimum(large, N_BUCKETS - 1))


def _bias_table(rel_bias, dist, valid):
    tbl = rel_bias.astype(jnp.float32) - rel_bias[N_BUCKETS - 1].astype(jnp.float32)[None, :]
    b = jnp.transpose(tbl[_bucket_np(dist)], (0, 3, 1, 2))
    return jnp.where(valid[:, None], b, NEG)


def _group_sum(x, grp, t):
    acc = x[grp * GROUP * t:(grp * GROUP + 1) * t]
    for r in range(1, GROUP):
        acc = acc + x[(grp * GROUP + r) * t:(grp * GROUP + r + 1) * t]
    return acc


def _nsa_prompt_kernel(q_ref, g_ref, kc_ref, vc_ref, ks_ref, vs_ref, kw_ref, vw_ref,
                       bc_ref, bsn_ref, bw_ref, cover_ref, e_ref, o_ref, *, tq):
    f32, bf16 = jnp.float32, jnp.bfloat16
    i = pl.program_id(1)
    rows = N_HEADS * tq
    nt = (((1,), (1,)), ((), ()))

    q = q_ref[0]
    lane_half = lax.broadcasted_iota(jnp.int32, (tq, 2 * HEAD_DIM), 1) // HEAD_DIM
    parts = []
    for h in range(N_HEADS):
        blk = q[:, (h // 2) * 2 * HEAD_DIM:(h // 2 + 1) * 2 * HEAD_DIM]
        grp = h // GROUP
        if h % 2 != grp:
            blk = pltpu.roll(blk, HEAD_DIM, 1)
        parts.append(jnp.where(lane_half == grp, blk, 0.0))
    qpad = jnp.concatenate(parts, axis=0).astype(bf16)

    def heads(x):
        return jnp.concatenate([x[:tq]] * GROUP + [x[tq:]] * GROUP, axis=0)

    tbl_c = bc_ref[0].reshape(rows, bc_ref.shape[-1])
    s = lax.dot_general(qpad, kc_ref[0], nt, preferred_element_type=f32) + tbl_c
    ok = tbl_c > 0.5 * NEG
    m = jnp.max(s, axis=1, keepdims=True)
    e = jnp.where(ok, jnp.exp(s - m), 0.0)
    d = jnp.sum(e, axis=1, keepdims=True)
    pc = e / jnp.where(d > 0.0, d, 1.0)
    pc16 = pc.astype(bf16)
    o_cmp = jnp.dot(pc16, vc_ref[0], preferred_element_type=f32)

    imp_h = jnp.dot(pc16, cover_ref[...], preferred_element_type=f32)
    imp = jnp.concatenate([_group_sum(imp_h, grp, tq) for grp in range(N_KV)], axis=0)
    ns = imp.shape[1]
    j = lax.broadcasted_iota(jnp.int32, imp.shape, 1)
    forced = (j == 0) | (j == i) | (j == i - 1)
    imp = jnp.where(forced, imp + FORCE_SCORE, imp)
    imp = jnp.where(j > i, -FORCE_SCORE, imp)
    cnt = jnp.zeros(imp.shape, jnp.int32)
    for jp in range(ns):
        col = imp[:, jp:jp + 1]
        ahead = (col > imp) | ((col == imp) & (j > jp))
        cnt = cnt + ahead.astype(jnp.int32)
    sel = cnt < TOP_N

    nb0 = jnp.maximum(i - (NEAR_BLOCKS - 1), 0)
    near = NEAR_BLOCKS * SEL_BLOCK
    k0 = pl.multiple_of(nb0 * SEL_BLOCK, SEL_BLOCK)
    sel_nb0 = jnp.sum(jnp.where((j == nb0) & sel, 1.0, 0.0), axis=1, keepdims=True)
    lane = lax.broadcasted_iota(jnp.int32, (rows, near), 1)
    s = lax.dot_general(qpad, ks_ref[0, pl.ds(k0, near), :], nt, preferred_element_type=f32)
    s = s + bsn_ref[0].reshape(rows, near)
    s = s + jnp.where(lane < SEL_BLOCK, (heads(sel_nb0) - 1.0) * SEL_MASK, 0.0)
    m = jnp.max(s, axis=1, keepdims=True)
    p = jnp.exp(s - m)
    l = jnp.sum(p, axis=1, keepdims=True)
    acc = jnp.dot(p.astype(bf16), vs_ref[0, pl.ds(k0, near), :], preferred_element_type=f32)

    sel_far = heads(jnp.where(sel & (j < nb0), 0.0, -1.0)).astype(bf16)
    n_far = (nb0 * SEL_BLOCK + FAR_KEYS - 1) // FAR_KEYS

    def far_body(c, carry):
        m, l, acc = carry
        c0 = pl.multiple_of(c * FAR_KEYS, FAR_KEYS)
        s = lax.dot_general(qpad, ks_ref[0, pl.ds(c0, FAR_KEYS), :], nt, preferred_element_type=f32)
        s = s + jnp.dot(sel_far, e_ref[c], preferred_element_type=f32)
        m_new = jnp.maximum(m, jnp.max(s, axis=1, keepdims=True))
        a = jnp.exp(m - m_new)
        p = jnp.exp(s - m_new)
        l = a * l + jnp.sum(p, axis=1, keepdims=True)
        acc = a * acc + jnp.dot(p.astype(bf16), vs_ref[0, pl.ds(c0, FAR_KEYS), :],
                                preferred_element_type=f32)
        return m_new, l, acc

    m, l, acc = lax.fori_loop(0, n_far, far_body, (m, l, acc))
    o_sel = acc / l

    wk = WIN_BLOCKS * SEL_BLOCK
    w0 = pl.multiple_of(jnp.maximum(i - (WIN_BLOCKS - 1), 0) * SEL_BLOCK, SEL_BLOCK)
    s = lax.dot_general(qpad, kw_ref[0, pl.ds(w0, wk), :], nt, preferred_element_type=f32)
    s = s + bw_ref[0].reshape(rows, wk)
    m = jnp.max(s, axis=1, keepdims=True)
    p = jnp.exp(s - m)
    l = jnp.sum(p, axis=1, keepdims=True)
    o_win = jnp.dot(p.astype(bf16), vw_ref[0, pl.ds(w0, wk), :], preferred_element_type=f32) / l

    gates = g_ref[0]
    for h in range(N_HEADS):
        grp = h // GROUP
        r0, r1 = h * tq, (h + 1) * tq
        c0, c1 = grp * HEAD_DIM, (grp + 1) * HEAD_DIM
        o_h = (gates[:, 3 * h:3 * h + 1] * o_cmp[r0:r1, c0:c1]
               + gates[:, 3 * h + 1:3 * h + 2] * o_sel[r0:r1, c0:c1]
               + gates[:, 3 * h + 2:3 * h + 3] * o_win[r0:r1, c0:c1])
        o_ref[0, :, h * HEAD_DIM:(h + 1) * HEAD_DIM] = o_h


def nsa_prompt_attention(q, gates, kc, vc, ks, vs, kw, vw, rel_bias):
    B, T = q.shape[0], q.shape[1]
    tq = SEL_BLOCK
    nq = T // tq
    ns = T // SEL_BLOCK
    nc = kc.shape[1]
    ncp = -(-nc // LANES) * LANES
    bf16 = jnp.bfloat16
    scale = HEAD_DIM ** -0.5
    assert T % FAR_KEYS == 0 and T >= WIN_BLOCKS * SEL_BLOCK

    tl = np.arange(tq)[None, :, None]
    r = tq // CMP_STRIDE
    span = r * (nq - 1) + ncp
    y = np.arange(span)[None, None, :]
    dy = tl + CMP_STRIDE * (y - (ncp - 1)) - (CMP_LEN - 1)
    master = _bias_table(rel_bias, dy, dy >= 0)[0][:, :, ::-1]
    pad_col = jnp.arange(ncp) < nc
    bc = jnp.stack([jnp.where(pad_col, master[:, :, r * (nq - 1 - i):r * (nq - 1 - i) + ncp], NEG)
                    for i in range(nq)])
    vv = np.arange(NEAR_BLOCKS)[:, None, None]
    dn = vv * tq + tl - np.arange(NEAR_BLOCKS * SEL_BLOCK)[None, None, :]
    bsn = _bias_table(rel_bias, dn, dn >= 0)
    vw_ = np.arange(WIN_BLOCKS)[:, None, None]
    dw = vw_ * tq + tl - np.arange(WIN_BLOCKS * SEL_BLOCK)[None, None, :]
    bw = _bias_table(rel_bias, dw, (dw >= 0) & (dw < WINDOW))

    ci = np.arange(ncp)[:, None]
    sj = np.arange(ns)[None, :]
    cover = ((ci * CMP_STRIDE < (sj + 1) * SEL_BLOCK) & (ci * CMP_STRIDE + CMP_LEN > sj * SEL_BLOCK)
             & (ci < nc))
    cover = jnp.asarray(cover.astype(np.float32), bf16)
    pos = np.arange(T).reshape(T // FAR_KEYS, 1, FAR_KEYS)
    expand = jnp.asarray(np.where(pos // SEL_BLOCK == np.arange(ns)[None, :, None], SEL_MASK, 0.0)
                         .astype(np.float32), bf16)

    padc = ((0, 0), (0, ncp - nc), (0, 0))
    seq = lambda b, i: (b, 0, 0)
    return pl.pallas_call(
        functools.partial(_nsa_prompt_kernel, tq=tq),
        grid=(B, nq),
        in_specs=[
            pl.BlockSpec((1, tq, ATTN_WIDTH), lambda b, i: (b, i, 0)),
            pl.BlockSpec((1, tq, 3 * N_HEADS), lambda b, i: (b, i, 0)),
            pl.BlockSpec((1, ncp, KV_WIDTH), seq),
            pl.BlockSpec((1, ncp, KV_WIDTH), seq),
            pl.BlockSpec((1, T, KV_WIDTH), seq),
            pl.BlockSpec((1, T, KV_WIDTH), seq),
            pl.BlockSpec((1, T, KV_WIDTH), seq),
            pl.BlockSpec((1, T, KV_WIDTH), seq),
            pl.BlockSpec((1, N_HEADS, tq, ncp), lambda b, i: (i, 0, 0, 0)),
            pl.BlockSpec((1, N_HEADS, tq, NEAR_BLOCKS * SEL_BLOCK),
                         lambda b, i: (jnp.minimum(i, NEAR_BLOCKS - 1), 0, 0, 0)),
            pl.BlockSpec((1, N_HEADS, tq, WIN_BLOCKS * SEL_BLOCK),
                         lambda b, i: (jnp.minimum(i, WIN_BLOCKS - 1), 0, 0, 0)),
            pl.BlockSpec((ncp, ns), lambda b, i: (0, 0)),
            pl.BlockSpec((T // FAR_KEYS, ns, FAR_KEYS), lambda b, i: (0, 0, 0)),
        ],
        out_specs=pl.BlockSpec((1, tq, ATTN_WIDTH), lambda b, i: (b, i, 0)),
        out_shape=jax.ShapeDtypeStruct((B, T, ATTN_WIDTH), jnp.float32),
        compiler_params=pltpu.CompilerParams(dimension_semantics=("arbitrary", "arbitrary"),
                                             vmem_limit_bytes=VMEM_LIMIT),
        name="nsa_prompt",
    )(q * scale, gates, jnp.pad(kc, padc).astype(bf16), jnp.pad(vc, padc).astype(bf16),
      ks.astype(bf16), vs.astype(bf16), kw.astype(bf16), vw.astype(bf16), bc, bsn, bw, cover, expand)


NEW_PAD = 16
KV_SLOTS = 3


def _softmax_parts(s):
    m = jnp.max(s, axis=1, keepdims=True)
    p = jnp.exp(s - m)
    return m, p, jnp.sum(p, axis=1, keepdims=True)


def _nsa_sample_kernel(pt_ref, q_ref, g_ref, kc_ref, vc_ref, cache_ref, nsk_ref, nsv_ref, cw_ref, nwk_ref, nwv_ref,
                       bc_ref, bsl_ref, bnew_ref, bw_ref, cover_ref, e_ref, o_ref, kv_buf, sem, *, tn, ns):
    f32, bf16 = jnp.float32, jnp.bfloat16
    nt = (((1,), (1,)), ((), ()))
    b = pl.program_id(0)
    n_pages, page = pt_ref.shape[1], cache_ref.shape[-1]
    pages_per_chunk = FAR_KEYS // page
    n_far = n_pages // pages_per_chunk
    total = pl.num_programs(0) * n_far

    def fetch(gc):
        seq, first = gc // n_far, (gc % n_far) * pages_per_chunk
        slot = gc % KV_SLOTS
        for i in range(pages_per_chunk):
            pltpu.make_async_copy(cache_ref.at[pt_ref[seq, first + i]], kv_buf.at[slot, i], sem.at[slot]).start()

    @pl.when(b == 0)
    def _():
        for gc in range(KV_SLOTS - 1):
            fetch(gc)
    qpad = q_ref[0]
    cur = ns - 1

    def heads(x):
        return jnp.concatenate([x[:tn]] * GROUP + [x[tn:]] * GROUP, axis=0)

    s = lax.dot_general(qpad, kc_ref[0], nt, preferred_element_type=f32) + bc_ref[...]
    _, e, d = _softmax_parts(s)
    pc = e / d
    pc16 = pc.astype(bf16)
    o_cmp = jnp.dot(pc16, vc_ref[0], preferred_element_type=f32)

    imp_h = jnp.dot(pc16, cover_ref[...], preferred_element_type=f32)
    imp = jnp.concatenate([_group_sum(imp_h, grp, tn) for grp in range(N_KV)], axis=0)
    j = lax.broadcasted_iota(jnp.int32, imp.shape, 1)
    forced = (j == 0) | (j == cur) | (j == cur - 1)
    imp = jnp.where(forced, imp + FORCE_SCORE, imp)
    imp = jnp.where(j > cur, -jnp.inf, imp)
    cnt = jnp.zeros(imp.shape, jnp.int32)
    for jp in range(ns):
        col = imp[:, jp:jp + 1]
        ahead = (col > imp) | ((col == imp) & (j > jp))
        cnt = cnt + ahead.astype(jnp.int32)
    sel = cnt < TOP_N
    sel_far = heads(jnp.where(sel & (j < cur), 0.0, -1.0)).astype(bf16)

    bnew = bnew_ref[...]
    m, p, l = _softmax_parts(lax.dot_general(qpad, nsk_ref[0], nt, preferred_element_type=f32) + bnew)
    acc = jnp.dot(p.astype(bf16), nsv_ref[0], preferred_element_type=f32)
    half = GROUP * tn
    q_g = [qpad[grp * half:(grp + 1) * half, grp * HEAD_DIM:(grp + 1) * HEAD_DIM] for grp in range(N_KV)]
    zero = jnp.zeros((half, HEAD_DIM), f32)

    def far_chunk(c, carry, bias):
        m, l, acc = carry
        gc = b * n_far + c
        slot = gc % KV_SLOTS
        pltpu.make_async_copy(kv_buf.at[slot], kv_buf.at[slot], sem.at[slot]).wait()

        @pl.when(gc + KV_SLOTS - 1 < total)
        def _():
            fetch(gc + KV_SLOTS - 1)

        def plane(kv, grp):
            return jnp.concatenate([kv_buf[slot, i, kv, grp] for i in range(pages_per_chunk)], axis=1).astype(bf16)

        s = jnp.concatenate([jnp.dot(q_g[grp], plane(0, grp), preferred_element_type=f32)
                             for grp in range(N_KV)], axis=0)
        s = s + jnp.dot(sel_far, e_ref[c], preferred_element_type=f32)
        if bias is not None:
            s = s + bias
        m_new = jnp.maximum(m, jnp.max(s, axis=1, keepdims=True))
        a = jnp.exp(m - m_new)
        p = jnp.exp(s - m_new)
        l = a * l + jnp.sum(p, axis=1, keepdims=True)
        p = p.astype(bf16)
        pv = [lax.dot_general(p[grp * half:(grp + 1) * half], plane(1, grp), nt, preferred_element_type=f32)
              for grp in range(N_KV)]
        pv = jnp.concatenate([jnp.concatenate([pv[0], zero], axis=1),
                              jnp.concatenate([zero, pv[1]], axis=1)], axis=0)
        return m_new, l, a * acc + pv

    carry = lax.fori_loop(0, n_far - 1, lambda c, carry: far_chunk(c, carry, None), (m, l, acc))
    m, l, acc = far_chunk(n_far - 1, carry, bsl_ref[...])
    o_sel = acc / l

    cw = cw_ref[0]
    s_c = lax.dot_general(qpad, cw[:, :KV_WIDTH].astype(bf16), nt, preferred_element_type=f32) + bw_ref[...]
    s_n = lax.dot_general(qpad, nwk_ref[0], nt, preferred_element_type=f32) + bnew
    m = jnp.maximum(jnp.max(s_c, axis=1, keepdims=True), jnp.max(s_n, axis=1, keepdims=True))
    p_c = jnp.exp(s_c - m)
    p_n = jnp.exp(s_n - m)
    l = jnp.sum(p_c, axis=1, keepdims=True) + jnp.sum(p_n, axis=1, keepdims=True)
    o_win = (jnp.dot(p_c.astype(bf16), cw[:, KV_WIDTH:].astype(bf16), preferred_element_type=f32)
             + jnp.dot(p_n.astype(bf16), nwv_ref[0], preferred_element_type=f32)) / l

    g = g_ref[0]
    o_ref[0] = g[:, 0:1] * o_cmp + g[:, 1:2] * o_sel + g[:, 2:3] * o_win


def nsa_sample_attention(q, gates, kc, vc, cache_sel, page_table, new_sel, cache_win, new_win, rel_bias):
    B, tn = q.shape[0], q.shape[1]
    past, wk, nc = page_table.shape[1] * cache_sel.shape[-1], cache_win.shape[1], kc.shape[1]
    assert past % FAR_KEYS == 0 and past % SEL_BLOCK == 0 and tn <= NEW_PAD and wk == WINDOW
    ns = past // SEL_BLOCK + 1
    nsp = -(-ns // LANES) * LANES
    ncp = -(-nc // LANES) * LANES
    rows = N_HEADS * tn
    bf16 = jnp.bfloat16
    n_far = past // FAR_KEYS

    tl = np.arange(tn)[:, None]
    flat = lambda t: t[0].reshape(rows, t.shape[-1])
    cc = np.arange(ncp)[None, :]
    dc = past + tl - (cc * CMP_STRIDE + CMP_LEN - 1)
    bc = flat(_bias_table(rel_bias, dc[None], ((dc >= 0) & (cc < nc))[None]))
    dl = tl + FAR_KEYS - np.arange(FAR_KEYS)[None, :]
    bsl = flat(_bias_table(rel_bias, dl[None], (dl >= 0)[None]))
    jn = np.arange(NEW_PAD)[None, :]
    dn = tl - jn
    bnew = flat(_bias_table(rel_bias, dn[None], ((dn >= 0) & (jn < tn))[None]))
    dw = wk + tl - np.arange(wk)[None, :]
    bw = flat(_bias_table(rel_bias, dw[None], ((dw >= 0) & (dw < WINDOW))[None]))

    ci = np.arange(ncp)[:, None]
    sj = np.arange(nsp)[None, :]
    cover = ((ci * CMP_STRIDE < (sj + 1) * SEL_BLOCK) & (ci * CMP_STRIDE + CMP_LEN > sj * SEL_BLOCK)
             & (ci < nc) & (sj < ns))
    cover = jnp.asarray(cover.astype(np.float32), bf16)
    pos = np.arange(past).reshape(n_far, 1, FAR_KEYS)
    expand = jnp.asarray(np.where(pos // SEL_BLOCK == np.arange(nsp)[None, :, None], SEL_MASK, 0.0)
                         .astype(np.float32), bf16)

    qh = jnp.transpose((q * HEAD_DIM ** -0.5).reshape(B, tn, N_HEADS, HEAD_DIM), (0, 2, 1, 3))
    zero = jnp.zeros_like(qh)
    grp = (np.arange(N_HEADS) // GROUP)[None, :, None, None]
    qpad = jnp.where(grp == 0, jnp.concatenate([qh, zero], -1), jnp.concatenate([zero, qh], -1))
    qpad = qpad.reshape(B, rows, KV_WIDTH).astype(bf16)
    grow = jnp.transpose(gates.reshape(B, tn, N_HEADS, 3), (0, 2, 1, 3)).reshape(B, rows, 3)
    padn = lambda a: jnp.pad(a, ((0, 0), (0, NEW_PAD - tn), (0, 0))).astype(bf16)
    padc = ((0, 0), (0, ncp - nc), (0, 0))

    seq3 = lambda b, pt: (b, 0, 0)
    const2 = lambda b, pt: (0, 0)
    grid_spec = pltpu.PrefetchScalarGridSpec(
        num_scalar_prefetch=1,
        grid=(B,),
        in_specs=[
            pl.BlockSpec((1, rows, KV_WIDTH), seq3),
            pl.BlockSpec((1, rows, 3), seq3),
            pl.BlockSpec((1, ncp, KV_WIDTH), seq3),
            pl.BlockSpec((1, ncp, KV_WIDTH), seq3),
            pl.BlockSpec(memory_space=pl.ANY),
            pl.BlockSpec((1, NEW_PAD, KV_WIDTH), seq3),
            pl.BlockSpec((1, NEW_PAD, KV_WIDTH), seq3),
            pl.BlockSpec((1, wk, 2 * KV_WIDTH), seq3),
            pl.BlockSpec((1, NEW_PAD, KV_WIDTH), seq3),
            pl.BlockSpec((1, NEW_PAD, KV_WIDTH), seq3),
            pl.BlockSpec((rows, ncp), const2),
            pl.BlockSpec((rows, FAR_KEYS), const2),
            pl.BlockSpec((rows, NEW_PAD), const2),
            pl.BlockSpec((rows, wk), const2),
            pl.BlockSpec((ncp, nsp), const2),
            pl.BlockSpec((n_far, nsp, FAR_KEYS), lambda b, pt: (0, 0, 0)),
        ],
        out_specs=pl.BlockSpec((1, rows, KV_WIDTH), seq3),
        scratch_shapes=[pltpu.VMEM((KV_SLOTS, FAR_KEYS // cache_sel.shape[-1]) + cache_sel.shape[1:], jnp.float32),
                        pltpu.SemaphoreType.DMA((KV_SLOTS,))],
    )
    out = pl.pallas_call(
        functools.partial(_nsa_sample_kernel, tn=tn, ns=ns),
        grid_spec=grid_spec,
        out_shape=jax.ShapeDtypeStruct((B, rows, KV_WIDTH), jnp.float32),
        compiler_params=pltpu.CompilerParams(dimension_semantics=("arbitrary",),
                                             vmem_limit_bytes=VMEM_LIMIT),
        name="nsa_sample",
    )(page_table, qpad, grow, jnp.pad(kc, padc).astype(bf16), jnp.pad(vc, padc).astype(bf16), cache_sel,
      padn(new_sel[:, :, :KV_WIDTH]), padn(new_sel[:, :, KV_WIDTH:]), cache_win,
      padn(new_win[:, :, :KV_WIDTH]), padn(new_win[:, :, KV_WIDTH:]), bc, bsl, bnew, bw, cover, expand)
    o = out.reshape(B, N_HEADS, tn, 2, HEAD_DIM)
    o = jnp.where(grp[..., None] == 0, o[:, :, :, 0:1], o[:, :, :, 1:2])[:, :, :, 0]
    return jnp.transpose(o, (0, 2, 1, 3)).reshape(B, tn, ATTN_WIDTH)


N_SUB = 2 * PEER_HEADS
HALF_DQ = PEER_DQ // 2
N_PAIRS = PEER_HEADS * PEER_TOPK


def _top_rounds(s, row, n):
    width = s.shape[0]
    k = lax.broadcasted_iota(jnp.int32, (n, s.shape[1]), 0)
    vals = jnp.zeros((n, s.shape[1]), jnp.float32)
    ids = jnp.zeros((n, s.shape[1]), jnp.float32)
    for a in range(n):
        m = jnp.max(s, axis=0, keepdims=True)
        ix = jnp.min(jnp.where(s == m, row, float(width)), axis=0, keepdims=True)
        vals = jnp.where(k == a, m, vals)
        ids = jnp.where(k == a, ix, ids)
        s = jnp.where(row == ix, -jnp.inf, s)
    return vals, ids


def _peer_route_kernel(h_ref, wq_ref, keys_ref, idx_ref, g_ref, q_scr, e_scr, w_scr):
    f32, bf16 = jnp.float32, jnp.bfloat16
    tm = h_ref.shape[0]
    nt = (((1,), (1,)), ((), ()))
    q = jnp.dot(h_ref[...].astype(bf16), wq_ref[...], preferred_element_type=f32)
    for k in range(N_SUB):
        q_scr[k] = q[:, k * HALF_DQ:(k + 1) * HALF_DQ].astype(bf16)
    row = lax.broadcasted_iota(jnp.int32, (N_KEYS, tm), 0).astype(f32)
    n_cand = PEER_TOPK * PEER_TOPK
    row2 = lax.broadcasted_iota(jnp.int32, (n_cand, tm), 0).astype(f32)

    def head(h, carry):
        s1, i1 = _top_rounds(lax.dot_general(keys_ref[2 * h], q_scr[2 * h], nt, preferred_element_type=f32),
                             row, PEER_TOPK)
        s2, i2 = _top_rounds(lax.dot_general(keys_ref[2 * h + 1], q_scr[2 * h + 1], nt,
                                             preferred_element_type=f32), row, PEER_TOPK)
        cand = jnp.concatenate([s1[a:a + 1] + s2 for a in range(PEER_TOPK)], axis=0)
        cidx = jnp.concatenate([i1[a:a + 1] * float(N_KEYS) + i2 for a in range(PEER_TOPK)], axis=0)
        k = lax.broadcasted_iota(jnp.int32, (PEER_TOPK, tm), 0)
        top = jnp.zeros((PEER_TOPK, tm), f32)
        eid = jnp.zeros((PEER_TOPK, tm), f32)
        for a in range(PEER_TOPK):
            m = jnp.max(cand, axis=0, keepdims=True)
            pos = jnp.min(jnp.where(cand == m, row2, float(n_cand)), axis=0, keepdims=True)
            hit = row2 == pos
            e = jnp.max(jnp.where(hit, cidx, -1.0), axis=0, keepdims=True)
            top = jnp.where(k == a, m, top)
            eid = jnp.where(k == a, e, eid)
            cand = jnp.where(hit, -jnp.inf, cand)
        ex = jnp.exp(top - top[0:1])
        r0 = pl.multiple_of(h * PEER_TOPK, PEER_TOPK)
        e_scr[pl.ds(r0, PEER_TOPK), :] = eid
        w_scr[pl.ds(r0, PEER_TOPK), :] = ex / jnp.sum(ex, axis=0, keepdims=True)
        return carry

    lax.fori_loop(0, PEER_HEADS, head, 0)
    idx_ref[...] = e_scr[...].T.astype(jnp.int32)
    g_ref[...] = w_scr[...].T


def peer_route(h2, peer_q, peer_keys, tm=128):
    n = h2.shape[0]
    tm = min(tm, n)
    bf16 = jnp.bfloat16
    keys = peer_keys.reshape(N_SUB, N_KEYS, HALF_DQ).astype(bf16)
    return pl.pallas_call(
        _peer_route_kernel,
        grid=(n // tm,),
        in_specs=[pl.BlockSpec((tm, D_MODEL), lambda i: (i, 0)),
                  pl.BlockSpec((D_MODEL, PEER_HEADS * PEER_DQ), lambda i: (0, 0)),
                  pl.BlockSpec((N_SUB, N_KEYS, HALF_DQ), lambda i: (0, 0, 0))],
        out_specs=[pl.BlockSpec((tm, N_PAIRS), lambda i: (i, 0)),
                   pl.BlockSpec((tm, N_PAIRS), lambda i: (i, 0))],
        out_shape=[jax.ShapeDtypeStruct((n, N_PAIRS), jnp.int32),
                   jax.ShapeDtypeStruct((n, N_PAIRS), jnp.float32)],
        scratch_shapes=[pltpu.VMEM((N_SUB, tm, HALF_DQ), bf16),
                        pltpu.VMEM((N_PAIRS, tm), jnp.float32),
                        pltpu.VMEM((N_PAIRS, tm), jnp.float32)],
        compiler_params=pltpu.CompilerParams(dimension_semantics=("arbitrary",),
                                             vmem_limit_bytes=VMEM_LIMIT),
        name="peer_route",
    )(h2, peer_q.astype(bf16), keys)


def _gelu_tanh(x):
    return 0.5 * x * (1.0 + jnp.tanh(math.sqrt(2.0 / math.pi) * (x + 0.044715 * (x * x * x))))


N_SLOTS = 8
ROW_TILES = 2 * D_MODEL // LANES
U_TILES = D_MODEL // LANES


def _peer_expert_kernel(idx_ref, nxt_ref, x_ref, g_ref, uv_ref, y_ref, *scratch):
    tb = x_ref.shape[0]
    i = pl.program_id(0)
    last = pl.num_programs(0) - 1
    ahead = N_SLOTS - 1
    bufs, sem = scratch[:N_SLOTS], scratch[N_SLOTS]

    def issue(ids, t, slot):
        for k in range(N_PAIRS):
            pltpu.make_async_copy(uv_ref.at[ids[t, k]], bufs[slot].at[k // 8, :, k % 8, :], sem.at[slot]).start()

    def wait_slot(slot):
        pltpu.make_async_copy(bufs[slot], bufs[slot], sem.at[slot]).wait()

    eye = (lax.broadcasted_iota(jnp.int32, (N_PAIRS, N_PAIRS), 0)
           == lax.broadcasted_iota(jnp.int32, (N_PAIRS, N_PAIRS), 1))

    def compute(t, slot):
        buf = bufs[slot]

        def chunk(c):
            return buf[:, c].reshape(N_PAIRS, LANES)

        x = x_ref[t]
        acc = chunk(0) * x[0:1, :]
        for s in range(1, U_TILES):
            acc = acc + chunk(s) * x[s:s + 1, :]
        a = jnp.sum(acc, axis=1, keepdims=True)
        g_col = jnp.sum(jnp.where(eye, g_ref[pl.ds(t, 1), :], 0.0), axis=1, keepdims=True)
        w = g_col * _gelu_tanh(a)
        y = [jnp.sum(w * chunk(U_TILES + s), axis=0, keepdims=True) for s in range(U_TILES)]
        y_ref[pl.ds(t, 1), :] = jnp.concatenate(y, axis=1)

    @pl.when(i == 0)
    def _():
        for t in range(ahead):
            issue(idx_ref, t, t)

    def group(j, c):
        for u in range(N_SLOTS):
            t = j * N_SLOTS + u
            wait_slot(u)
            issue(idx_ref, t + ahead, (u + ahead) % N_SLOTS)
            compute(t, u)
        return c

    n_groups = tb // N_SLOTS
    lax.fori_loop(0, n_groups - 1, group, 0)
    for u in range(N_SLOTS):
        t = tb - N_SLOTS + u
        wait_slot(u)
        if u == 0:
            issue(idx_ref, t + ahead, ahead)
        else:
            issue(nxt_ref, u - 1, u - 1)
        compute(t, u)

    @pl.when(i == last)
    def _():
        for u in range(ahead):
            wait_slot(u)


def peer_experts(h2, eidx, gate, expert_u, expert_v, tb=64):
    n = h2.shape[0]
    tb = min(tb, n)
    nb = n // tb
    e = expert_u.shape[0]
    uv = jnp.concatenate([expert_u.reshape(e, U_TILES, LANES), expert_v.reshape(e, U_TILES, LANES)], axis=1)
    return pl.pallas_call(
        _peer_expert_kernel,
        grid=(nb,),
        in_specs=[pl.BlockSpec((tb, N_PAIRS), lambda i: (i, 0), memory_space=pltpu.SMEM),
                  pl.BlockSpec((tb, N_PAIRS), lambda i: (jnp.minimum(i + 1, nb - 1), 0), memory_space=pltpu.SMEM),
                  pl.BlockSpec((tb, U_TILES, LANES), lambda i: (i, 0, 0)),
                  pl.BlockSpec((tb, N_PAIRS), lambda i: (i, 0)),
                  pl.BlockSpec(memory_space=pl.ANY)],
        out_specs=pl.BlockSpec((tb, D_MODEL), lambda i: (i, 0)),
        out_shape=jax.ShapeDtypeStruct((n, D_MODEL), jnp.float32),
        scratch_shapes=[pltpu.VMEM((N_PAIRS // 8, ROW_TILES, 8, LANES), jnp.float32)] * N_SLOTS
        + [pltpu.SemaphoreType.DMA((N_SLOTS,))],
        compiler_params=pltpu.CompilerParams(dimension_semantics=("arbitrary",),
                                             vmem_limit_bytes=VMEM_LIMIT),
        name="peer_experts",
    )(eidx, eidx, h2.reshape(n, U_TILES, LANES), gate, uv)


def rms_norm(x, g):
    y = x * lax.rsqrt(jnp.mean(x * x, axis=-1, keepdims=True) + EPS)
    return y * g


def layer_norm(x, g, b):
    mu = jnp.mean(x, axis=-1, keepdims=True)
    var = jnp.mean(jnp.square(x - mu), axis=-1, keepdims=True)
    return (x - mu) * lax.rsqrt(var + EPS) * g + b


def masked_softmax(logits, mask, axis):
    neg = jnp.where(mask, logits, -jnp.inf)
    m = jnp.max(neg, axis=axis, keepdims=True)
    m = jnp.where(jnp.isfinite(m), m, 0.0)
    e = jnp.where(mask, jnp.exp(neg - m), 0.0)
    d = jnp.sum(e, axis=axis, keepdims=True)
    return e / jnp.where(d > 0.0, d, 1.0)


def t5_bucket(dist):
    n = jnp.maximum(dist, 0)
    exact = N_BUCKETS // 2
    nf = jnp.maximum(n, 1).astype(jnp.float32)
    large = exact + (jnp.log(nf / exact) / math.log(MAX_DISTANCE / exact) * (N_BUCKETS - exact)).astype(jnp.int32)
    return jnp.where(n < exact, n, jnp.minimum(large, N_BUCKETS - 1))


def compress(rows, pe, w1, b1, w2, b2):
    B, L = rows.shape[0], rows.shape[1]
    ratio = CMP_LEN // CMP_STRIDE
    n_chunks = L // CMP_STRIDE
    nc = n_chunks - ratio + 1
    r = rows[:, :n_chunks * CMP_STRIDE].reshape(B, n_chunks, CMP_STRIDE, N_KV, HEAD_DIM)
    w1c = w1.reshape(ratio, CMP_STRIDE, HEAD_DIM, CMP_HIDDEN)
    parts = jnp.einsum('bcsgd,jsdh->jbcgh', r, w1c)
    hid = b1 + pe.reshape(-1) @ w1
    for j in range(ratio):
        hid = hid + parts[j, :, j:j + nc]
    out = jax.nn.gelu(hid) @ w2 + b2
    end = jnp.arange(nc, dtype=jnp.int32) * CMP_STRIDE + (CMP_LEN - 1)
    return out, end


def compress_branch(rows, lp, which):
    return compress(rows, lp['cmp_pe'][which], lp['cmp_w1'][which], lp['cmp_b1'][which],
                    lp['cmp_w2'][which], lp['cmp_b2'][which])


def to_blocks(rows):
    B, L = rows.shape[0], rows.shape[1]
    ns = -(-L // SEL_BLOCK)
    rows = jnp.pad(rows, ((0, 0), (0, ns * SEL_BLOCK - L), (0, 0), (0, 0)))
    return jnp.transpose(rows.reshape(B, ns, SEL_BLOCK, N_KV, HEAD_DIM), (0, 3, 1, 2, 4))


def nsa_block(q, t_pos, gates, kc, vc, kc_end, ks_blk, vs_blk, kw, vw, w_pos, rel_bias):
    B = q.shape[0]
    scale = HEAD_DIM ** -0.5
    tbl = rel_bias.astype(jnp.float32).reshape(N_BUCKETS, N_KV, GROUP)
    lc = jnp.einsum('bqgrd,bcgd->bqgrc', q, kc).astype(jnp.float32) * scale
    dc = t_pos[:, None] - kc_end[None, :]
    lc = lc + jnp.transpose(tbl[t5_bucket(dc)], (0, 2, 3, 1))[None]
    pc = masked_softmax(lc, (dc >= 0)[None, :, None, None, :], axis=-1)
    o_cmp = jnp.einsum('bqgrc,bcgd->bqgrd', pc.astype(vc.dtype), vc)
    nc, ns = kc.shape[1], ks_blk.shape[2]
    ci = jnp.arange(nc)[:, None]
    sj = jnp.arange(ns)[None, :]
    cover = ((ci * CMP_STRIDE < (sj + 1) * SEL_BLOCK) & (ci * CMP_STRIDE + CMP_LEN > sj * SEL_BLOCK)).astype(jnp.float32)
    imp = jnp.einsum('bqgrc,cj->bqgj', pc, cover)
    cur = (t_pos // SEL_BLOCK)[:, None]
    forced = (sj == 0) | (sj == cur) | (sj == cur - 1)
    valid = sj * SEL_BLOCK <= t_pos[:, None]
    imp = jnp.where(forced[None, :, None, :], imp + FORCE_SCORE, imp)
    imp = jnp.where(valid[None, :, None, :], imp, -FORCE_SCORE)
    _, idx = lax.top_k(imp, min(TOP_N, ns))
    bi = jnp.arange(B)[:, None, None, None]
    gi = jnp.arange(N_KV)[None, None, :, None]
    ks_g = ks_blk[bi, gi, idx]
    vs_g = vs_blk[bi, gi, idx]
    ls = jnp.einsum('bqgrd,bqgnkd->bqgrnk', q, ks_g).astype(jnp.float32) * scale
    pos = idx[..., None] * SEL_BLOCK + jnp.arange(SEL_BLOCK, dtype=jnp.int32)
    ds = t_pos[None, :, None, None, None] - pos
    tblk = jnp.transpose(tbl, (1, 0, 2))
    bs = tblk[jnp.arange(N_KV)[None, None, :, None, None], t5_bucket(ds)]
    ls = ls + jnp.moveaxis(bs, -1, 3)
    ps = masked_softmax(ls, (ds >= 0)[:, :, :, None], axis=(-2, -1))
    o_sel = jnp.einsum('bqgrnk,bqgnkd->bqgrd', ps.astype(vs_g.dtype), vs_g)
    lw = jnp.einsum('bqgrd,bwgd->bqgrw', q, kw).astype(jnp.float32) * scale
    dw = t_pos[:, None] - w_pos[None, :]
    lw = lw + jnp.transpose(tbl[t5_bucket(dw)], (0, 2, 3, 1))[None]
    mw = (dw >= 0) & (dw < WINDOW) & (w_pos >= 0)[None, :]
    pw = masked_softmax(lw, mw[None, :, None, None, :], axis=-1)
    o_win = jnp.einsum('bqgrw,bwgd->bqgrd', pw.astype(vw.dtype), vw)
    return gates[..., 0:1] * o_cmp + gates[..., 1:2] * o_sel + gates[..., 2:3] * o_win


def nsa_prompt(q, gates, kc_raw, vc_raw, ks, vs, kw, vw, lp, rel_bias):
    B, T = q.shape[0], q.shape[1]
    kc, kc_end = compress_branch(kc_raw, lp, 0)
    kc = rms_norm(kc, lp['k_norm'][0])
    vc, _ = compress_branch(vc_raw, lp, 1)
    flat = lambda a: a.reshape(B, a.shape[1], -1)
    o = nsa_prompt_attention(flat(q), flat(gates), flat(kc), flat(vc), flat(ks), flat(vs), flat(kw), flat(vw),
                             rel_bias)
    new_win = jnp.stack([kw, vw], axis=2)[:, T - min(WINDOW, T):]
    return o, (jnp.stack([kc_raw, vc_raw], axis=2), jnp.stack([ks, vs], axis=2), new_win)


def nsa_sample(q, gates, kc_raw, vc_raw, ks, vs, kw, vw, cache_cmp, cache_sel, cache_win, page_table, lp, rel_bias):
    Bd, Tn = q.shape[0], q.shape[1]
    past = page_table.shape[1] * cache_cmp.shape[1]
    t_pos = past + jnp.arange(Tn, dtype=jnp.int32)
    past_cmp = cache_cmp[page_table].reshape(Bd, past, 2, N_KV, HEAD_DIM)
    kc, kc_end = compress_branch(jnp.concatenate([past_cmp[:, :, 0], kc_raw], axis=1), lp, 0)
    kc = rms_norm(kc, lp['k_norm'][0])
    vc, _ = compress_branch(jnp.concatenate([past_cmp[:, :, 1], vc_raw], axis=1), lp, 1)
    wk = cache_win.shape[1]
    flat = lambda a: a.reshape(Bd, a.shape[1], -1)
    new_sel = jnp.concatenate([flat(ks), flat(vs)], axis=-1)
    new_wkv = jnp.concatenate([flat(kw), flat(vw)], axis=-1)
    o = nsa_sample_attention(flat(q), flat(gates), flat(kc), flat(vc), jnp.transpose(cache_sel, (0, 2, 3, 4, 1)),
                             page_table, new_sel, flat(cache_win), new_wkv, rel_bias)
    win = jnp.concatenate([cache_win, jnp.stack([kw, vw], axis=2)], axis=1)
    new_win = win[:, wk + Tn - min(WINDOW, wk + Tn):]
    return o.reshape(Bd, Tn, ATTN_WIDTH), (jnp.stack([kc_raw, vc_raw], axis=2), jnp.stack([ks, vs], axis=2), new_win)


def conformer_conv(glu_in, hist, lp):
    u = glu_in[..., :CONV_WIDTH] * jax.nn.sigmoid(glu_in[..., CONV_WIDTH:])
    uh = jnp.concatenate([hist.astype(u.dtype), u], axis=1)
    y = lax.conv_general_dilated(uh, lp['conv_w'][:, None, :], window_strides=(1,), padding='VALID',
                                 dimension_numbers=('NWC', 'WIO', 'NWC'), feature_group_count=CONV_WIDTH)
    y = jax.nn.silu(layer_norm(y + lp['conv_b'], lp['conv_ln_g'], lp['conv_ln_b']))
    return y, uh[:, uh.shape[1] - (CONV_K - 1):]


def peer(h, lp):
    shp = h.shape
    xt = h.reshape(-1, D_MODEL)
    n = xt.shape[0]
    eidx, gate = peer_route(xt, lp['peer_q'], lp['peer_keys'])
    y = peer_experts(xt, eidx, gate, lp['expert_u'], lp['expert_v'])
    return y.reshape(shp)


def split_points():
    pts, acc = [], 0
    for s in SPLIT_SIZES[:-1]:
        acc += s
        pts.append(acc)
    return pts


def layer(x, c, conv_hist, lp, rel_bias, cache):
    B, T = x.shape[0], x.shape[1]
    mod = (jax.nn.silu(c) @ lp['w_ada'] + lp['b_ada'])[:, None, :]
    sh1, sc1, gt1, sh2, sc2, gt2 = jnp.split(mod, 6, axis=-1)
    h = rms_norm(x, lp['norm1']) * (1 + sc1) + sh1
    z = matmul(h.reshape(B * T, D_MODEL), lp['w_in']).reshape(B, T, IN_COLS)
    zq, zkc, zvc, zks, zvs, zkw, zvw, zg, zglu = jnp.split(z, split_points(), axis=-1)

    def kv(z):
        return z.reshape(B, T, N_KV, HEAD_DIM)

    q = rms_norm(zq.reshape(B, T, N_KV, GROUP, HEAD_DIM), lp['q_norm'])
    ks = rms_norm(kv(zks), lp['k_norm'][1])
    kw = rms_norm(kv(zkw), lp['k_norm'][2])
    gates = jax.nn.sigmoid(zg.reshape(B, T, N_KV, GROUP, 3))
    if cache is None:
        o_attn, attn_state = nsa_prompt(q, gates, kv(zkc), kv(zvc), ks, kv(zvs), kw, kv(zvw), lp, rel_bias)
    else:
        cache_cmp, cache_sel, cache_win, page_table = cache
        o_attn, attn_state = nsa_sample(q, gates, kv(zkc), kv(zvc), ks, kv(zvs), kw, kv(zvw),
                                        cache_cmp, cache_sel, cache_win, page_table, lp, rel_bias)
    o_conv, conv_state = conformer_conv(zglu, conv_hist, lp)
    cat = jnp.concatenate([rms_norm(o_attn, lp['attn_out_norm']), o_conv * lp['conv_out_scale']], axis=-1)
    mix = matmul(cat.reshape(B * T, D_MODEL), lp['w_out']).reshape(B, T, D_MODEL)
    x = x + gt1 * mix
    h2 = rms_norm(x, lp['norm2']) * (1 + sc2) + sh2
    x = x + gt2 * peer(h2, lp)
    return x, attn_state, conv_state


def kernel(x_prompt, x_sample, c_prompt, c_sample, cache_cmp_kv, cache_sel_kv, cache_win_kv, state_conv, page_table, rel_bias, w_ada, b_ada, norm1, w_in, q_norm, k_norm, cmp_pe, cmp_w1, cmp_b1, cmp_w2, cmp_b2, conv_w, conv_b, conv_ln_g, conv_ln_b, attn_out_norm, conv_out_scale, w_out, norm2, peer_q, peer_keys, expert_u, expert_v):
    lp = {
        'w_ada': w_ada[0], 'b_ada': b_ada[0], 'norm1': norm1[0], 'w_in': w_in[0],
        'q_norm': q_norm[0], 'k_norm': k_norm[0], 'cmp_pe': cmp_pe[0], 'cmp_w1': cmp_w1[0],
        'cmp_b1': cmp_b1[0], 'cmp_w2': cmp_w2[0], 'cmp_b2': cmp_b2[0], 'conv_w': conv_w[0],
        'conv_b': conv_b[0], 'conv_ln_g': conv_ln_g[0], 'conv_ln_b': conv_ln_b[0],
        'attn_out_norm': attn_out_norm[0], 'conv_out_scale': conv_out_scale[0], 'w_out': w_out[0],
        'norm2': norm2[0], 'peer_q': peer_q[0], 'peer_keys': peer_keys[0],
        'expert_u': expert_u[0], 'expert_v': expert_v[0],
    }
    hist0 = jnp.zeros((x_prompt.shape[0], CONV_K - 1, CONV_WIDTH), x_prompt.dtype)
    y_prompt, (a_c, a_s, a_w), cv = layer(x_prompt, c_prompt, hist0, lp, rel_bias, None)
    y_sample, (b_c, b_s, b_w), cvs = layer(x_sample, c_sample, state_conv[0], lp, rel_bias,
                                           (cache_cmp_kv[0], cache_sel_kv[0], cache_win_kv[0], page_table))
    return (y_prompt, y_sample, a_c[None], a_s[None], a_w[None], cv[None],
            b_c[None], b_s[None], b_w[None], cvs[None])
```

```python
import functools
import math

import jax
import jax.numpy as jnp
import numpy as np
from jax import lax
from jax.experimental import pallas as pl
from jax.experimental.pallas import tpu as pltpu

D_MODEL = 1024
HEAD_DIM = 64
ATTN_WIDTH = D_MODEL // 2
N_HEADS = ATTN_WIDTH // HEAD_DIM
N_KV = 2
GROUP = N_HEADS // N_KV
KV_WIDTH = N_KV * HEAD_DIM
CONV_WIDTH = D_MODEL - ATTN_WIDTH
CONV_K = 31
CMP_LEN = 32
CMP_STRIDE = 16
CMP_HIDDEN = 256
SEL_BLOCK = 64
TOP_N = 16
WINDOW = 512
N_BUCKETS = 32
MAX_DISTANCE = 128
Q_BLOCK = 64
PEER_HEADS = 8
PEER_DQ = 256
N_KEYS = 128
PEER_TOPK = 16
PEER_BLOCK = 256
FORCE_SCORE = 1.0e4
EPS = 1e-6
SPLIT_SIZES = (ATTN_WIDTH, KV_WIDTH, KV_WIDTH, KV_WIDTH, KV_WIDTH, KV_WIDTH, KV_WIDTH, 3 * N_HEADS, 2 * CONV_WIDTH)
IN_COLS = sum(SPLIT_SIZES)

LANES = 128
VMEM_LIMIT = 48 * 1024 * 1024


def _matmul_kernel(x_ref, w_ref, o_ref):
    o_ref[...] = jnp.dot(x_ref[...].astype(jnp.bfloat16), w_ref[...],
                         preferred_element_type=jnp.float32)


def matmul(x, w, tm=512):
    m, k = x.shape
    n = w.shape[1]
    n_pad = -(-n // LANES) * LANES
    wb = jnp.pad(w, ((0, 0), (0, n_pad - n))).astype(jnp.bfloat16)
    tm = min(tm, m)
    out = pl.pallas_call(
        _matmul_kernel,
        grid=(m // tm,),
        in_specs=[pl.BlockSpec((tm, k), lambda i: (i, 0)),
                  pl.BlockSpec((k, n_pad), lambda i: (0, 0))],
        out_specs=pl.BlockSpec((tm, n_pad), lambda i: (i, 0)),
        out_shape=jax.ShapeDtypeStruct((m, n_pad), jnp.float32),
        compiler_params=pltpu.CompilerParams(dimension_semantics=("arbitrary",),
                                             vmem_limit_bytes=VMEM_LIMIT),
        name="matmul",
    )(x, wb)
    return out[:, :n]


NEG = -1.0e30
SEL_MASK = 16384.0
FAR_KEYS = 512
NEAR_BLOCKS = 3
WIN_BLOCKS = WINDOW // SEL_BLOCK + 1


def _bucket_np(d):
    n = np.maximum(d, 0)
    exact = N_BUCKETS // 2
    nf = np.maximum(n, 1).astype(np.float32)
    large = exact + (np.log(nf / np.float32(exact)) / np.float32(math.log(MAX_DISTANCE / exact))
                     * np.float32(N_BUCKETS - exact)).astype(np.int32)
    return np.where(n < exact, n, np.minimum(large, N_BUCKETS - 1))


def _bias_table(rel_bias, dist, valid):
    tbl = rel_bias.astype(jnp.float32) - rel_bias[N_BUCKETS - 1].astype(jnp.float32)[None, :]
    b = jnp.transpose(tbl[_bucket_np(dist)], (0, 3, 1, 2))
    return jnp.where(valid[:, None], b, NEG)


def _group_sum(x, grp, t):
    acc = x[grp * GROUP * t:(grp * GROUP + 1) * t]
    for r in range(1, GROUP):
        acc = acc + x[(grp * GROUP + r) * t:(grp * GROUP + r + 1) * t]
    return acc


def _nsa_prompt_kernel(q_ref, g_ref, kc_ref, vc_ref, ks_ref, vs_ref, kw_ref, vw_ref,
                       bc_ref, bsn_ref, bw_ref, cover_ref, e_ref, o_ref, *, tq):
    f32, bf16 = jnp.float32, jnp.bfloat16
    i = pl.program_id(1)
    rows = N_HEADS * tq
    nt = (((1,), (1,)), ((), ()))

    q = q_ref[0]
    lane_half = lax.broadcasted_iota(jnp.int32, (tq, 2 * HEAD_DIM), 1) // HEAD_DIM
    parts = []
    for h in range(N_HEADS):
        blk = q[:, (h // 2) * 2 * HEAD_DIM:(h // 2 + 1) * 2 * HEAD_DIM]
        grp = h // GROUP
        if h % 2 != grp:
            blk = pltpu.roll(blk, HEAD_DIM, 1)
        parts.append(jnp.where(lane_half == grp, blk, 0.0))
    qpad = jnp.concatenate(parts, axis=0).astype(bf16)

    def heads(x):
        return jnp.concatenate([x[:tq]] * GROUP + [x[tq:]] * GROUP, axis=0)

    tbl_c = bc_ref[0].reshape(rows, bc_ref.shape[-1])
    s = lax.dot_general(qpad, kc_ref[0], nt, preferred_element_type=f32) + tbl_c
    ok = tbl_c > 0.5 * NEG
    m = jnp.max(s, axis=1, keepdims=True)
    e = jnp.where(ok, jnp.exp(s - m), 0.0)
    d = jnp.sum(e, axis=1, keepdims=True)
    pc = e / jnp.where(d > 0.0, d, 1.0)
    pc16 = pc.astype(bf16)
    o_cmp = jnp.dot(pc16, vc_ref[0], preferred_element_type=f32)

    imp_h = jnp.dot(pc16, cover_ref[...], preferred_element_type=f32)
    imp = jnp.concatenate([_group_sum(imp_h, grp, tq) for grp in range(N_KV)], axis=0)
    ns = imp.shape[1]
    j = lax.broadcasted_iota(jnp.int32, imp.shape, 1)
    forced = (j == 0) | (j == i) | (j == i - 1)
    imp = jnp.where(forced, imp + FORCE_SCORE, imp)
    imp = jnp.where(j > i, -FORCE_SCORE, imp)
    cnt = jnp.zeros(imp.shape, jnp.int32)
    for jp in range(ns):
        col = imp[:, jp:jp + 1]
        ahead = (col > imp) | ((col == imp) & (j > jp))
        cnt = cnt + ahead.astype(jnp.int32)
    sel = cnt < TOP_N

    nb0 = jnp.maximum(i - (NEAR_BLOCKS - 1), 0)
    near = NEAR_BLOCKS * SEL_BLOCK
    k0 = pl.multiple_of(nb0 * SEL_BLOCK, SEL_BLOCK)
    sel_nb0 = jnp.sum(jnp.where((j == nb0) & sel, 1.0, 0.0), axis=1, keepdims=True)
    lane = lax.broadcasted_iota(jnp.int32, (rows, near), 1)
    s = lax.dot_general(qpad, ks_ref[0, pl.ds(k0, near), :], nt, preferred_element_type=f32)
    s = s + bsn_ref[0].reshape(rows, near)
    s = s + jnp.where(lane < SEL_BLOCK, (heads(sel_nb0) - 1.0) * SEL_MASK, 0.0)
    m = jnp.max(s, axis=1, keepdims=True)
    p = jnp.exp(s - m)
    l = jnp.sum(p, axis=1, keepdims=True)
    acc = jnp.dot(p.astype(bf16), vs_ref[0, pl.ds(k0, near), :], preferred_element_type=f32)

    sel_far = heads(jnp.where(sel & (j < nb0), 0.0, -1.0)).astype(bf16)
    n_far = (nb0 * SEL_BLOCK + FAR_KEYS - 1) // FAR_KEYS

    def far_body(c, carry):
        m, l, acc = carry
        c0 = pl.multiple_of(c * FAR_KEYS, FAR_KEYS)
        s = lax.dot_general(qpad, ks_ref[0, pl.ds(c0, FAR_KEYS), :], nt, preferred_element_type=f32)
        s = s + jnp.dot(sel_far, e_ref[c], preferred_element_type=f32)
        m_new = jnp.maximum(m, jnp.max(s, axis=1, keepdims=True))
        a = jnp.exp(m - m_new)
        p = jnp.exp(s - m_new)
        l = a * l + jnp.sum(p, axis=1, keepdims=True)
        acc = a * acc + jnp.dot(p.astype(bf16), vs_ref[0, pl.ds(c0, FAR_KEYS), :],
                                preferred_element_type=f32)
        return m_new, l, acc

    m, l, acc = lax.fori_loop(0, n_far, far_body, (m, l, acc))
    o_sel = acc / l

    wk = WIN_BLOCKS * SEL_BLOCK
    w0 = pl.multiple_of(jnp.maximum(i - (WIN_BLOCKS - 1), 0) * SEL_BLOCK, SEL_BLOCK)
    s = lax.dot_general(qpad, kw_ref[0, pl.ds(w0, wk), :], nt, preferred_element_type=f32)
    s = s + bw_ref[0].reshape(rows, wk)
    m = jnp.max(s, axis=1, keepdims=True)
    p = jnp.exp(s - m)
    l = jnp.sum(p, axis=1, keepdims=True)
    o_win = jnp.dot(p.astype(bf16), vw_ref[0, pl.ds(w0, wk), :], preferred_element_type=f32) / l

    gates = g_ref[0]
    for h in range(N_HEADS):
        grp = h // GROUP
        r0, r1 = h * tq, (h + 1) * tq
        c0, c1 = grp * HEAD_DIM, (grp + 1) * HEAD_DIM
        o_h = (gates[:, 3 * h:3 * h + 1] * o_cmp[r0:r1, c0:c1]
               + gates[:, 3 * h + 1:3 * h + 2] * o_sel[r0:r1, c0:c1]
               + gates[:, 3 * h + 2:3 * h + 3] * o_win[r0:r1, c0:c1])
        o_ref[0, :, h * HEAD_DIM:(h + 1) * HEAD_DIM] = o_h


def nsa_prompt_attention(q, gates, kc, vc, ks, vs, kw, vw, rel_bias):
    B, T = q.shape[0], q.shape[1]
    tq = SEL_BLOCK
    nq = T // tq
    ns = T // SEL_BLOCK
    nc = kc.shape[1]
    ncp = -(-nc // LANES) * LANES
    bf16 = jnp.bfloat16
    scale = HEAD_DIM ** -0.5
    assert T % FAR_KEYS == 0 and T >= WIN_BLOCKS * SEL_BLOCK

    tl = np.arange(tq)[None, :, None]
    r = tq // CMP_STRIDE
    span = r * (nq - 1) + ncp
    y = np.arange(span)[None, None, :]
    dy = tl + CMP_STRIDE * (y - (ncp - 1)) - (CMP_LEN - 1)
    master = _bias_table(rel_bias, dy, dy >= 0)[0][:, :, ::-1]
    pad_col = jnp.arange(ncp) < nc
    bc = jnp.stack([jnp.where(pad_col, master[:, :, r * (nq - 1 - i):r * (nq - 1 - i) + ncp], NEG)
                    for i in range(nq)])
    vv = np.arange(NEAR_BLOCKS)[:, None, None]
    dn = vv * tq + tl - np.arange(NEAR_BLOCKS * SEL_BLOCK)[None, None, :]
    bsn = _bias_table(rel_bias, dn, dn >= 0)
    vw_ = np.arange(WIN_BLOCKS)[:, None, None]
    dw = vw_ * tq + tl - np.arange(WIN_BLOCKS * SEL_BLOCK)[None, None, :]
    bw = _bias_table(rel_bias, dw, (dw >= 0) & (dw < WINDOW))

    ci = np.arange(ncp)[:, None]
    sj = np.arange(ns)[None, :]
    cover = ((ci * CMP_STRIDE < (sj + 1) * SEL_BLOCK) & (ci * CMP_STRIDE + CMP_LEN > sj * SEL_BLOCK)
             & (ci < nc))
    cover = jnp.asarray(cover.astype(np.float32), bf16)
    pos = np.arange(T).reshape(T // FAR_KEYS, 1, FAR_KEYS)
    expand = jnp.asarray(np.where(pos // SEL_BLOCK == np.arange(ns)[None, :, None], SEL_MASK, 0.0)
                         .astype(np.float32), bf16)

    padc = ((0, 0), (0, ncp - nc), (0, 0))
    seq = lambda b, i: (b, 0, 0)
    return pl.pallas_call(
        functools.partial(_nsa_prompt_kernel, tq=tq),
        grid=(B, nq),
        in_specs=[
            pl.BlockSpec((1, tq, ATTN_WIDTH), lambda b, i: (b, i, 0)),
            pl.BlockSpec((1, tq, 3 * N_HEADS), lambda b, i: (b, i, 0)),
            pl.BlockSpec((1, ncp, KV_WIDTH), seq),
            pl.BlockSpec((1, ncp, KV_WIDTH), seq),
            pl.BlockSpec((1, T, KV_WIDTH), seq),
            pl.BlockSpec((1, T, KV_WIDTH), seq),
            pl.BlockSpec((1, T, KV_WIDTH), seq),
            pl.BlockSpec((1, T, KV_WIDTH), seq),
            pl.BlockSpec((1, N_HEADS, tq, ncp), lambda b, i: (i, 0, 0, 0)),
            pl.BlockSpec((1, N_HEADS, tq, NEAR_BLOCKS * SEL_BLOCK),
                         lambda b, i: (jnp.minimum(i, NEAR_BLOCKS - 1), 0, 0, 0)),
            pl.BlockSpec((1, N_HEADS, tq, WIN_BLOCKS * SEL_BLOCK),
                         lambda b, i: (jnp.minimum(i, WIN_BLOCKS - 1), 0, 0, 0)),
            pl.BlockSpec((ncp, ns), lambda b, i: (0, 0)),
            pl.BlockSpec((T // FAR_KEYS, ns, FAR_KEYS), lambda b, i: (0, 0, 0)),
        ],
        out_specs=pl.BlockSpec((1, tq, ATTN_WIDTH), lambda b, i: (b, i, 0)),
        out_shape=jax.ShapeDtypeStruct((B, T, ATTN_WIDTH), jnp.float32),
        compiler_params=pltpu.CompilerParams(dimension_semantics=("arbitrary", "arbitrary"),
                                             vmem_limit_bytes=VMEM_LIMIT),
        name="nsa_prompt",
    )(q * scale, gates, jnp.pad(kc, padc).astype(bf16), jnp.pad(vc, padc).astype(bf16),
      ks.astype(bf16), vs.astype(bf16), kw.astype(bf16), vw.astype(bf16), bc, bsn, bw, cover, expand)


NEW_PAD = 16
KV_SLOTS = 3


def _softmax_parts(s):
    m = jnp.max(s, axis=1, keepdims=True)
    p = jnp.exp(s - m)
    return m, p, jnp.sum(p, axis=1, keepdims=True)


def _nsa_sample_kernel(pt_ref, q_ref, g_ref, kc_ref, vc_ref, cache_ref, nsk_ref, nsv_ref, cw_ref, nwk_ref, nwv_ref,
                       bc_ref, bsl_ref, bnew_ref, bw_ref, cover_ref, e_ref, o_ref, kv_buf, sem, *, tn, ns):
    f32, bf16 = jnp.float32, jnp.bfloat16
    nt = (((1,), (1,)), ((), ()))
    b = pl.program_id(0)
    n_pages, page = pt_ref.shape[1], cache_ref.shape[-1]
    pages_per_chunk = FAR_KEYS // page
    n_far = n_pages // pages_per_chunk
    total = pl.num_programs(0) * n_far

    def fetch(gc):
        seq, first = gc // n_far, (gc % n_far) * pages_per_chunk
        slot = gc % KV_SLOTS
        for i in range(pages_per_chunk):
            pltpu.make_async_copy(cache_ref.at[pt_ref[seq, first + i]], kv_buf.at[slot, i], sem.at[slot]).start()

    @pl.when(b == 0)
    def _():
        for gc in range(KV_SLOTS - 1):
            fetch(gc)
    qpad = q_ref[0]
    cur = ns - 1

    def heads(x):
        return jnp.concatenate([x[:tn]] * GROUP + [x[tn:]] * GROUP, axis=0)

    s = lax.dot_general(qpad, kc_ref[0], nt, preferred_element_type=f32) + bc_ref[...]
    _, e, d = _softmax_parts(s)
    pc = e / d
    pc16 = pc.astype(bf16)
    o_cmp = jnp.dot(pc16, vc_ref[0], preferred_element_type=f32)

    imp_h = jnp.dot(pc16, cover_ref[...], preferred_element_type=f32)
    imp = jnp.concatenate([_group_sum(imp_h, grp, tn) for grp in range(N_KV)], axis=0)
    j = lax.broadcasted_iota(jnp.int32, imp.shape, 1)
    forced = (j == 0) | (j == cur) | (j == cur - 1)
    imp = jnp.where(forced, imp + FORCE_SCORE, imp)
    imp = jnp.where(j > cur, -jnp.inf, imp)
    cnt = jnp.zeros(imp.shape, jnp.int32)
    for jp in range(ns):
        col = imp[:, jp:jp + 1]
        ahead = (col > imp) | ((col == imp) & (j > jp))
        cnt = cnt + ahead.astype(jnp.int32)
    sel = cnt < TOP_N
    sel_far = heads(jnp.where(sel & (j < cur), 0.0, -1.0)).astype(bf16)

    bnew = bnew_ref[...]
    m, p, l = _softmax_parts(lax.dot_general(qpad, nsk_ref[0], nt, preferred_element_type=f32) + bnew)
    acc = jnp.dot(p.astype(bf16), nsv_ref[0], preferred_element_type=f32)
    half = GROUP * tn
    q_g = [qpad[grp * half:(grp + 1) * half, grp * HEAD_DIM:(grp + 1) * HEAD_DIM] for grp in range(N_KV)]
    zero = jnp.zeros((half, HEAD_DIM), f32)

    def far_chunk(c, carry, bias):
        m, l, acc = carry
        gc = b * n_far + c
        slot = gc % KV_SLOTS
        pltpu.make_async_copy(kv_buf.at[slot], kv_buf.at[slot], sem.at[slot]).wait()

        @pl.when(gc + KV_SLOTS - 1 < total)
        def _():
            fetch(gc + KV_SLOTS - 1)

        def plane(kv, grp):
            return jnp.concatenate([kv_buf[slot, i, kv, grp] for i in range(pages_per_chunk)], axis=1).astype(bf16)

        s = jnp.concatenate([jnp.dot(q_g[grp], plane(0, grp), preferred_element_type=f32)
                             for grp in range(N_KV)], axis=0)
        s = s + jnp.dot(sel_far, e_ref[c], preferred_element_type=f32)
        if bias is not None:
            s = s + bias
        m_new = jnp.maximum(m, jnp.max(s, axis=1, keepdims=True))
        a = jnp.exp(m - m_new)
        p = jnp.exp(s - m_new)
        l = a * l + jnp.sum(p, axis=1, keepdims=True)
        p = p.astype(bf16)
        pv = [lax.dot_general(p[grp * half:(grp + 1) * half], plane(1, grp), nt, preferred_element_type=f32)
              for grp in range(N_KV)]
        pv = jnp.concatenate([jnp.concatenate([pv[0], zero], axis=1),
                              jnp.concatenate([zero, pv[1]], axis=1)], axis=0)
        return m_new, l, a * acc + pv

    carry = lax.fori_loop(0, n_far - 1, lambda c, carry: far_chunk(c, carry, None), (m, l, acc))
    m, l, acc = far_chunk(n_far - 1, carry, bsl_ref[...])
    o_sel = acc / l

    cw = cw_ref[0]
    s_c = lax.dot_general(qpad, cw[:, :KV_WIDTH].astype(bf16), nt, preferred_element_type=f32) + bw_ref[...]
    s_n = lax.dot_general(qpad, nwk_ref[0], nt, preferred_element_type=f32) + bnew
    m = jnp.maximum(jnp.max(s_c, axis=1, keepdims=True), jnp.max(s_n, axis=1, keepdims=True))
    p_c = jnp.exp(s_c - m)
    p_n = jnp.exp(s_n - m)
    l = jnp.sum(p_c, axis=1, keepdims=True) + jnp.sum(p_n, axis=1, keepdims=True)
    o_win = (jnp.dot(p_c.astype(bf16), cw[:, KV_WIDTH:].astype(bf16), preferred_element_type=f32)
             + jnp.dot(p_n.astype(bf16), nwv_ref[0], preferred_element_type=f32)) / l

    g = g_ref[0]
    o_ref[0] = g[:, 0:1] * o_cmp + g[:, 1:2] * o_sel + g[:, 2:3] * o_win


def nsa_sample_attention(q, gates, kc, vc, cache_sel, page_table, new_sel, cache_win, new_win, rel_bias):
    B, tn = q.shape[0], q.shape[1]
    past, wk, nc = page_table.shape[1] * cache_sel.shape[-1], cache_win.shape[1], kc.shape[1]
    assert past % FAR_KEYS == 0 and past % SEL_BLOCK == 0 and tn <= NEW_PAD and wk == WINDOW
    ns = past // SEL_BLOCK + 1
    nsp = -(-ns // LANES) * LANES
    ncp = -(-nc // LANES) * LANES
    rows = N_HEADS * tn
    bf16 = jnp.bfloat16
    n_far = past // FAR_KEYS

    tl = np.arange(tn)[:, None]
    flat = lambda t: t[0].reshape(rows, t.shape[-1])
    cc = np.arange(ncp)[None, :]
    dc = past + tl - (cc * CMP_STRIDE + CMP_LEN - 1)
    bc = flat(_bias_table(rel_bias, dc[None], ((dc >= 0) & (cc < nc))[None]))
    dl = tl + FAR_KEYS - np.arange(FAR_KEYS)[None, :]
    bsl = flat(_bias_table(rel_bias, dl[None], (dl >= 0)[None]))
    jn = np.arange(NEW_PAD)[None, :]
    dn = tl - jn
    bnew = flat(_bias_table(rel_bias, dn[None], ((dn >= 0) & (jn < tn))[None]))
    dw = wk + tl - np.arange(wk)[None, :]
    bw = flat(_bias_table(rel_bias, dw[None], ((dw >= 0) & (dw < WINDOW))[None]))

    ci = np.arange(ncp)[:, None]
    sj = np.arange(nsp)[None, :]
    cover = ((ci * CMP_STRIDE < (sj + 1) * SEL_BLOCK) & (ci * CMP_STRIDE + CMP_LEN > sj * SEL_BLOCK)
             & (ci < nc) & (sj < ns))
    cover = jnp.asarray(cover.astype(np.float32), bf16)
    pos = np.arange(past).reshape(n_far, 1, FAR_KEYS)
    expand = jnp.asarray(np.where(pos // SEL_BLOCK == np.arange(nsp)[None, :, None], SEL_MASK, 0.0)
                         .astype(np.float32), bf16)

    qh = jnp.transpose((q * HEAD_DIM ** -0.5).reshape(B, tn, N_HEADS, HEAD_DIM), (0, 2, 1, 3))
    zero = jnp.zeros_like(qh)
    grp = (np.arange(N_HEADS) // GROUP)[None, :, None, None]
    qpad = jnp.where(grp == 0, jnp.concatenate([qh, zero], -1), jnp.concatenate([zero, qh], -1))
    qpad = qpad.reshape(B, rows, KV_WIDTH).astype(bf16)
    grow = jnp.transpose(gates.reshape(B, tn, N_HEADS, 3), (0, 2, 1, 3)).reshape(B, rows, 3)
    padn = lambda a: jnp.pad(a, ((0, 0), (0, NEW_PAD - tn), (0, 0))).astype(bf16)
    padc = ((0, 0), (0, ncp - nc), (0, 0))

    seq3 = lambda b, pt: (b, 0, 0)
    const2 = lambda b, pt: (0, 0)
    grid_spec = pltpu.PrefetchScalarGridSpec(
        num_scalar_prefetch=1,
        grid=(B,),
        in_specs=[
            pl.BlockSpec((1, rows, KV_WIDTH), seq3),
            pl.BlockSpec((1, rows, 3), seq3),
            pl.BlockSpec((1, ncp, KV_WIDTH), seq3),
            pl.BlockSpec((1, ncp, KV_WIDTH), seq3),
            pl.BlockSpec(memory_space=pl.ANY),
            pl.BlockSpec((1, NEW_PAD, KV_WIDTH), seq3),
            pl.BlockSpec((1, NEW_PAD, KV_WIDTH), seq3),
            pl.BlockSpec((1, wk, 2 * KV_WIDTH), seq3),
            pl.BlockSpec((1, NEW_PAD, KV_WIDTH), seq3),
            pl.BlockSpec((1, NEW_PAD, KV_WIDTH), seq3),
            pl.BlockSpec((rows, ncp), const2),
            pl.BlockSpec((rows, FAR_KEYS), const2),
            pl.BlockSpec((rows, NEW_PAD), const2),
            pl.BlockSpec((rows, wk), const2),
            pl.BlockSpec((ncp, nsp), const2),
            pl.BlockSpec((n_far, nsp, FAR_KEYS), lambda b, pt: (0, 0, 0)),
        ],
        out_specs=pl.BlockSpec((1, rows, KV_WIDTH), seq3),
        scratch_shapes=[pltpu.VMEM((KV_SLOTS, FAR_KEYS // cache_sel.shape[-1]) + cache_sel.shape[1:], jnp.float32),
                        pltpu.SemaphoreType.DMA((KV_SLOTS,))],
    )
    out = pl.pallas_call(
        functools.partial(_nsa_sample_kernel, tn=tn, ns=ns),
        grid_spec=grid_spec,
        out_shape=jax.ShapeDtypeStruct((B, rows, KV_WIDTH), jnp.float32),
        compiler_params=pltpu.CompilerParams(dimension_semantics=("arbitrary",),
                                             vmem_limit_bytes=VMEM_LIMIT),
        name="nsa_sample",
    )(page_table, qpad, grow, jnp.pad(kc, padc).astype(bf16), jnp.pad(vc, padc).astype(bf16), cache_sel,
      padn(new_sel[:, :, :KV_WIDTH]), padn(new_sel[:, :, KV_WIDTH:]), cache_win,
      padn(new_win[:, :, :KV_WIDTH]), padn(new_win[:, :, KV_WIDTH:]), bc, bsl, bnew, bw, cover, expand)
    o = out.reshape(B, N_HEADS, tn, 2, HEAD_DIM)
    o = jnp.where(grp[..., None] == 0, o[:, :, :, 0:1], o[:, :, :, 1:2])[:, :, :, 0]
    return jnp.transpose(o, (0, 2, 1, 3)).reshape(B, tn, ATTN_WIDTH)


N_SUB = 2 * PEER_HEADS
HALF_DQ = PEER_DQ // 2
N_PAIRS = PEER_HEADS * PEER_TOPK


def _top_rounds(s, row, n):
    width = s.shape[0]
    k = lax.broadcasted_iota(jnp.int32, (n, s.shape[1]), 0)
    vals = jnp.zeros((n, s.shape[1]), jnp.float32)
    ids = jnp.zeros((n, s.shape[1]), jnp.float32)
    for a in range(n):
        m = jnp.max(s, axis=0, keepdims=True)
        ix = jnp.min(jnp.where(s == m, row, float(width)), axis=0, keepdims=True)
        vals = jnp.where(k == a, m, vals)
        ids = jnp.where(k == a, ix, ids)
        s = jnp.where(row == ix, -jnp.inf, s)
    return vals, ids


def _peer_route_kernel(h_ref, wq_ref, keys_ref, idx_ref, g_ref, q_scr, e_scr, w_scr):
    f32, bf16 = jnp.float32, jnp.bfloat16
    tm = h_ref.shape[0]
    nt = (((1,), (1,)), ((), ()))
    q = jnp.dot(h_ref[...].astype(bf16), wq_ref[...], preferred_element_type=f32)
    for k in range(N_SUB):
        q_scr[k] = q[:, k * HALF_DQ:(k + 1) * HALF_DQ].astype(bf16)
    row = lax.broadcasted_iota(jnp.int32, (N_KEYS, tm), 0).astype(f32)
    cand_blocks, a = [], 0
    while PEER_TOPK // (a + 1) > 1:
        cand_blocks.append((a, a + 1, min(-(-(PEER_TOPK // (a + 1)) // 8) * 8, PEER_TOPK)))
        a += 1
    cand_blocks.append((a, PEER_TOPK, 1))
    row2 = jnp.concatenate(
        [(lax.broadcasted_iota(jnp.int32, (max(a1 - a0, nb), tm), 0) * (1 if nb > 1 else PEER_TOPK)
          + a0 * PEER_TOPK).astype(f32) for a0, a1, nb in cand_blocks], axis=0)
    n_cand = PEER_TOPK * PEER_TOPK

    def head(h, carry):
        s1, i1 = _top_rounds(lax.dot_general(keys_ref[2 * h], q_scr[2 * h], nt, preferred_element_type=f32),
                             row, PEER_TOPK)
        s2, i2 = _top_rounds(lax.dot_general(keys_ref[2 * h + 1], q_scr[2 * h + 1], nt,
                                             preferred_element_type=f32), row, PEER_TOPK)
        cand = jnp.concatenate([s1[a0:a1] + s2[:nb] for a0, a1, nb in cand_blocks], axis=0)
        cidx = jnp.concatenate([i1[a0:a1] * float(N_KEYS) + i2[:nb] for a0, a1, nb in cand_blocks], axis=0)
        k = lax.broadcasted_iota(jnp.int32, (PEER_TOPK, tm), 0)
        top = jnp.zeros((PEER_TOPK, tm), f32)
        eid = jnp.zeros((PEER_TOPK, tm), f32)
        for a in range(PEER_TOPK):
            m = jnp.max(cand, axis=0, keepdims=True)
            pos = jnp.min(jnp.where(cand == m, row2, float(n_cand)), axis=0, keepdims=True)
            hit = row2 == pos
            e = jnp.max(jnp.where(hit, cidx, -1.0), axis=0, keepdims=True)
            top = jnp.where(k == a, m, top)
            eid = jnp.where(k == a, e, eid)
            cand = jnp.where(hit, -jnp.inf, cand)
        ex = jnp.exp(top - top[0:1])
        r0 = pl.multiple_of(h * PEER_TOPK, PEER_TOPK)
        e_scr[pl.ds(r0, PEER_TOPK), :] = eid
        w_scr[pl.ds(r0, PEER_TOPK), :] = ex / jnp.sum(ex, axis=0, keepdims=True)
        return carry

    lax.fori_loop(0, PEER_HEADS, head, 0)
    idx_ref[...] = e_scr[...].T.astype(jnp.int32)
    g_ref[...] = w_scr[...].T


def peer_route(h2, peer_q, peer_keys, tm=128):
    n = h2.shape[0]
    tm = min(tm, n)
    bf16 = jnp.bfloat16
    keys = peer_keys.reshape(N_SUB, N_KEYS, HALF_DQ).astype(bf16)
    return pl.pallas_call(
        _peer_route_kernel,
        grid=(n // tm,),
        in_specs=[pl.BlockSpec((tm, D_MODEL), lambda i: (i, 0)),
                  pl.BlockSpec((D_MODEL, PEER_HEADS * PEER_DQ), lambda i: (0, 0)),
                  pl.BlockSpec((N_SUB, N_KEYS, HALF_DQ), lambda i: (0, 0, 0))],
        out_specs=[pl.BlockSpec((tm, N_PAIRS), lambda i: (i, 0)),
                   pl.BlockSpec((tm, N_PAIRS), lambda i: (i, 0))],
        out_shape=[jax.ShapeDtypeStruct((n, N_PAIRS), jnp.int32),
                   jax.ShapeDtypeStruct((n, N_PAIRS), jnp.float32)],
        scratch_shapes=[pltpu.VMEM((N_SUB, tm, HALF_DQ), bf16),
                        pltpu.VMEM((N_PAIRS, tm), jnp.float32),
                        pltpu.VMEM((N_PAIRS, tm), jnp.float32)],
        compiler_params=pltpu.CompilerParams(dimension_semantics=("arbitrary",),
                                             vmem_limit_bytes=VMEM_LIMIT),
        name="peer_route",
    )(h2, peer_q.astype(bf16), keys)


def _gelu_tanh(x):
    return 0.5 * x * (1.0 + jnp.tanh(math.sqrt(2.0 / math.pi) * (x + 0.044715 * (x * x * x))))


N_SLOTS = 8
DMA_QUEUES = 2
ROW_TILES = 2 * D_MODEL // LANES
U_TILES = D_MODEL // LANES


def _peer_expert_kernel(idx_ref, nxt_ref, x_ref, g_ref, uv_ref, y_ref, *scratch):
    tb = x_ref.shape[0]
    i = pl.program_id(0)
    last = pl.num_programs(0) - 1
    ahead = N_SLOTS - 1
    bufs, sem = scratch[:N_SLOTS], scratch[N_SLOTS]

    def issue(ids, t, slot):
        for k in range(N_PAIRS):
            pltpu.make_async_copy(uv_ref.at[ids[t, k]], bufs[slot].at[k // 8, :, k % 8, :],
                                  sem.at[slot]).start(priority=k % DMA_QUEUES)

    def wait_slot(slot):
        pltpu.make_async_copy(bufs[slot], bufs[slot], sem.at[slot]).wait()

    eye = (lax.broadcasted_iota(jnp.int32, (N_PAIRS, N_PAIRS), 0)
           == lax.broadcasted_iota(jnp.int32, (N_PAIRS, N_PAIRS), 1))

    def compute(t, slot):
        buf = bufs[slot]

        def chunk(c):
            return buf[:, c].reshape(N_PAIRS, LANES)

        x = x_ref[t]
        acc = chunk(0) * x[0:1, :]
        for s in range(1, U_TILES):
            acc = acc + chunk(s) * x[s:s + 1, :]
        a = jnp.sum(acc, axis=1, keepdims=True)
        g_col = jnp.sum(jnp.where(eye, g_ref[pl.ds(t, 1), :], 0.0), axis=1, keepdims=True)
        w = g_col * _gelu_tanh(a)
        y = [jnp.sum(w * chunk(U_TILES + s), axis=0, keepdims=True) for s in range(U_TILES)]
        y_ref[pl.ds(t, 1), :] = jnp.concatenate(y, axis=1)

    @pl.when(i == 0)
    def _():
        for t in range(ahead):
            issue(idx_ref, t, t)

    def group(j, c):
        for u in range(N_SLOTS):
            t = j * N_SLOTS + u
            wait_slot(u)
            issue(idx_ref, t + ahead, (u + ahead) % N_SLOTS)
            compute(t, u)
        return c

    n_groups = tb // N_SLOTS
    lax.fori_loop(0, n_groups - 1, group, 0)
    for u in range(N_SLOTS):
        t = tb - N_SLOTS + u
        wait_slot(u)
        if u == 0:
            issue(idx_ref, t + ahead, ahead)
        else:
            issue(nxt_ref, u - 1, u - 1)
        compute(t, u)

    @pl.when(i == last)
    def _():
        for u in range(ahead):
            wait_slot(u)


def peer_experts(h2, eidx, gate, expert_u, expert_v, tb=64):
    n = h2.shape[0]
    tb = min(tb, n)
    nb = n // tb
    e = expert_u.shape[0]
    uv = jnp.concatenate([expert_u.reshape(e, U_TILES, LANES), expert_v.reshape(e, U_TILES, LANES)], axis=1)
    return pl.pallas_call(
        _peer_expert_kernel,
        grid=(nb,),
        in_specs=[pl.BlockSpec((tb, N_PAIRS), lambda i: (i, 0), memory_space=pltpu.SMEM),
                  pl.BlockSpec((tb, N_PAIRS), lambda i: (jnp.minimum(i + 1, nb - 1), 0), memory_space=pltpu.SMEM),
                  pl.BlockSpec((tb, U_TILES, LANES), lambda i: (i, 0, 0)),
                  pl.BlockSpec((tb, N_PAIRS), lambda i: (i, 0)),
                  pl.BlockSpec(memory_space=pl.ANY)],
        out_specs=pl.BlockSpec((tb, D_MODEL), lambda i: (i, 0)),
        out_shape=jax.ShapeDtypeStruct((n, D_MODEL), jnp.float32),
        scratch_shapes=[pltpu.VMEM((N_PAIRS // 8, ROW_TILES, 8, LANES), jnp.float32)] * N_SLOTS
        + [pltpu.SemaphoreType.DMA((N_SLOTS,))],
        compiler_params=pltpu.CompilerParams(dimension_semantics=("arbitrary",),
                                             vmem_limit_bytes=VMEM_LIMIT),
        name="peer_experts",
    )(eidx, eidx, h2.reshape(n, U_TILES, LANES), gate, uv)


def rms_norm(x, g):
    y = x * lax.rsqrt(jnp.mean(x * x, axis=-1, keepdims=True) + EPS)
    return y * g


def layer_norm(x, g, b):
    mu = jnp.mean(x, axis=-1, keepdims=True)
    var = jnp.mean(jnp.square(x - mu), axis=-1, keepdims=True)
    return (x - mu) * lax.rsqrt(var + EPS) * g + b


def masked_softmax(logits, mask, axis):
    neg = jnp.where(mask, logits, -jnp.inf)
    m = jnp.max(neg, axis=axis, keepdims=True)
    m = jnp.where(jnp.isfinite(m), m, 0.0)
    e = jnp.where(mask, jnp.exp(neg - m), 0.0)
    d = jnp.sum(e, axis=axis, keepdims=True)
    return e / jnp.where(d > 0.0, d, 1.0)


def t5_bucket(dist):
    n = jnp.maximum(dist, 0)
    exact = N_BUCKETS // 2
    nf = jnp.maximum(n, 1).astype(jnp.float32)
    large = exact + (jnp.log(nf / exact) / math.log(MAX_DISTANCE / exact) * (N_BUCKETS - exact)).astype(jnp.int32)
    return jnp.where(n < exact, n, jnp.minimum(large, N_BUCKETS - 1))


def compress(rows, pe, w1, b1, w2, b2):
    B, L = rows.shape[0], rows.shape[1]
    ratio = CMP_LEN // CMP_STRIDE
    n_chunks = L // CMP_STRIDE
    nc = n_chunks - ratio + 1
    r = rows[:, :n_chunks * CMP_STRIDE].reshape(B, n_chunks, CMP_STRIDE, N_KV, HEAD_DIM)
    w1c = w1.reshape(ratio, CMP_STRIDE, HEAD_DIM, CMP_HIDDEN)
    parts = jnp.einsum('bcsgd,jsdh->jbcgh', r, w1c)
    hid = b1 + pe.reshape(-1) @ w1
    for j in range(ratio):
        hid = hid + parts[j, :, j:j + nc]
    out = jax.nn.gelu(hid) @ w2 + b2
    end = jnp.arange(nc, dtype=jnp.int32) * CMP_STRIDE + (CMP_LEN - 1)
    return out, end


def compress_branch(rows, lp, which):
    return compress(rows, lp['cmp_pe'][which], lp['cmp_w1'][which], lp['cmp_b1'][which],
                    lp['cmp_w2'][which], lp['cmp_b2'][which])


def to_blocks(rows):
    B, L = rows.shape[0], rows.shape[1]
    ns = -(-L // SEL_BLOCK)
    rows = jnp.pad(rows, ((0, 0), (0, ns * SEL_BLOCK - L), (0, 0), (0, 0)))
    return jnp.transpose(rows.reshape(B, ns, SEL_BLOCK, N_KV, HEAD_DIM), (0, 3, 1, 2, 4))


def nsa_block(q, t_pos, gates, kc, vc, kc_end, ks_blk, vs_blk, kw, vw, w_pos, rel_bias):
    B = q.shape[0]
    scale = HEAD_DIM ** -0.5
    tbl = rel_bias.astype(jnp.float32).reshape(N_BUCKETS, N_KV, GROUP)
    lc = jnp.einsum('bqgrd,bcgd->bqgrc', q, kc).astype(jnp.float32) * scale
    dc = t_pos[:, None] - kc_end[None, :]
    lc = lc + jnp.transpose(tbl[t5_bucket(dc)], (0, 2, 3, 1))[None]
    pc = masked_softmax(lc, (dc >= 0)[None, :, None, None, :], axis=-1)
    o_cmp = jnp.einsum('bqgrc,bcgd->bqgrd', pc.astype(vc.dtype), vc)
    nc, ns = kc.shape[1], ks_blk.shape[2]
    ci = jnp.arange(nc)[:, None]
    sj = jnp.arange(ns)[None, :]
    cover = ((ci * CMP_STRIDE < (sj + 1) * SEL_BLOCK) & (ci * CMP_STRIDE + CMP_LEN > sj * SEL_BLOCK)).astype(jnp.float32)
    imp = jnp.einsum('bqgrc,cj->bqgj', pc, cover)
    cur = (t_pos // SEL_BLOCK)[:, None]
    forced = (sj == 0) | (sj == cur) | (sj == cur - 1)
    valid = sj * SEL_BLOCK <= t_pos[:, None]
    imp = jnp.where(forced[None, :, None, :], imp + FORCE_SCORE, imp)
    imp = jnp.where(valid[None, :, None, :], imp, -FORCE_SCORE)
    _, idx = lax.top_k(imp, min(TOP_N, ns))
    bi = jnp.arange(B)[:, None, None, None]
    gi = jnp.arange(N_KV)[None, None, :, None]
    ks_g = ks_blk[bi, gi, idx]
    vs_g = vs_blk[bi, gi, idx]
    ls = jnp.einsum('bqgrd,bqgnkd->bqgrnk', q, ks_g).astype(jnp.float32) * scale
    pos = idx[..., None] * SEL_BLOCK + jnp.arange(SEL_BLOCK, dtype=jnp.int32)
    ds = t_pos[None, :, None, None, None] - pos
    tblk = jnp.transpose(tbl, (1, 0, 2))
    bs = tblk[jnp.arange(N_KV)[None, None, :, None, None], t5_bucket(ds)]
    ls = ls + jnp.moveaxis(bs, -1, 3)
    ps = masked_softmax(ls, (ds >= 0)[:, :, :, None], axis=(-2, -1))
    o_sel = jnp.einsum('bqgrnk,bqgnkd->bqgrd', ps.astype(vs_g.dtype), vs_g)
    lw = jnp.einsum('bqgrd,bwgd->bqgrw', q, kw).astype(jnp.float32) * scale
    dw = t_pos[:, None] - w_pos[None, :]
    lw = lw + jnp.transpose(tbl[t5_bucket(dw)], (0, 2, 3, 1))[None]
    mw = (dw >= 0) & (dw < WINDOW) & (w_pos >= 0)[None, :]
    pw = masked_softmax(lw, mw[None, :, None, None, :], axis=-1)
    o_win = jnp.einsum('bqgrw,bwgd->bqgrd', pw.astype(vw.dtype), vw)
    return gates[..., 0:1] * o_cmp + gates[..., 1:2] * o_sel + gates[..., 2:3] * o_win


def nsa_prompt(q, gates, kc_raw, vc_raw, ks, vs, kw, vw, lp, rel_bias):
    B, T = q.shape[0], q.shape[1]
    kc, kc_end = compress_branch(kc_raw, lp, 0)
    kc = rms_norm(kc, lp['k_norm'][0])
    vc, _ = compress_branch(vc_raw, lp, 1)
    flat = lambda a: a.reshape(B, a.shape[1], -1)
    o = nsa_prompt_attention(flat(q), flat(gates), flat(kc), flat(vc), flat(ks), flat(vs), flat(kw), flat(vw),
                             rel_bias)
    new_win = jnp.stack([kw, vw], axis=2)[:, T - min(WINDOW, T):]
    return o, (jnp.stack([kc_raw, vc_raw], axis=2), jnp.stack([ks, vs], axis=2), new_win)


def nsa_sample(q, gates, kc_raw, vc_raw, ks, vs, kw, vw, cache_cmp, cache_sel, cache_win, page_table, lp, rel_bias):
    Bd, Tn = q.shape[0], q.shape[1]
    past = page_table.shape[1] * cache_cmp.shape[1]
    t_pos = past + jnp.arange(Tn, dtype=jnp.int32)
    past_cmp = cache_cmp[page_table].reshape(Bd, past, 2, N_KV, HEAD_DIM)
    kc, kc_end = compress_branch(jnp.concatenate([past_cmp[:, :, 0], kc_raw], axis=1), lp, 0)
    kc = rms_norm(kc, lp['k_norm'][0])
    vc, _ = compress_branch(jnp.concatenate([past_cmp[:, :, 1], vc_raw], axis=1), lp, 1)
    wk = cache_win.shape[1]
    flat = lambda a: a.reshape(Bd, a.shape[1], -1)
    new_sel = jnp.concatenate([flat(ks), flat(vs)], axis=-1)
    new_wkv = jnp.concatenate([flat(kw), flat(vw)], axis=-1)
    o = nsa_sample_attention(flat(q), flat(gates), flat(kc), flat(vc), jnp.transpose(cache_sel, (0, 2, 3, 4, 1)),
                             page_table, new_sel, flat(cache_win), new_wkv, rel_bias)
    win = jnp.concatenate([cache_win, jnp.stack([kw, vw], axis=2)], axis=1)
    new_win = win[:, wk + Tn - min(WINDOW, wk + Tn):]
    return o.reshape(Bd, Tn, ATTN_WIDTH), (jnp.stack([kc_raw, vc_raw], axis=2), jnp.stack([ks, vs], axis=2), new_win)


def conformer_conv(glu_in, hist, lp):
    u = glu_in[..., :CONV_WIDTH] * jax.nn.sigmoid(glu_in[..., CONV_WIDTH:])
    uh = jnp.concatenate([hist.astype(u.dtype), u], axis=1)
    y = lax.conv_general_dilated(uh, lp['conv_w'][:, None, :], window_strides=(1,), padding='VALID',
                                 dimension_numbers=('NWC', 'WIO', 'NWC'), feature_group_count=CONV_WIDTH)
    y = jax.nn.silu(layer_norm(y + lp['conv_b'], lp['conv_ln_g'], lp['conv_ln_b']))
    return y, uh[:, uh.shape[1] - (CONV_K - 1):]


def peer(h, lp):
    shp = h.shape
    xt = h.reshape(-1, D_MODEL)
    n = xt.shape[0]
    eidx, gate = peer_route(xt, lp['peer_q'], lp['peer_keys'])
    y = peer_experts(xt, eidx, gate, lp['expert_u'], lp['expert_v'])
    return y.reshape(shp)


def split_points():
    pts, acc = [], 0
    for s in SPLIT_SIZES[:-1]:
        acc += s
        pts.append(acc)
    return pts


def layer(x, c, conv_hist, lp, rel_bias, cache):
    B, T = x.shape[0], x.shape[1]
    mod = (jax.nn.silu(c) @ lp['w_ada'] + lp['b_ada'])[:, None, :]
    sh1, sc1, gt1, sh2, sc2, gt2 = jnp.split(mod, 6, axis=-1)
    h = rms_norm(x, lp['norm1']) * (1 + sc1) + sh1
    z = matmul(h.reshape(B * T, D_MODEL), lp['w_in']).reshape(B, T, IN_COLS)
    zq, zkc, zvc, zks, zvs, zkw, zvw, zg, zglu = jnp.split(z, split_points(), axis=-1)

    def kv(z):
        return z.reshape(B, T, N_KV, HEAD_DIM)

    q = rms_norm(zq.reshape(B, T, N_KV, GROUP, HEAD_DIM), lp['q_norm'])
    ks = rms_norm(kv(zks), lp['k_norm'][1])
    kw = rms_norm(kv(zkw), lp['k_norm'][2])
    gates = jax.nn.sigmoid(zg.reshape(B, T, N_KV, GROUP, 3))
    if cache is None:
        o_attn, attn_state = nsa_prompt(q, gates, kv(zkc), kv(zvc), ks, kv(zvs), kw, kv(zvw), lp, rel_bias)
    else:
        cache_cmp, cache_sel, cache_win, page_table = cache
        o_attn, attn_state = nsa_sample(q, gates, kv(zkc), kv(zvc), ks, kv(zvs), kw, kv(zvw),
                                        cache_cmp, cache_sel, cache_win, page_table, lp, rel_bias)
    o_conv, conv_state = conformer_conv(zglu, conv_hist, lp)
    cat = jnp.concatenate([rms_norm(o_attn, lp['attn_out_norm']), o_conv * lp['conv_out_scale']], axis=-1)
    mix = matmul(cat.reshape(B * T, D_MODEL), lp['w_out']).reshape(B, T, D_MODEL)
    x = x + gt1 * mix
    h2 = rms_norm(x, lp['norm2']) * (1 + sc2) + sh2
    x = x + gt2 * peer(h2, lp)
    return x, attn_state, conv_state


def kernel(x_prompt, x_sample, c_prompt, c_sample, cache_cmp_kv, cache_sel_kv, cache_win_kv, state_conv, page_table, rel_bias, w_ada, b_ada, norm1, w_in, q_norm, k_norm, cmp_pe, cmp_w1, cmp_b1, cmp_w2, cmp_b2, conv_w, conv_b, conv_ln_g, conv_ln_b, attn_out_norm, conv_out_scale, w_out, norm2, peer_q, peer_keys, expert_u, expert_v):
    lp = {
        'w_ada': w_ada[0], 'b_ada': b_ada[0], 'norm1': norm1[0], 'w_in': w_in[0],
        'q_norm': q_norm[0], 'k_norm': k_norm[0], 'cmp_pe': cmp_pe[0], 'cmp_w1': cmp_w1[0],
        'cmp_b1': cmp_b1[0], 'cmp_w2': cmp_w2[0], 'cmp_b2': cmp_b2[0], 'conv_w': conv_w[0],
        'conv_b': conv_b[0], 'conv_ln_g': conv_ln_g[0], 'conv_ln_b': conv_ln_b[0],
        'attn_out_norm': attn_out_norm[0], 'conv_out_scale': conv_out_scale[0], 'w_out': w_out[0],
        'norm2': norm2[0], 'peer_q': peer_q[0], 'peer_keys': peer_keys[0],
        'expert_u': expert_u[0], 'expert_v': expert_v[0],
    }
    hist0 = jnp.zeros((x_prompt.shape[0], CONV_K - 1, CONV_WIDTH), x_prompt.dtype)
    y_prompt, (a_c, a_s, a_w), cv = layer(x_prompt, c_prompt, hist0, lp, rel_bias, None)
    y_sample, (b_c, b_s, b_w), cvs = layer(x_sample, c_sample, state_conv[0], lp, rel_bias,
                                           (cache_cmp_kv[0], cache_sel_kv[0], cache_win_kv[0], page_table))
    return (y_prompt, y_sample, a_c[None], a_s[None], a_w[None], cv[None],
            b_c[None], b_s[None], b_w[None], cvs[None])
```

```python
import functools
import math

import jax
import jax.numpy as jnp
import numpy as np
from jax import lax
from jax.experimental import pallas as pl
from jax.experimental.pallas import tpu as pltpu

D_MODEL = 1024
HEAD_DIM = 64
ATTN_WIDTH = D_MODEL // 2
N_HEADS = ATTN_WIDTH // HEAD_DIM
N_KV = 2
GROUP = N_HEADS // N_KV
KV_WIDTH = N_KV * HEAD_DIM
CONV_WIDTH = D_MODEL - ATTN_WIDTH
CONV_K = 31
CMP_LEN = 32
CMP_STRIDE = 16
CMP_HIDDEN = 256
SEL_BLOCK = 64
TOP_N = 16
WINDOW = 512
N_BUCKETS = 32
MAX_DISTANCE = 128
Q_BLOCK = 64
PEER_HEADS = 8
PEER_DQ = 256
N_KEYS = 128
PEER_TOPK = 16
PEER_BLOCK = 256
FORCE_SCORE = 1.0e4
EPS = 1e-6
SPLIT_SIZES = (ATTN_WIDTH, KV_WIDTH, KV_WIDTH, KV_WIDTH, KV_WIDTH, KV_WIDTH, KV_WIDTH, 3 * N_HEADS, 2 * CONV_WIDTH)
IN_COLS = sum(SPLIT_SIZES)

LANES = 128
VMEM_LIMIT = 48 * 1024 * 1024


def _matmul_kernel(x_ref, w_ref, o_ref):
    o_ref[...] = jnp.dot(x_ref[...].astype(jnp.bfloat16), w_ref[...],
                         preferred_element_type=jnp.float32)


def matmul(x, w, tm=512):
    m, k = x.shape
    n = w.shape[1]
    n_pad = -(-n // LANES) * LANES
    wb = jnp.pad(w, ((0, 0), (0, n_pad - n))).astype(jnp.bfloat16)
    tm = min(tm, m)
    out = pl.pallas_call(
        _matmul_kernel,
        grid=(m // tm,),
        in_specs=[pl.BlockSpec((tm, k), lambda i: (i, 0)),
                  pl.BlockSpec((k, n_pad), lambda i: (0, 0))],
        out_specs=pl.BlockSpec((tm, n_pad), lambda i: (i, 0)),
        out_shape=jax.ShapeDtypeStruct((m, n_pad), jnp.float32),
        compiler_params=pltpu.CompilerParams(dimension_semantics=("arbitrary",),
                                             vmem_limit_bytes=VMEM_LIMIT),
        name="matmul",
    )(x, wb)
    return out[:, :n]


NEG = -1.0e30
SEL_MASK = 16384.0
FAR_KEYS = 512
NEAR_BLOCKS = 3
WIN_BLOCKS = WINDOW // SEL_BLOCK + 1


def _bucket_np(d):
    n = np.maximum(d, 0)
    exact = N_BUCKETS // 2
    nf = np.maximum(n, 1).astype(np.float32)
    large = exact + (np.log(nf / np.float32(exact)) / np.float32(math.log(MAX_DISTANCE / exact))
                     * np.float32(N_BUCKETS - exact)).astype(np.int32)
    return np.where(n < exact, n, np.minimum(large, N_BUCKETS - 1))


def _bias_table(rel_bias, dist, valid):
    tbl = rel_bias.astype(jnp.float32) - rel_bias[N_BUCKETS - 1].astype(jnp.float32)[None, :]
    b = jnp.transpose(tbl[_bucket_np(dist)], (0, 3, 1, 2))
    return jnp.where(valid[:, None], b, NEG)


def _toeplitz_bias(rel_bias, n_var, tq, m, valid_fn):
    n = n_var * tq
    period = n + m
    k = np.arange(period)
    d = np.where(k < m, -k, period - k)
    g = _bias_table(rel_bias, d[None, None, :], valid_fn(d)[None, None, :])[0, :, 0, :]
    t = jnp.tile(g, (1, n))[:, :n * (period - 1)].reshape(N_HEADS, n, period - 1)[:, :, :m]
    return jnp.transpose(t.reshape(N_HEADS, n_var, tq, m), (1, 0, 2, 3))


def _group_sum(x, grp, t):
    acc = x[grp * GROUP * t:(grp * GROUP + 1) * t]
    for r in range(1, GROUP):
        acc = acc + x[(grp * GROUP + r) * t:(grp * GROUP + r + 1) * t]
    return acc


def _nsa_prompt_kernel(q_ref, g_ref, kc_ref, vc_ref, ks_ref, vs_ref, kw_ref, vw_ref,
                       bc_ref, bsn_ref, bw_ref, cover_ref, e_ref, o_ref, *, tq):
    f32, bf16 = jnp.float32, jnp.bfloat16
    i = pl.program_id(1)
    rows = N_HEADS * tq
    nt = (((1,), (1,)), ((), ()))

    q = q_ref[0]
    lane_half = lax.broadcasted_iota(jnp.int32, (tq, 2 * HEAD_DIM), 1) // HEAD_DIM
    parts = []
    for h in range(N_HEADS):
        blk = q[:, (h // 2) * 2 * HEAD_DIM:(h // 2 + 1) * 2 * HEAD_DIM]
        grp = h // GROUP
        if h % 2 != grp:
            blk = pltpu.roll(blk, HEAD_DIM, 1)
        parts.append(jnp.where(lane_half == grp, blk, 0.0))
    qpad = jnp.concatenate(parts, axis=0).astype(bf16)

    def heads(x):
        return jnp.concatenate([x[:tq]] * GROUP + [x[tq:]] * GROUP, axis=0)

    tbl_c = bc_ref[0].reshape(rows, bc_ref.shape[-1])
    s = lax.dot_general(qpad, kc_ref[0], nt, preferred_element_type=f32) + tbl_c
    ok = tbl_c > 0.5 * NEG
    m = jnp.max(s, axis=1, keepdims=True)
    e = jnp.where(ok, jnp.exp(s - m), 0.0)
    d = jnp.sum(e, axis=1, keepdims=True)
    pc = e / jnp.where(d > 0.0, d, 1.0)
    pc16 = pc.astype(bf16)
    o_cmp = jnp.dot(pc16, vc_ref[0], preferred_element_type=f32)

    imp_h = jnp.dot(pc16, cover_ref[...], preferred_element_type=f32)
    imp = jnp.concatenate([_group_sum(imp_h, grp, tq) for grp in range(N_KV)], axis=0)
    ns = imp.shape[1]
    j = lax.broadcasted_iota(jnp.int32, imp.shape, 1)
    forced = (j == 0) | (j == i) | (j == i - 1)
    imp = jnp.where(forced, imp + FORCE_SCORE, imp)
    imp = jnp.where(j > i, -FORCE_SCORE, imp)
    cnt = jnp.zeros(imp.shape, jnp.int32)
    for jp in range(ns):
        col = imp[:, jp:jp + 1]
        ahead = (col > imp) | ((col == imp) & (j > jp))
        cnt = cnt + ahead.astype(jnp.int32)
    sel = cnt < TOP_N

    nb0 = jnp.maximum(i - (NEAR_BLOCKS - 1), 0)
    near = NEAR_BLOCKS * SEL_BLOCK
    k0 = pl.multiple_of(nb0 * SEL_BLOCK, SEL_BLOCK)
    sel_nb0 = jnp.sum(jnp.where((j == nb0) & sel, 1.0, 0.0), axis=1, keepdims=True)
    lane = lax.broadcasted_iota(jnp.int32, (rows, near), 1)
    s = lax.dot_general(qpad, ks_ref[0, pl.ds(k0, near), :], nt, preferred_element_type=f32)
    s = s + bsn_ref[0].reshape(rows, near)
    s = s + jnp.where(lane < SEL_BLOCK, (heads(sel_nb0) - 1.0) * SEL_MASK, 0.0)
    m = jnp.max(s, axis=1, keepdims=True)
    p = jnp.exp(s - m)
    l = jnp.sum(p, axis=1, keepdims=True)
    acc = jnp.dot(p.astype(bf16), vs_ref[0, pl.ds(k0, near), :], preferred_element_type=f32)

    sel_far = heads(jnp.where(sel & (j < nb0), 0.0, -1.0)).astype(bf16)
    n_far = (nb0 * SEL_BLOCK + FAR_KEYS - 1) // FAR_KEYS

    def far_body(c, carry):
        m, l, acc = carry
        c0 = pl.multiple_of(c * FAR_KEYS, FAR_KEYS)
        s = lax.dot_general(qpad, ks_ref[0, pl.ds(c0, FAR_KEYS), :], nt, preferred_element_type=f32)
        s = s + jnp.dot(sel_far, e_ref[c], preferred_element_type=f32)
        m_new = jnp.maximum(m, jnp.max(s, axis=1, keepdims=True))
        a = jnp.exp(m - m_new)
        p = jnp.exp(s - m_new)
        l = a * l + jnp.sum(p, axis=1, keepdims=True)
        acc = a * acc + jnp.dot(p.astype(bf16), vs_ref[0, pl.ds(c0, FAR_KEYS), :],
                                preferred_element_type=f32)
        return m_new, l, acc

    m, l, acc = lax.fori_loop(0, n_far, far_body, (m, l, acc))
    o_sel = acc / l

    wk = WIN_BLOCKS * SEL_BLOCK
    w0 = pl.multiple_of(jnp.maximum(i - (WIN_BLOCKS - 1), 0) * SEL_BLOCK, SEL_BLOCK)
    s = lax.dot_general(qpad, kw_ref[0, pl.ds(w0, wk), :], nt, preferred_element_type=f32)
    s = s + bw_ref[0].reshape(rows, wk)
    m = jnp.max(s, axis=1, keepdims=True)
    p = jnp.exp(s - m)
    l = jnp.sum(p, axis=1, keepdims=True)
    o_win = jnp.dot(p.astype(bf16), vw_ref[0, pl.ds(w0, wk), :], preferred_element_type=f32) / l

    gates = g_ref[0]
    for h in range(N_HEADS):
        grp = h // GROUP
        r0, r1 = h * tq, (h + 1) * tq
        c0, c1 = grp * HEAD_DIM, (grp + 1) * HEAD_DIM
        o_h = (gates[:, 3 * h:3 * h + 1] * o_cmp[r0:r1, c0:c1]
               + gates[:, 3 * h + 1:3 * h + 2] * o_sel[r0:r1, c0:c1]
               + gates[:, 3 * h + 2:3 * h + 3] * o_win[r0:r1, c0:c1])
        o_ref[0, :, h * HEAD_DIM:(h + 1) * HEAD_DIM] = o_h


def nsa_prompt_attention(q, gates, kc, vc, ks, vs, kw, vw, rel_bias):
    B, T = q.shape[0], q.shape[1]
    tq = SEL_BLOCK
    nq = T // tq
    ns = T // SEL_BLOCK
    nc = kc.shape[1]
    ncp = -(-nc // LANES) * LANES
    bf16 = jnp.bfloat16
    scale = HEAD_DIM ** -0.5
    assert T % FAR_KEYS == 0 and T >= WIN_BLOCKS * SEL_BLOCK

    tl = np.arange(tq)[None, :, None]
    r = tq // CMP_STRIDE
    span = r * (nq - 1) + ncp
    y = np.arange(span)[None, None, :]
    dy = tl + CMP_STRIDE * (y - (ncp - 1)) - (CMP_LEN - 1)
    master = _bias_table(rel_bias, dy, dy >= 0)[0][:, :, ::-1]
    pad_col = jnp.arange(ncp) < nc
    bc = jnp.stack([jnp.where(pad_col, master[:, :, r * (nq - 1 - i):r * (nq - 1 - i) + ncp], NEG)
                    for i in range(nq)])
    bsn = _toeplitz_bias(rel_bias, NEAR_BLOCKS, tq, NEAR_BLOCKS * SEL_BLOCK, lambda d: d >= 0)
    bw = _toeplitz_bias(rel_bias, WIN_BLOCKS, tq, WIN_BLOCKS * SEL_BLOCK, lambda d: (d >= 0) & (d < WINDOW))

    ci = np.arange(ncp)[:, None]
    sj = np.arange(ns)[None, :]
    cover = ((ci * CMP_STRIDE < (sj + 1) * SEL_BLOCK) & (ci * CMP_STRIDE + CMP_LEN > sj * SEL_BLOCK)
             & (ci < nc))
    cover = jnp.asarray(cover.astype(np.float32), bf16)
    pos = np.arange(T).reshape(T // FAR_KEYS, 1, FAR_KEYS)
    expand = jnp.asarray(np.where(pos // SEL_BLOCK == np.arange(ns)[None, :, None], SEL_MASK, 0.0)
                         .astype(np.float32), bf16)

    padc = ((0, 0), (0, ncp - nc), (0, 0))
    seq = lambda b, i: (b, 0, 0)
    return pl.pallas_call(
        functools.partial(_nsa_prompt_kernel, tq=tq),
        grid=(B, nq),
        in_specs=[
            pl.BlockSpec((1, tq, ATTN_WIDTH), lambda b, i: (b, i, 0)),
            pl.BlockSpec((1, tq, 3 * N_HEADS), lambda b, i: (b, i, 0)),
            pl.BlockSpec((1, ncp, KV_WIDTH), seq),
            pl.BlockSpec((1, ncp, KV_WIDTH), seq),
            pl.BlockSpec((1, T, KV_WIDTH), seq),
            pl.BlockSpec((1, T, KV_WIDTH), seq),
            pl.BlockSpec((1, T, KV_WIDTH), seq),
            pl.BlockSpec((1, T, KV_WIDTH), seq),
            pl.BlockSpec((1, N_HEADS, tq, ncp), lambda b, i: (i, 0, 0, 0)),
            pl.BlockSpec((1, N_HEADS, tq, NEAR_BLOCKS * SEL_BLOCK),
                         lambda b, i: (jnp.minimum(i, NEAR_BLOCKS - 1), 0, 0, 0)),
            pl.BlockSpec((1, N_HEADS, tq, WIN_BLOCKS * SEL_BLOCK),
                         lambda b, i: (jnp.minimum(i, WIN_BLOCKS - 1), 0, 0, 0)),
            pl.BlockSpec((ncp, ns), lambda b, i: (0, 0)),
            pl.BlockSpec((T // FAR_KEYS, ns, FAR_KEYS), lambda b, i: (0, 0, 0)),
        ],
        out_specs=pl.BlockSpec((1, tq, ATTN_WIDTH), lambda b, i: (b, i, 0)),
        out_shape=jax.ShapeDtypeStruct((B, T, ATTN_WIDTH), jnp.float32),
        compiler_params=pltpu.CompilerParams(dimension_semantics=("arbitrary", "arbitrary"),
                                             vmem_limit_bytes=VMEM_LIMIT),
        name="nsa_prompt",
    )(q * scale, gates, jnp.pad(kc, padc).astype(bf16), jnp.pad(vc, padc).astype(bf16),
      ks.astype(bf16), vs.astype(bf16), kw.astype(bf16), vw.astype(bf16), bc, bsn, bw, cover, expand)


NEW_PAD = 16
KV_SLOTS = 3


def _softmax_parts(s):
    m = jnp.max(s, axis=1, keepdims=True)
    p = jnp.exp(s - m)
    return m, p, jnp.sum(p, axis=1, keepdims=True)


def _nsa_sample_kernel(pt_ref, q_ref, g_ref, kc_ref, vc_ref, cache_ref, nsk_ref, nsv_ref, cw_ref, nwk_ref, nwv_ref,
                       bc_ref, bsl_ref, bnew_ref, bw_ref, cover_ref, e_ref, o_ref, kv_buf, sem, *, tn, ns):
    f32, bf16 = jnp.float32, jnp.bfloat16
    nt = (((1,), (1,)), ((), ()))
    b = pl.program_id(0)
    n_pages, page = pt_ref.shape[1], cache_ref.shape[-1]
    pages_per_chunk = FAR_KEYS // page
    n_far = n_pages // pages_per_chunk
    total = pl.num_programs(0) * n_far

    def fetch(gc):
        seq, first = gc // n_far, (gc % n_far) * pages_per_chunk
        slot = gc % KV_SLOTS
        for i in range(pages_per_chunk):
            pltpu.make_async_copy(cache_ref.at[pt_ref[seq, first + i]], kv_buf.at[slot, i], sem.at[slot]).start()

    @pl.when(b == 0)
    def _():
        for gc in range(KV_SLOTS - 1):
            fetch(gc)
    qpad = q_ref[0]
    cur = ns - 1

    def heads(x):
        return jnp.concatenate([x[:tn]] * GROUP + [x[tn:]] * GROUP, axis=0)

    s = lax.dot_general(qpad, kc_ref[0], nt, preferred_element_type=f32) + bc_ref[...]
    _, e, d = _softmax_parts(s)
    pc = e / d
    pc16 = pc.astype(bf16)
    o_cmp = jnp.dot(pc16, vc_ref[0], preferred_element_type=f32)

    imp_h = jnp.dot(pc16, cover_ref[...], preferred_element_type=f32)
    imp = jnp.concatenate([_group_sum(imp_h, grp, tn) for grp in range(N_KV)], axis=0)
    j = lax.broadcasted_iota(jnp.int32, imp.shape, 1)
    forced = (j == 0) | (j == cur) | (j == cur - 1)
    imp = jnp.where(forced, imp + FORCE_SCORE, imp)
    imp = jnp.where(j > cur, -jnp.inf, imp)
    cnt = jnp.zeros(imp.shape, jnp.int32)
    for jp in range(ns):
        col = imp[:, jp:jp + 1]
        ahead = (col > imp) | ((col == imp) & (j > jp))
        cnt = cnt + ahead.astype(jnp.int32)
    sel = cnt < TOP_N
    sel_far = heads(jnp.where(sel & (j < cur), 0.0, -1.0)).astype(bf16)

    bnew = bnew_ref[...]
    m, p, l = _softmax_parts(lax.dot_general(qpad, nsk_ref[0], nt, preferred_element_type=f32) + bnew)
    acc = jnp.dot(p.astype(bf16), nsv_ref[0], preferred_element_type=f32)
    half = GROUP * tn
    q_g = [qpad[grp * half:(grp + 1) * half, grp * HEAD_DIM:(grp + 1) * HEAD_DIM] for grp in range(N_KV)]
    zero = jnp.zeros((half, HEAD_DIM), f32)

    def far_chunk(c, carry, bias):
        m, l, acc = carry
        gc = b * n_far + c
        slot = gc % KV_SLOTS
        pltpu.make_async_copy(kv_buf.at[slot], kv_buf.at[slot], sem.at[slot]).wait()

        @pl.when(gc + KV_SLOTS - 1 < total)
        def _():
            fetch(gc + KV_SLOTS - 1)

        def plane(kv, grp):
            return jnp.concatenate([kv_buf[slot, i, kv, grp] for i in range(pages_per_chunk)], axis=1).astype(bf16)

        s = jnp.concatenate([jnp.dot(q_g[grp], plane(0, grp), preferred_element_type=f32)
                             for grp in range(N_KV)], axis=0)
        s = s + jnp.dot(sel_far, e_ref[c], preferred_element_type=f32)
        if bias is not None:
            s = s + bias
        m_new = jnp.maximum(m, jnp.max(s, axis=1, keepdims=True))
        a = jnp.exp(m - m_new)
        p = jnp.exp(s - m_new)
        l = a * l + jnp.sum(p, axis=1, keepdims=True)
        p = p.astype(bf16)
        pv = [lax.dot_general(p[grp * half:(grp + 1) * half], plane(1, grp), nt, preferred_element_type=f32)
              for grp in range(N_KV)]
        pv = jnp.concatenate([jnp.concatenate([pv[0], zero], axis=1),
                              jnp.concatenate([zero, pv[1]], axis=1)], axis=0)
        return m_new, l, a * acc + pv

    carry = lax.fori_loop(0, n_far - 1, lambda c, carry: far_chunk(c, carry, None), (m, l, acc))
    m, l, acc = far_chunk(n_far - 1, carry, bsl_ref[...])
    o_sel = acc / l

    cw = cw_ref[0]
    s_c = lax.dot_general(qpad, cw[:, :KV_WIDTH].astype(bf16), nt, preferred_element_type=f32) + bw_ref[...]
    s_n = lax.dot_general(qpad, nwk_ref[0], nt, preferred_element_type=f32) + bnew
    m = jnp.maximum(jnp.max(s_c, axis=1, keepdims=True), jnp.max(s_n, axis=1, keepdims=True))
    p_c = jnp.exp(s_c - m)
    p_n = jnp.exp(s_n - m)
    l = jnp.sum(p_c, axis=1, keepdims=True) + jnp.sum(p_n, axis=1, keepdims=True)
    o_win = (jnp.dot(p_c.astype(bf16), cw[:, KV_WIDTH:].astype(bf16), preferred_element_type=f32)
             + jnp.dot(p_n.astype(bf16), nwv_ref[0], preferred_element_type=f32)) / l

    g = g_ref[0]
    o_ref[0] = g[:, 0:1] * o_cmp + g[:, 1:2] * o_sel + g[:, 2:3] * o_win


def nsa_sample_attention(q, gates, kc, vc, cache_sel, page_table, new_sel, cache_win, new_win, rel_bias):
    B, tn = q.shape[0], q.shape[1]
    past, wk, nc = page_table.shape[1] * cache_sel.shape[-1], cache_win.shape[1], kc.shape[1]
    assert past % FAR_KEYS == 0 and past % SEL_BLOCK == 0 and tn <= NEW_PAD and wk == WINDOW
    ns = past // SEL_BLOCK + 1
    nsp = -(-ns // LANES) * LANES
    ncp = -(-nc // LANES) * LANES
    rows = N_HEADS * tn
    bf16 = jnp.bfloat16
    n_far = past // FAR_KEYS

    tl = np.arange(tn)[:, None]
    flat = lambda t: t[0].reshape(rows, t.shape[-1])
    cc = np.arange(ncp)[None, :]
    dc = past + tl - (cc * CMP_STRIDE + CMP_LEN - 1)
    bc = flat(_bias_table(rel_bias, dc[None], ((dc >= 0) & (cc < nc))[None]))
    dl = tl + FAR_KEYS - np.arange(FAR_KEYS)[None, :]
    bsl = flat(_bias_table(rel_bias, dl[None], (dl >= 0)[None]))
    jn = np.arange(NEW_PAD)[None, :]
    dn = tl - jn
    bnew = flat(_bias_table(rel_bias, dn[None], ((dn >= 0) & (jn < tn))[None]))
    dw = wk + tl - np.arange(wk)[None, :]
    bw = flat(_bias_table(rel_bias, dw[None], ((dw >= 0) & (dw < WINDOW))[None]))

    ci = np.arange(ncp)[:, None]
    sj = np.arange(nsp)[None, :]
    cover = ((ci * CMP_STRIDE < (sj + 1) * SEL_BLOCK) & (ci * CMP_STRIDE + CMP_LEN > sj * SEL_BLOCK)
             & (ci < nc) & (sj < ns))
    cover = jnp.asarray(cover.astype(np.float32), bf16)
    pos = np.arange(past).reshape(n_far, 1, FAR_KEYS)
    expand = jnp.asarray(np.where(pos // SEL_BLOCK == np.arange(nsp)[None, :, None], SEL_MASK, 0.0)
                         .astype(np.float32), bf16)

    qh = jnp.transpose((q * HEAD_DIM ** -0.5).reshape(B, tn, N_HEADS, HEAD_DIM), (0, 2, 1, 3))
    zero = jnp.zeros_like(qh)
    grp = (np.arange(N_HEADS) // GROUP)[None, :, None, None]
    qpad = jnp.where(grp == 0, jnp.concatenate([qh, zero], -1), jnp.concatenate([zero, qh], -1))
    qpad = qpad.reshape(B, rows, KV_WIDTH).astype(bf16)
    grow = jnp.transpose(gates.reshape(B, tn, N_HEADS, 3), (0, 2, 1, 3)).reshape(B, rows, 3)
    padn = lambda a: jnp.pad(a, ((0, 0), (0, NEW_PAD - tn), (0, 0))).astype(bf16)
    padc = ((0, 0), (0, ncp - nc), (0, 0))

    seq3 = lambda b, pt: (b, 0, 0)
    const2 = lambda b, pt: (0, 0)
    grid_spec = pltpu.PrefetchScalarGridSpec(
        num_scalar_prefetch=1,
        grid=(B,),
        in_specs=[
            pl.BlockSpec((1, rows, KV_WIDTH), seq3),
            pl.BlockSpec((1, rows, 3), seq3),
            pl.BlockSpec((1, ncp, KV_WIDTH), seq3),
            pl.BlockSpec((1, ncp, KV_WIDTH), seq3),
            pl.BlockSpec(memory_space=pl.ANY),
            pl.BlockSpec((1, NEW_PAD, KV_WIDTH), seq3),
            pl.BlockSpec((1, NEW_PAD, KV_WIDTH), seq3),
            pl.BlockSpec((1, wk, 2 * KV_WIDTH), seq3),
            pl.BlockSpec((1, NEW_PAD, KV_WIDTH), seq3),
            pl.BlockSpec((1, NEW_PAD, KV_WIDTH), seq3),
            pl.BlockSpec((rows, ncp), const2),
            pl.BlockSpec((rows, FAR_KEYS), const2),
            pl.BlockSpec((rows, NEW_PAD), const2),
            pl.BlockSpec((rows, wk), const2),
            pl.BlockSpec((ncp, nsp), const2),
            pl.BlockSpec((n_far, nsp, FAR_KEYS), lambda b, pt: (0, 0, 0)),
        ],
        out_specs=pl.BlockSpec((1, rows, KV_WIDTH), seq3),
        scratch_shapes=[pltpu.VMEM((KV_SLOTS, FAR_KEYS // cache_sel.shape[-1]) + cache_sel.shape[1:], jnp.float32),
                        pltpu.SemaphoreType.DMA((KV_SLOTS,))],
    )
    out = pl.pallas_call(
        functools.partial(_nsa_sample_kernel, tn=tn, ns=ns),
        grid_spec=grid_spec,
        out_shape=jax.ShapeDtypeStruct((B, rows, KV_WIDTH), jnp.float32),
        compiler_params=pltpu.CompilerParams(dimension_semantics=("arbitrary",),
                                             vmem_limit_bytes=VMEM_LIMIT),
        name="nsa_sample",
    )(page_table, qpad, grow, jnp.pad(kc, padc).astype(bf16), jnp.pad(vc, padc).astype(bf16), cache_sel,
      padn(new_sel[:, :, :KV_WIDTH]), padn(new_sel[:, :, KV_WIDTH:]), cache_win,
      padn(new_win[:, :, :KV_WIDTH]), padn(new_win[:, :, KV_WIDTH:]), bc, bsl, bnew, bw, cover, expand)
    o = out.reshape(B, N_HEADS, tn, 2, HEAD_DIM)
    o = jnp.where(grp[..., None] == 0, o[:, :, :, 0:1], o[:, :, :, 1:2])[:, :, :, 0]
    return jnp.transpose(o, (0, 2, 1, 3)).reshape(B, tn, ATTN_WIDTH)


N_SUB = 2 * PEER_HEADS
HALF_DQ = PEER_DQ // 2
N_PAIRS = PEER_HEADS * PEER_TOPK


def _top_rounds(s, row, n):
    width = s.shape[0]
    k = lax.broadcasted_iota(jnp.int32, (n, s.shape[1]), 0)
    vals = jnp.zeros((n, s.shape[1]), jnp.float32)
    ids = jnp.zeros((n, s.shape[1]), jnp.float32)
    for a in range(n):
        m = jnp.max(s, axis=0, keepdims=True)
        ix = jnp.min(jnp.where(s == m, row, float(width)), axis=0, keepdims=True)
        vals = jnp.where(k == a, m, vals)
        ids = jnp.where(k == a, ix, ids)
        s = jnp.where(row == ix, -jnp.inf, s)
    return vals, ids


def _peer_route_kernel(h_ref, wq_ref, keys_ref, idx_ref, g_ref, q_scr, e_scr, w_scr):
    f32, bf16 = jnp.float32, jnp.bfloat16
    tm = h_ref.shape[0]
    nt = (((1,), (1,)), ((), ()))
    q = jnp.dot(h_ref[...].astype(bf16), wq_ref[...], preferred_element_type=f32)
    for k in range(N_SUB):
        q_scr[k] = q[:, k * HALF_DQ:(k + 1) * HALF_DQ].astype(bf16)
    row = lax.broadcasted_iota(jnp.int32, (N_KEYS, tm), 0).astype(f32)
    cand_blocks, a = [], 0
    while PEER_TOPK // (a + 1) > 1:
        cand_blocks.append((a, a + 1, min(-(-(PEER_TOPK // (a + 1)) // 8) * 8, PEER_TOPK)))
        a += 1
    cand_blocks.append((a, PEER_TOPK, 1))
    row2 = jnp.concatenate(
        [(lax.broadcasted_iota(jnp.int32, (max(a1 - a0, nb), tm), 0) * (1 if nb > 1 else PEER_TOPK)
          + a0 * PEER_TOPK).astype(f32) for a0, a1, nb in cand_blocks], axis=0)
    n_cand = PEER_TOPK * PEER_TOPK

    def head(h, carry):
        s1, i1 = _top_rounds(lax.dot_general(keys_ref[2 * h], q_scr[2 * h], nt, preferred_element_type=f32),
                             row, PEER_TOPK)
        s2, i2 = _top_rounds(lax.dot_general(keys_ref[2 * h + 1], q_scr[2 * h + 1], nt,
                                             preferred_element_type=f32), row, PEER_TOPK)
        cand = jnp.concatenate([s1[a0:a1] + s2[:nb] for a0, a1, nb in cand_blocks], axis=0)
        cidx = jnp.concatenate([i1[a0:a1] * float(N_KEYS) + i2[:nb] for a0, a1, nb in cand_blocks], axis=0)
        k = lax.broadcasted_iota(jnp.int32, (PEER_TOPK, tm), 0)
        top = jnp.zeros((PEER_TOPK, tm), f32)
        eid = jnp.zeros((PEER_TOPK, tm), f32)
        for a in range(PEER_TOPK):
            m = jnp.max(cand, axis=0, keepdims=True)
            pos = jnp.min(jnp.where(cand == m, row2, float(n_cand)), axis=0, keepdims=True)
            hit = row2 == pos
            e = jnp.max(jnp.where(hit, cidx, -1.0), axis=0, keepdims=True)
            top = jnp.where(k == a, m, top)
            eid = jnp.where(k == a, e, eid)
            cand = jnp.where(hit, -jnp.inf, cand)
        ex = jnp.exp(top - top[0:1])
        r0 = pl.multiple_of(h * PEER_TOPK, PEER_TOPK)
        e_scr[pl.ds(r0, PEER_TOPK), :] = eid
        w_scr[pl.ds(r0, PEER_TOPK), :] = ex / jnp.sum(ex, axis=0, keepdims=True)
        return carry

    lax.fori_loop(0, PEER_HEADS, head, 0)
    idx_ref[...] = e_scr[...].T.astype(jnp.int32)
    g_ref[...] = w_scr[...].T


def peer_route(h2, peer_q, peer_keys, tm=128):
    n = h2.shape[0]
    tm = min(tm, n)
    bf16 = jnp.bfloat16
    keys = peer_keys.reshape(N_SUB, N_KEYS, HALF_DQ).astype(bf16)
    return pl.pallas_call(
        _peer_route_kernel,
        grid=(n // tm,),
        in_specs=[pl.BlockSpec((tm, D_MODEL), lambda i: (i, 0)),
                  pl.BlockSpec((D_MODEL, PEER_HEADS * PEER_DQ), lambda i: (0, 0)),
                  pl.BlockSpec((N_SUB, N_KEYS, HALF_DQ), lambda i: (0, 0, 0))],
        out_specs=[pl.BlockSpec((tm, N_PAIRS), lambda i: (i, 0)),
                   pl.BlockSpec((tm, N_PAIRS), lambda i: (i, 0))],
        out_shape=[jax.ShapeDtypeStruct((n, N_PAIRS), jnp.int32),
                   jax.ShapeDtypeStruct((n, N_PAIRS), jnp.float32)],
        scratch_shapes=[pltpu.VMEM((N_SUB, tm, HALF_DQ), bf16),
                        pltpu.VMEM((N_PAIRS, tm), jnp.float32),
                        pltpu.VMEM((N_PAIRS, tm), jnp.float32)],
        compiler_params=pltpu.CompilerParams(dimension_semantics=("arbitrary",),
                                             vmem_limit_bytes=VMEM_LIMIT),
        name="peer_route",
    )(h2, peer_q.astype(bf16), keys)


def _gelu_tanh(x):
    return 0.5 * x * (1.0 + jnp.tanh(math.sqrt(2.0 / math.pi) * (x + 0.044715 * (x * x * x))))


N_SLOTS = 8
DMA_QUEUES = 2
ROW_TILES = 2 * D_MODEL // LANES
U_TILES = D_MODEL // LANES


def _peer_expert_kernel(idx_ref, nxt_ref, x_ref, g_ref, uv_ref, y_ref, *scratch):
    tb = x_ref.shape[0]
    i = pl.program_id(0)
    last = pl.num_programs(0) - 1
    ahead = N_SLOTS - 1
    bufs, sem = scratch[:N_SLOTS], scratch[N_SLOTS]

    def issue(ids, t, slot):
        for k in range(N_PAIRS):
            pltpu.make_async_copy(uv_ref.at[ids[t, k]], bufs[slot].at[k // 8, :, k % 8, :],
                                  sem.at[slot]).start(priority=k % DMA_QUEUES)

    def wait_slot(slot):
        pltpu.make_async_copy(bufs[slot], bufs[slot], sem.at[slot]).wait()

    eye = (lax.broadcasted_iota(jnp.int32, (N_PAIRS, N_PAIRS), 0)
           == lax.broadcasted_iota(jnp.int32, (N_PAIRS, N_PAIRS), 1))

    def compute(t, slot):
        buf = bufs[slot]

        def chunk(c):
            return buf[:, c].reshape(N_PAIRS, LANES)

        x = x_ref[t]
        acc = chunk(0) * x[0:1, :]
        for s in range(1, U_TILES):
            acc = acc + chunk(s) * x[s:s + 1, :]
        a = jnp.sum(acc, axis=1, keepdims=True)
        g_col = jnp.sum(jnp.where(eye, g_ref[pl.ds(t, 1), :], 0.0), axis=1, keepdims=True)
        w = g_col * _gelu_tanh(a)
        y = [jnp.sum(w * chunk(U_TILES + s), axis=0, keepdims=True) for s in range(U_TILES)]
        y_ref[pl.ds(t, 1), :] = jnp.concatenate(y, axis=1)

    @pl.when(i == 0)
    def _():
        for t in range(ahead):
            issue(idx_ref, t, t)

    def group(j, c):
        for u in range(N_SLOTS):
            t = j * N_SLOTS + u
            wait_slot(u)
            issue(idx_ref, t + ahead, (u + ahead) % N_SLOTS)
            compute(t, u)
        return c

    n_groups = tb // N_SLOTS
    lax.fori_loop(0, n_groups - 1, group, 0)
    for u in range(N_SLOTS):
        t = tb - N_SLOTS + u
        wait_slot(u)
        if u == 0:
            issue(idx_ref, t + ahead, ahead)
        else:
            issue(nxt_ref, u - 1, u - 1)
        compute(t, u)

    @pl.when(i == last)
    def _():
        for u in range(ahead):
            wait_slot(u)


def peer_experts(h2, eidx, gate, expert_u, expert_v, tb=64):
    n = h2.shape[0]
    tb = min(tb, n)
    nb = n // tb
    e = expert_u.shape[0]
    uv = jnp.concatenate([expert_u.reshape(e, U_TILES, LANES), expert_v.reshape(e, U_TILES, LANES)], axis=1)
    return pl.pallas_call(
        _peer_expert_kernel,
        grid=(nb,),
        in_specs=[pl.BlockSpec((tb, N_PAIRS), lambda i: (i, 0), memory_space=pltpu.SMEM),
                  pl.BlockSpec((tb, N_PAIRS), lambda i: (jnp.minimum(i + 1, nb - 1), 0), memory_space=pltpu.SMEM),
                  pl.BlockSpec((tb, U_TILES, LANES), lambda i: (i, 0, 0)),
                  pl.BlockSpec((tb, N_PAIRS), lambda i: (i, 0)),
                  pl.BlockSpec(memory_space=pl.ANY)],
        out_specs=pl.BlockSpec((tb, D_MODEL), lambda i: (i, 0)),
        out_shape=jax.ShapeDtypeStruct((n, D_MODEL), jnp.float32),
        scratch_shapes=[pltpu.VMEM((N_PAIRS // 8, ROW_TILES, 8, LANES), jnp.float32)] * N_SLOTS
        + [pltpu.SemaphoreType.DMA((N_SLOTS,))],
        compiler_params=pltpu.CompilerParams(dimension_semantics=("arbitrary",),
                                             vmem_limit_bytes=VMEM_LIMIT),
        name="peer_experts",
    )(eidx, eidx, h2.reshape(n, U_TILES, LANES), gate, uv)


def rms_norm(x, g):
    y = x * lax.rsqrt(jnp.mean(x * x, axis=-1, keepdims=True) + EPS)
    return y * g


def layer_norm(x, g, b):
    mu = jnp.mean(x, axis=-1, keepdims=True)
    var = jnp.mean(jnp.square(x - mu), axis=-1, keepdims=True)
    return (x - mu) * lax.rsqrt(var + EPS) * g + b


def masked_softmax(logits, mask, axis):
    neg = jnp.where(mask, logits, -jnp.inf)
    m = jnp.max(neg, axis=axis, keepdims=True)
    m = jnp.where(jnp.isfinite(m), m, 0.0)
    e = jnp.where(mask, jnp.exp(neg - m), 0.0)
    d = jnp.sum(e, axis=axis, keepdims=True)
    return e / jnp.where(d > 0.0, d, 1.0)


def t5_bucket(dist):
    n = jnp.maximum(dist, 0)
    exact = N_BUCKETS // 2
    nf = jnp.maximum(n, 1).astype(jnp.float32)
    large = exact + (jnp.log(nf / exact) / math.log(MAX_DISTANCE / exact) * (N_BUCKETS - exact)).astype(jnp.int32)
    return jnp.where(n < exact, n, jnp.minimum(large, N_BUCKETS - 1))


def compress(rows, pe, w1, b1, w2, b2):
    B, L = rows.shape[0], rows.shape[1]
    ratio = CMP_LEN // CMP_STRIDE
    n_chunks = L // CMP_STRIDE
    nc = n_chunks - ratio + 1
    r = rows[:, :n_chunks * CMP_STRIDE].reshape(B, n_chunks, CMP_STRIDE, N_KV, HEAD_DIM)
    w1c = w1.reshape(ratio, CMP_STRIDE, HEAD_DIM, CMP_HIDDEN)
    parts = jnp.einsum('bcsgd,jsdh->jbcgh', r, w1c)
    hid = b1 + pe.reshape(-1) @ w1
    for j in range(ratio):
        hid = hid + parts[j, :, j:j + nc]
    out = jax.nn.gelu(hid) @ w2 + b2
    end = jnp.arange(nc, dtype=jnp.int32) * CMP_STRIDE + (CMP_LEN - 1)
    return out, end


def compress_branch(rows, lp, which):
    return compress(rows, lp['cmp_pe'][which], lp['cmp_w1'][which], lp['cmp_b1'][which],
                    lp['cmp_w2'][which], lp['cmp_b2'][which])


def to_blocks(rows):
    B, L = rows.shape[0], rows.shape[1]
    ns = -(-L // SEL_BLOCK)
    rows = jnp.pad(rows, ((0, 0), (0, ns * SEL_BLOCK - L), (0, 0), (0, 0)))
    return jnp.transpose(rows.reshape(B, ns, SEL_BLOCK, N_KV, HEAD_DIM), (0, 3, 1, 2, 4))


def nsa_block(q, t_pos, gates, kc, vc, kc_end, ks_blk, vs_blk, kw, vw, w_pos, rel_bias):
    B = q.shape[0]
    scale = HEAD_DIM ** -0.5
    tbl = rel_bias.astype(jnp.float32).reshape(N_BUCKETS, N_KV, GROUP)
    lc = jnp.einsum('bqgrd,bcgd->bqgrc', q, kc).astype(jnp.float32) * scale
    dc = t_pos[:, None] - kc_end[None, :]
    lc = lc + jnp.transpose(tbl[t5_bucket(dc)], (0, 2, 3, 1))[None]
    pc = masked_softmax(lc, (dc >= 0)[None, :, None, None, :], axis=-1)
    o_cmp = jnp.einsum('bqgrc,bcgd->bqgrd', pc.astype(vc.dtype), vc)
    nc, ns = kc.shape[1], ks_blk.shape[2]
    ci = jnp.arange(nc)[:, None]
    sj = jnp.arange(ns)[None, :]
    cover = ((ci * CMP_STRIDE < (sj + 1) * SEL_BLOCK) & (ci * CMP_STRIDE + CMP_LEN > sj * SEL_BLOCK)).astype(jnp.float32)
    imp = jnp.einsum('bqgrc,cj->bqgj', pc, cover)
    cur = (t_pos // SEL_BLOCK)[:, None]
    forced = (sj == 0) | (sj == cur) | (sj == cur - 1)
    valid = sj * SEL_BLOCK <= t_pos[:, None]
    imp = jnp.where(forced[None, :, None, :], imp + FORCE_SCORE, imp)
    imp = jnp.where(valid[None, :, None, :], imp, -FORCE_SCORE)
    _, idx = lax.top_k(imp, min(TOP_N, ns))
    bi = jnp.arange(B)[:, None, None, None]
    gi = jnp.arange(N_KV)[None, None, :, None]
    ks_g = ks_blk[bi, gi, idx]
    vs_g = vs_blk[bi, gi, idx]
    ls = jnp.einsum('bqgrd,bqgnkd->bqgrnk', q, ks_g).astype(jnp.float32) * scale
    pos = idx[..., None] * SEL_BLOCK + jnp.arange(SEL_BLOCK, dtype=jnp.int32)
    ds = t_pos[None, :, None, None, None] - pos
    tblk = jnp.transpose(tbl, (1, 0, 2))
    bs = tblk[jnp.arange(N_KV)[None, None, :, None, None], t5_bucket(ds)]
    ls = ls + jnp.moveaxis(bs, -1, 3)
    ps = masked_softmax(ls, (ds >= 0)[:, :, :, None], axis=(-2, -1))
    o_sel = jnp.einsum('bqgrnk,bqgnkd->bqgrd', ps.astype(vs_g.dtype), vs_g)
    lw = jnp.einsum('bqgrd,bwgd->bqgrw', q, kw).astype(jnp.float32) * scale
    dw = t_pos[:, None] - w_pos[None, :]
    lw = lw + jnp.transpose(tbl[t5_bucket(dw)], (0, 2, 3, 1))[None]
    mw = (dw >= 0) & (dw < WINDOW) & (w_pos >= 0)[None, :]
    pw = masked_softmax(lw, mw[None, :, None, None, :], axis=-1)
    o_win = jnp.einsum('bqgrw,bwgd->bqgrd', pw.astype(vw.dtype), vw)
    return gates[..., 0:1] * o_cmp + gates[..., 1:2] * o_sel + gates[..., 2:3] * o_win


def nsa_prompt(q, gates, kc_raw, vc_raw, ks, vs, kw, vw, lp, rel_bias):
    B, T = q.shape[0], q.shape[1]
    kc, kc_end = compress_branch(kc_raw, lp, 0)
    kc = rms_norm(kc, lp['k_norm'][0])
    vc, _ = compress_branch(vc_raw, lp, 1)
    flat = lambda a: a.reshape(B, a.shape[1], -1)
    o = nsa_prompt_attention(flat(q), flat(gates), flat(kc), flat(vc), flat(ks), flat(vs), flat(kw), flat(vw),
                             rel_bias)
    new_win = jnp.stack([kw, vw], axis=2)[:, T - min(WINDOW, T):]
    return o, (jnp.stack([kc_raw, vc_raw], axis=2), jnp.stack([ks, vs], axis=2), new_win)


def nsa_sample(q, gates, kc_raw, vc_raw, ks, vs, kw, vw, cache_cmp, cache_sel, cache_win, page_table, lp, rel_bias):
    Bd, Tn = q.shape[0], q.shape[1]
    past = page_table.shape[1] * cache_cmp.shape[1]
    t_pos = past + jnp.arange(Tn, dtype=jnp.int32)
    past_cmp = cache_cmp[page_table].reshape(Bd, past, 2, N_KV, HEAD_DIM)
    kc, kc_end = compress_branch(jnp.concatenate([past_cmp[:, :, 0], kc_raw], axis=1), lp, 0)
    kc = rms_norm(kc, lp['k_norm'][0])
    vc, _ = compress_branch(jnp.concatenate([past_cmp[:, :, 1], vc_raw], axis=1), lp, 1)
    wk = cache_win.shape[1]
    flat = lambda a: a.reshape(Bd, a.shape[1], -1)
    new_sel = jnp.concatenate([flat(ks), flat(vs)], axis=-1)
    new_wkv = jnp.concatenate([flat(kw), flat(vw)], axis=-1)
    o = nsa_sample_attention(flat(q), flat(gates), flat(kc), flat(vc), jnp.transpose(cache_sel, (0, 2, 3, 4, 1)),
                             page_table, new_sel, flat(cache_win), new_wkv, rel_bias)
    win = jnp.concatenate([cache_win, jnp.stack([kw, vw], axis=2)], axis=1)
    new_win = win[:, wk + Tn - min(WINDOW, wk + Tn):]
    return o.reshape(Bd, Tn, ATTN_WIDTH), (jnp.stack([kc_raw, vc_raw], axis=2), jnp.stack([ks, vs], axis=2), new_win)


def conformer_conv(glu_in, hist, lp):
    u = glu_in[..., :CONV_WIDTH] * jax.nn.sigmoid(glu_in[..., CONV_WIDTH:])
    uh = jnp.concatenate([hist.astype(u.dtype), u], axis=1)
    y = lax.conv_general_dilated(uh, lp['conv_w'][:, None, :], window_strides=(1,), padding='VALID',
                                 dimension_numbers=('NWC', 'WIO', 'NWC'), feature_group_count=CONV_WIDTH)
    y = jax.nn.silu(layer_norm(y + lp['conv_b'], lp['conv_ln_g'], lp['conv_ln_b']))
    return y, uh[:, uh.shape[1] - (CONV_K - 1):]


def peer(h, lp):
    shp = h.shape
    xt = h.reshape(-1, D_MODEL)
    n = xt.shape[0]
    eidx, gate = peer_route(xt, lp['peer_q'], lp['peer_keys'])
    y = peer_experts(xt, eidx, gate, lp['expert_u'], lp['expert_v'])
    return y.reshape(shp)


def split_points():
    pts, acc = [], 0
    for s in SPLIT_SIZES[:-1]:
        acc += s
        pts.append(acc)
    return pts


def layer(x, c, conv_hist, lp, rel_bias, cache):
    B, T = x.shape[0], x.shape[1]
    mod = (jax.nn.silu(c) @ lp['w_ada'] + lp['b_ada'])[:, None, :]
    sh1, sc1, gt1, sh2, sc2, gt2 = jnp.split(mod, 6, axis=-1)
    h = rms_norm(x, lp['norm1']) * (1 + sc1) + sh1
    z = matmul(h.reshape(B * T, D_MODEL), lp['w_in']).reshape(B, T, IN_COLS)
    zq, zkc, zvc, zks, zvs, zkw, zvw, zg, zglu = jnp.split(z, split_points(), axis=-1)

    def kv(z):
        return z.reshape(B, T, N_KV, HEAD_DIM)

    q = rms_norm(zq.reshape(B, T, N_KV, GROUP, HEAD_DIM), lp['q_norm'])
    ks = rms_norm(kv(zks), lp['k_norm'][1])
    kw = rms_norm(kv(zkw), lp['k_norm'][2])
    gates = jax.nn.sigmoid(zg.reshape(B, T, N_KV, GROUP, 3))
    if cache is None:
        o_attn, attn_state = nsa_prompt(q, gates, kv(zkc), kv(zvc), ks, kv(zvs), kw, kv(zvw), lp, rel_bias)
    else:
        cache_cmp, cache_sel, cache_win, page_table = cache
        o_attn, attn_state = nsa_sample(q, gates, kv(zkc), kv(zvc), ks, kv(zvs), kw, kv(zvw),
                                        cache_cmp, cache_sel, cache_win, page_table, lp, rel_bias)
    o_conv, conv_state = conformer_conv(zglu, conv_hist, lp)
    cat = jnp.concatenate([rms_norm(o_attn, lp['attn_out_norm']), o_conv * lp['conv_out_scale']], axis=-1)
    mix = matmul(cat.reshape(B * T, D_MODEL), lp['w_out']).reshape(B, T, D_MODEL)
    x = x + gt1 * mix
    h2 = rms_norm(x, lp['norm2']) * (1 + sc2) + sh2
    x = x + gt2 * peer(h2, lp)
    return x, attn_state, conv_state


def kernel(x_prompt, x_sample, c_prompt, c_sample, cache_cmp_kv, cache_sel_kv, cache_win_kv, state_conv, page_table, rel_bias, w_ada, b_ada, norm1, w_in, q_norm, k_norm, cmp_pe, cmp_w1, cmp_b1, cmp_w2, cmp_b2, conv_w, conv_b, conv_ln_g, conv_ln_b, attn_out_norm, conv_out_scale, w_out, norm2, peer_q, peer_keys, expert_u, expert_v):
    lp = {
        'w_ada': w_ada[0], 'b_ada': b_ada[0], 'norm1': norm1[0], 'w_in': w_in[0],
        'q_norm': q_norm[0], 'k_norm': k_norm[0], 'cmp_pe': cmp_pe[0], 'cmp_w1': cmp_w1[0],
        'cmp_b1': cmp_b1[0], 'cmp_w2': cmp_w2[0], 'cmp_b2': cmp_b2[0], 'conv_w': conv_w[0],
        'conv_b': conv_b[0], 'conv_ln_g': conv_ln_g[0], 'conv_ln_b': conv_ln_b[0],
        'attn_out_norm': attn_out_norm[0], 'conv_out_scale': conv_out_scale[0], 'w_out': w_out[0],
        'norm2': norm2[0], 'peer_q': peer_q[0], 'peer_keys': peer_keys[0],
        'expert_u': expert_u[0], 'expert_v': expert_v[0],
    }
    hist0 = jnp.zeros((x_prompt.shape[0], CONV_K - 1, CONV_WIDTH), x_prompt.dtype)
    y_prompt, (a_c, a_s, a_w), cv = layer(x_prompt, c_prompt, hist0, lp, rel_bias, None)
    y_sample, (b_c, b_s, b_w), cvs = layer(x_sample, c_sample, state_conv[0], lp, rel_bias,
                                           (cache_cmp_kv[0], cache_sel_kv[0], cache_win_kv[0], page_table))
    return (y_prompt, y_sample, a_c[None], a_s[None], a_w[None], cv[None],
            b_c[None], b_s[None], b_w[None], cvs[None])
```

```python
import functools
import math

import jax
import jax.numpy as jnp
import numpy as np
from jax import lax
from jax.experimental import pallas as pl
from jax.experimental.pallas import tpu as pltpu

D_MODEL = 1024
HEAD_DIM = 64
ATTN_WIDTH = D_MODEL // 2
N_HEADS = ATTN_WIDTH // HEAD_DIM
N_KV = 2
GROUP = N_HEADS // N_KV
KV_WIDTH = N_KV * HEAD_DIM
CONV_WIDTH = D_MODEL - ATTN_WIDTH
CONV_K = 31
CMP_LEN = 32
CMP_STRIDE = 16
CMP_HIDDEN = 256
SEL_BLOCK = 64
TOP_N = 16
WINDOW = 512
N_BUCKETS = 32
MAX_DISTANCE = 128
Q_BLOCK = 64
PEER_HEADS = 8
PEER_DQ = 256
N_KEYS = 128
PEER_TOPK = 16
PEER_BLOCK = 256
FORCE_SCORE = 1.0e4
EPS = 1e-6
SPLIT_SIZES = (ATTN_WIDTH, KV_WIDTH, KV_WIDTH, KV_WIDTH, KV_WIDTH, KV_WIDTH, KV_WIDTH, 3 * N_HEADS, 2 * CONV_WIDTH)
IN_COLS = sum(SPLIT_SIZES)

LANES = 128
VMEM_LIMIT = 48 * 1024 * 1024


def _matmul_kernel(x_ref, w_ref, o_ref):
    o_ref[...] = jnp.dot(x_ref[...].astype(jnp.bfloat16), w_ref[...],
                         preferred_element_type=jnp.float32)


def matmul(x, w, tm=512):
    m, k = x.shape
    n = w.shape[1]
    n_pad = -(-n // LANES) * LANES
    wb = jnp.pad(w, ((0, 0), (0, n_pad - n))).astype(jnp.bfloat16)
    tm = min(tm, m)
    out = pl.pallas_call(
        _matmul_kernel,
        grid=(m // tm,),
        in_specs=[pl.BlockSpec((tm, k), lambda i: (i, 0)),
                  pl.BlockSpec((k, n_pad), lambda i: (0, 0))],
        out_specs=pl.BlockSpec((tm, n_pad), lambda i: (i, 0)),
        out_shape=jax.ShapeDtypeStruct((m, n_pad), jnp.float32),
        compiler_params=pltpu.CompilerParams(dimension_semantics=("arbitrary",),
                                             vmem_limit_bytes=VMEM_LIMIT),
        name="matmul",
    )(x, wb)
    return out[:, :n]


NEG = -1.0e30
SEL_MASK = 16384.0
FAR_KEYS = 512
NEAR_BLOCKS = 3
WIN_BLOCKS = WINDOW // SEL_BLOCK + 1


def _bucket_np(d):
    n = np.maximum(d, 0)
    exact = N_BUCKETS // 2
    nf = np.maximum(n, 1).astype(np.float32)
    large = exact + (np.log(nf / np.float32(exact)) / np.float32(math.log(MAX_DISTANCE / exact))
                     * np.float32(N_BUCKETS - exact)).astype(np.int32)
    return np.where(n < exact, n, np.minimum(large, N_BUCKETS - 1))


def _bias_table(rel_bias, dist, valid):
    tbl = rel_bias.astype(jnp.float32) - rel_bias[N_BUCKETS - 1].astype(jnp.float32)[None, :]
    b = jnp.transpose(tbl[_bucket_np(dist)], (0, 3, 1, 2))
    return jnp.where(valid[:, None], b, NEG)


def _toeplitz_bias(rel_bias, n_var, tq, m, valid_fn):
    n = n_var * tq
    period = n + m
    k = np.arange(period)
    d = np.where(k < m, -k, period - k)
    g = _bias_table(rel_bias, d[None, None, :], valid_fn(d)[None, None, :])[0, :, 0, :]
    t = jnp.tile(g, (1, n))[:, :n * (period - 1)].reshape(N_HEADS, n, period - 1)[:, :, :m]
    return jnp.transpose(t.reshape(N_HEADS, n_var, tq, m), (1, 0, 2, 3))


def _group_sum(x, grp, t):
    acc = x[grp * GROUP * t:(grp * GROUP + 1) * t]
    for r in range(1, GROUP):
        acc = acc + x[(grp * GROUP + r) * t:(grp * GROUP + r + 1) * t]
    return acc


def _nsa_prompt_kernel(q_ref, g_ref, kc_ref, vc_ref, ks_ref, vs_ref, kw_ref, vw_ref,
                       bc_ref, bsn_ref, bw_ref, cover_ref, e_ref, o_ref, *, tq):
    f32, bf16 = jnp.float32, jnp.bfloat16
    i = pl.program_id(1)
    rows = N_HEADS * tq
    nt = (((1,), (1,)), ((), ()))

    q = q_ref[0]
    lane_half = lax.broadcasted_iota(jnp.int32, (tq, 2 * HEAD_DIM), 1) // HEAD_DIM
    parts = []
    for h in range(N_HEADS):
        blk = q[:, (h // 2) * 2 * HEAD_DIM:(h // 2 + 1) * 2 * HEAD_DIM]
        grp = h // GROUP
        if h % 2 != grp:
            blk = pltpu.roll(blk, HEAD_DIM, 1)
        parts.append(jnp.where(lane_half == grp, blk, 0.0))
    qpad = jnp.concatenate(parts, axis=0).astype(bf16)

    def heads(x):
        return jnp.concatenate([x[:tq]] * GROUP + [x[tq:]] * GROUP, axis=0)

    tbl_c = bc_ref[0].reshape(rows, bc_ref.shape[-1])
    s = lax.dot_general(qpad, kc_ref[0], nt, preferred_element_type=f32) + tbl_c
    ok = tbl_c > 0.5 * NEG
    m = jnp.max(s, axis=1, keepdims=True)
    e = jnp.where(ok, jnp.exp(s - m), 0.0)
    d = jnp.sum(e, axis=1, keepdims=True)
    pc = e / jnp.where(d > 0.0, d, 1.0)
    pc16 = pc.astype(bf16)
    o_cmp = jnp.dot(pc16, vc_ref[0], preferred_element_type=f32)

    imp_h = jnp.dot(pc16, cover_ref[...], preferred_element_type=f32)
    imp = jnp.concatenate([_group_sum(imp_h, grp, tq) for grp in range(N_KV)], axis=0)
    ns = imp.shape[1]
    j = lax.broadcasted_iota(jnp.int32, imp.shape, 1)
    forced = (j == 0) | (j == i) | (j == i - 1)
    imp = jnp.where(forced, imp + FORCE_SCORE, imp)
    imp = jnp.where(j > i, -FORCE_SCORE, imp)
    imp_p = jnp.concatenate([imp[:tq], imp[tq:]], axis=1)
    lane_p = lax.broadcasted_iota(jnp.int32, imp_p.shape, 1)
    j_p = jnp.where(lane_p < ns, lane_p, lane_p - ns)
    cnt = jnp.zeros(imp_p.shape, jnp.int32)
    for jp in range(ns):
        col = jnp.where(lane_p < ns, imp_p[:, jp:jp + 1], imp_p[:, ns + jp:ns + jp + 1])
        ahead = (col > imp_p) | ((col == imp_p) & (j_p > jp))
        cnt = cnt + ahead.astype(jnp.int32)
    sel_p = jnp.where(cnt < TOP_N, 1.0, 0.0)
    sel = jnp.concatenate([sel_p[:, :ns], sel_p[:, ns:]], axis=0) > 0.5

    nb0 = jnp.maximum(i - (NEAR_BLOCKS - 1), 0)
    near = NEAR_BLOCKS * SEL_BLOCK
    k0 = pl.multiple_of(nb0 * SEL_BLOCK, SEL_BLOCK)
    sel_nb0 = jnp.sum(jnp.where((j == nb0) & sel, 1.0, 0.0), axis=1, keepdims=True)
    lane = lax.broadcasted_iota(jnp.int32, (rows, near), 1)
    s = lax.dot_general(qpad, ks_ref[0, pl.ds(k0, near), :], nt, preferred_element_type=f32)
    s = s + bsn_ref[0].reshape(rows, near)
    s = s + jnp.where(lane < SEL_BLOCK, (heads(sel_nb0) - 1.0) * SEL_MASK, 0.0)
    m = jnp.max(s, axis=1, keepdims=True)
    p = jnp.exp(s - m)
    l = jnp.sum(p, axis=1, keepdims=True)
    acc = jnp.dot(p.astype(bf16), vs_ref[0, pl.ds(k0, near), :], preferred_element_type=f32)

    sel_far = heads(jnp.where(sel & (j < nb0), 0.0, -1.0)).astype(bf16)
    n_far = (nb0 * SEL_BLOCK + FAR_KEYS - 1) // FAR_KEYS

    def far_body(c, carry):
        m, l, acc = carry
        c0 = pl.multiple_of(c * FAR_KEYS, FAR_KEYS)
        s = lax.dot_general(qpad, ks_ref[0, pl.ds(c0, FAR_KEYS), :], nt, preferred_element_type=f32)
        s = s + jnp.dot(sel_far, e_ref[c], preferred_element_type=f32)
        m_new = jnp.maximum(m, jnp.max(s, axis=1, keepdims=True))
        a = jnp.exp(m - m_new)
        p = jnp.exp(s - m_new)
        l = a * l + jnp.sum(p, axis=1, keepdims=True)
        acc = a * acc + jnp.dot(p.astype(bf16), vs_ref[0, pl.ds(c0, FAR_KEYS), :],
                                preferred_element_type=f32)
        return m_new, l, acc

    m, l, acc = lax.fori_loop(0, n_far, far_body, (m, l, acc))
    o_sel = acc / l

    wk = WIN_BLOCKS * SEL_BLOCK
    w0 = pl.multiple_of(jnp.maximum(i - (WIN_BLOCKS - 1), 0) * SEL_BLOCK, SEL_BLOCK)
    s = lax.dot_general(qpad, kw_ref[0, pl.ds(w0, wk), :], nt, preferred_element_type=f32)
    s = s + bw_ref[0].reshape(rows, wk)
    m = jnp.max(s, axis=1, keepdims=True)
    p = jnp.exp(s - m)
    l = jnp.sum(p, axis=1, keepdims=True)
    o_win = jnp.dot(p.astype(bf16), vw_ref[0, pl.ds(w0, wk), :], preferred_element_type=f32) / l

    gates = g_ref[0]
    for h in range(N_HEADS):
        grp = h // GROUP
        r0, r1 = h * tq, (h + 1) * tq
        c0, c1 = grp * HEAD_DIM, (grp + 1) * HEAD_DIM
        o_h = (gates[:, 3 * h:3 * h + 1] * o_cmp[r0:r1, c0:c1]
               + gates[:, 3 * h + 1:3 * h + 2] * o_sel[r0:r1, c0:c1]
               + gates[:, 3 * h + 2:3 * h + 3] * o_win[r0:r1, c0:c1])
        o_ref[0, :, h * HEAD_DIM:(h + 1) * HEAD_DIM] = o_h


def nsa_prompt_attention(q, gates, kc, vc, ks, vs, kw, vw, rel_bias):
    B, T = q.shape[0], q.shape[1]
    tq = SEL_BLOCK
    nq = T // tq
    ns = T // SEL_BLOCK
    nc = kc.shape[1]
    ncp = -(-nc // LANES) * LANES
    bf16 = jnp.bfloat16
    scale = HEAD_DIM ** -0.5
    assert T % FAR_KEYS == 0 and T >= WIN_BLOCKS * SEL_BLOCK

    tl = np.arange(tq)[None, :, None]
    r = tq // CMP_STRIDE
    span = r * (nq - 1) + ncp
    y = np.arange(span)[None, None, :]
    dy = tl + CMP_STRIDE * (y - (ncp - 1)) - (CMP_LEN - 1)
    master = _bias_table(rel_bias, dy, dy >= 0)[0][:, :, ::-1]
    pad_col = jnp.arange(ncp) < nc
    bc = jnp.stack([jnp.where(pad_col, master[:, :, r * (nq - 1 - i):r * (nq - 1 - i) + ncp], NEG)
                    for i in range(nq)])
    bsn = _toeplitz_bias(rel_bias, NEAR_BLOCKS, tq, NEAR_BLOCKS * SEL_BLOCK, lambda d: d >= 0)
    bw = _toeplitz_bias(rel_bias, WIN_BLOCKS, tq, WIN_BLOCKS * SEL_BLOCK, lambda d: (d >= 0) & (d < WINDOW))

    ci = np.arange(ncp)[:, None]
    sj = np.arange(ns)[None, :]
    cover = ((ci * CMP_STRIDE < (sj + 1) * SEL_BLOCK) & (ci * CMP_STRIDE + CMP_LEN > sj * SEL_BLOCK)
             & (ci < nc))
    cover = jnp.asarray(cover.astype(np.float32), bf16)
    pos = np.arange(T).reshape(T // FAR_KEYS, 1, FAR_KEYS)
    expand = jnp.asarray(np.where(pos // SEL_BLOCK == np.arange(ns)[None, :, None], SEL_MASK, 0.0)
                         .astype(np.float32), bf16)

    padc = ((0, 0), (0, ncp - nc), (0, 0))
    seq = lambda b, i: (b, 0, 0)
    return pl.pallas_call(
        functools.partial(_nsa_prompt_kernel, tq=tq),
        grid=(B, nq),
        in_specs=[
            pl.BlockSpec((1, tq, ATTN_WIDTH), lambda b, i: (b, i, 0)),
            pl.BlockSpec((1, tq, 3 * N_HEADS), lambda b, i: (b, i, 0)),
            pl.BlockSpec((1, ncp, KV_WIDTH), seq),
            pl.BlockSpec((1, ncp, KV_WIDTH), seq),
            pl.BlockSpec((1, T, KV_WIDTH), seq),
            pl.BlockSpec((1, T, KV_WIDTH), seq),
            pl.BlockSpec((1, T, KV_WIDTH), seq),
            pl.BlockSpec((1, T, KV_WIDTH), seq),
            pl.BlockSpec((1, N_HEADS, tq, ncp), lambda b, i: (i, 0, 0, 0)),
            pl.BlockSpec((1, N_HEADS, tq, NEAR_BLOCKS * SEL_BLOCK),
                         lambda b, i: (jnp.minimum(i, NEAR_BLOCKS - 1), 0, 0, 0)),
            pl.BlockSpec((1, N_HEADS, tq, WIN_BLOCKS * SEL_BLOCK),
                         lambda b, i: (jnp.minimum(i, WIN_BLOCKS - 1), 0, 0, 0)),
            pl.BlockSpec((ncp, ns), lambda b, i: (0, 0)),
            pl.BlockSpec((T // FAR_KEYS, ns, FAR_KEYS), lambda b, i: (0, 0, 0)),
        ],
        out_specs=pl.BlockSpec((1, tq, ATTN_WIDTH), lambda b, i: (b, i, 0)),
        out_shape=jax.ShapeDtypeStruct((B, T, ATTN_WIDTH), jnp.float32),
        compiler_params=pltpu.CompilerParams(dimension_semantics=("arbitrary", "arbitrary"),
                                             vmem_limit_bytes=VMEM_LIMIT),
        name="nsa_prompt",
    )(q * scale, gates, jnp.pad(kc, padc).astype(bf16), jnp.pad(vc, padc).astype(bf16),
      ks.astype(bf16), vs.astype(bf16), kw.astype(bf16), vw.astype(bf16), bc, bsn, bw, cover, expand)


NEW_PAD = 16
KV_SLOTS = 3


def _softmax_parts(s):
    m = jnp.max(s, axis=1, keepdims=True)
    p = jnp.exp(s - m)
    return m, p, jnp.sum(p, axis=1, keepdims=True)


def _nsa_sample_kernel(pt_ref, q_ref, g_ref, kc_ref, vc_ref, cache_ref, nsk_ref, nsv_ref, cw_ref, nwk_ref, nwv_ref,
                       bc_ref, bsl_ref, bnew_ref, bw_ref, cover_ref, e_ref, o_ref, kv_buf, sem, *, tn, ns):
    f32, bf16 = jnp.float32, jnp.bfloat16
    nt = (((1,), (1,)), ((), ()))
    b = pl.program_id(0)
    n_pages, page = pt_ref.shape[1], cache_ref.shape[-1]
    pages_per_chunk = FAR_KEYS // page
    n_far = n_pages // pages_per_chunk
    total = pl.num_programs(0) * n_far

    def fetch(gc):
        seq, first = gc // n_far, (gc % n_far) * pages_per_chunk
        slot = gc % KV_SLOTS
        for i in range(pages_per_chunk):
            pltpu.make_async_copy(cache_ref.at[pt_ref[seq, first + i]], kv_buf.at[slot, i], sem.at[slot]).start()

    @pl.when(b == 0)
    def _():
        for gc in range(KV_SLOTS - 1):
            fetch(gc)
    qpad = q_ref[0]
    cur = ns - 1

    def heads(x):
        return jnp.concatenate([x[:tn]] * GROUP + [x[tn:]] * GROUP, axis=0)

    s = lax.dot_general(qpad, kc_ref[0], nt, preferred_element_type=f32) + bc_ref[...]
    _, e, d = _softmax_parts(s)
    pc = e / d
    pc16 = pc.astype(bf16)
    o_cmp = jnp.dot(pc16, vc_ref[0], preferred_element_type=f32)

    imp_h = jnp.dot(pc16, cover_ref[...], preferred_element_type=f32)
    imp = jnp.concatenate([_group_sum(imp_h, grp, tn) for grp in range(N_KV)], axis=0)
    j = lax.broadcasted_iota(jnp.int32, imp.shape, 1)
    forced = (j == 0) | (j == cur) | (j == cur - 1)
    imp = jnp.where(forced, imp + FORCE_SCORE, imp)
    imp = jnp.where(j > cur, -jnp.inf, imp)
    cnt = jnp.zeros(imp.shape, jnp.int32)
    for jp in range(ns):
        col = imp[:, jp:jp + 1]
        ahead = (col > imp) | ((col == imp) & (j > jp))
        cnt = cnt + ahead.astype(jnp.int32)
    sel = cnt < TOP_N
    sel_far = heads(jnp.where(sel & (j < cur), 0.0, -1.0)).astype(bf16)

    bnew = bnew_ref[...]
    m, p, l = _softmax_parts(lax.dot_general(qpad, nsk_ref[0], nt, preferred_element_type=f32) + bnew)
    acc = jnp.dot(p.astype(bf16), nsv_ref[0], preferred_element_type=f32)
    half = GROUP * tn
    q_g = [qpad[grp * half:(grp + 1) * half, grp * HEAD_DIM:(grp + 1) * HEAD_DIM] for grp in range(N_KV)]
    zero = jnp.zeros((half, HEAD_DIM), f32)

    def far_chunk(c, carry, bias):
        m, l, acc = carry
        gc = b * n_far + c
        slot = gc % KV_SLOTS
        pltpu.make_async_copy(kv_buf.at[slot], kv_buf.at[slot], sem.at[slot]).wait()

        @pl.when(gc + KV_SLOTS - 1 < total)
        def _():
            fetch(gc + KV_SLOTS - 1)

        def plane(kv, grp):
            return jnp.concatenate([kv_buf[slot, i, kv, grp] for i in range(pages_per_chunk)], axis=1).astype(bf16)

        s = jnp.concatenate([jnp.dot(q_g[grp], plane(0, grp), preferred_element_type=f32)
                             for grp in range(N_KV)], axis=0)
        s = s + jnp.dot(sel_far, e_ref[c], preferred_element_type=f32)
        if bias is not None:
            s = s + bias
        m_new = jnp.maximum(m, jnp.max(s, axis=1, keepdims=True))
        a = jnp.exp(m - m_new)
        p = jnp.exp(s - m_new)
        l = a * l + jnp.sum(p, axis=1, keepdims=True)
        p = p.astype(bf16)
        pv = [lax.dot_general(p[grp * half:(grp + 1) * half], plane(1, grp), nt, preferred_element_type=f32)
              for grp in range(N_KV)]
        pv = jnp.concatenate([jnp.concatenate([pv[0], zero], axis=1),
                              jnp.concatenate([zero, pv[1]], axis=1)], axis=0)
        return m_new, l, a * acc + pv

    carry = lax.fori_loop(0, n_far - 1, lambda c, carry: far_chunk(c, carry, None), (m, l, acc))
    m, l, acc = far_chunk(n_far - 1, carry, bsl_ref[...])
    o_sel = acc / l

    cw = cw_ref[0]
    s_c = lax.dot_general(qpad, cw[:, :KV_WIDTH].astype(bf16), nt, preferred_element_type=f32) + bw_ref[...]
    s_n = lax.dot_general(qpad, nwk_ref[0], nt, preferred_element_type=f32) + bnew
    m = jnp.maximum(jnp.max(s_c, axis=1, keepdims=True), jnp.max(s_n, axis=1, keepdims=True))
    p_c = jnp.exp(s_c - m)
    p_n = jnp.exp(s_n - m)
    l = jnp.sum(p_c, axis=1, keepdims=True) + jnp.sum(p_n, axis=1, keepdims=True)
    o_win = (jnp.dot(p_c.astype(bf16), cw[:, KV_WIDTH:].astype(bf16), preferred_element_type=f32)
             + jnp.dot(p_n.astype(bf16), nwv_ref[0], preferred_element_type=f32)) / l

    g = g_ref[0]
    o_ref[0] = g[:, 0:1] * o_cmp + g[:, 1:2] * o_sel + g[:, 2:3] * o_win


def nsa_sample_attention(q, gates, kc, vc, cache_sel, page_table, new_sel, cache_win, new_win, rel_bias):
    B, tn = q.shape[0], q.shape[1]
    past, wk, nc = page_table.shape[1] * cache_sel.shape[-1], cache_win.shape[1], kc.shape[1]
    assert past % FAR_KEYS == 0 and past % SEL_BLOCK == 0 and tn <= NEW_PAD and wk == WINDOW
    ns = past // SEL_BLOCK + 1
    nsp = -(-ns // LANES) * LANES
    ncp = -(-nc // LANES) * LANES
    rows = N_HEADS * tn
    bf16 = jnp.bfloat16
    n_far = past // FAR_KEYS

    tl = np.arange(tn)[:, None]
    flat = lambda t: t[0].reshape(rows, t.shape[-1])
    cc = np.arange(ncp)[None, :]
    dc = past + tl - (cc * CMP_STRIDE + CMP_LEN - 1)
    bc = flat(_bias_table(rel_bias, dc[None], ((dc >= 0) & (cc < nc))[None]))
    dl = tl + FAR_KEYS - np.arange(FAR_KEYS)[None, :]
    bsl = flat(_bias_table(rel_bias, dl[None], (dl >= 0)[None]))
    jn = np.arange(NEW_PAD)[None, :]
    dn = tl - jn
    bnew = flat(_bias_table(rel_bias, dn[None], ((dn >= 0) & (jn < tn))[None]))
    dw = wk + tl - np.arange(wk)[None, :]
    bw = flat(_bias_table(rel_bias, dw[None], ((dw >= 0) & (dw < WINDOW))[None]))

    ci = np.arange(ncp)[:, None]
    sj = np.arange(nsp)[None, :]
    cover = ((ci * CMP_STRIDE < (sj + 1) * SEL_BLOCK) & (ci * CMP_STRIDE + CMP_LEN > sj * SEL_BLOCK)
             & (ci < nc) & (sj < ns))
    cover = jnp.asarray(cover.astype(np.float32), bf16)
    pos = np.arange(past).reshape(n_far, 1, FAR_KEYS)
    expand = jnp.asarray(np.where(pos // SEL_BLOCK == np.arange(nsp)[None, :, None], SEL_MASK, 0.0)
                         .astype(np.float32), bf16)

    qh = jnp.transpose((q * HEAD_DIM ** -0.5).reshape(B, tn, N_HEADS, HEAD_DIM), (0, 2, 1, 3))
    zero = jnp.zeros_like(qh)
    grp = (np.arange(N_HEADS) // GROUP)[None, :, None, None]
    qpad = jnp.where(grp == 0, jnp.concatenate([qh, zero], -1), jnp.concatenate([zero, qh], -1))
    qpad = qpad.reshape(B, rows, KV_WIDTH).astype(bf16)
    grow = jnp.transpose(gates.reshape(B, tn, N_HEADS, 3), (0, 2, 1, 3)).reshape(B, rows, 3)
    padn = lambda a: jnp.pad(a, ((0, 0), (0, NEW_PAD - tn), (0, 0))).astype(bf16)
    padc = ((0, 0), (0, ncp - nc), (0, 0))

    seq3 = lambda b, pt: (b, 0, 0)
    const2 = lambda b, pt: (0, 0)
    grid_spec = pltpu.PrefetchScalarGridSpec(
        num_scalar_prefetch=1,
        grid=(B,),
        in_specs=[
            pl.BlockSpec((1, rows, KV_WIDTH), seq3),
            pl.BlockSpec((1, rows, 3), seq3),
            pl.BlockSpec((1, ncp, KV_WIDTH), seq3),
            pl.BlockSpec((1, ncp, KV_WIDTH), seq3),
            pl.BlockSpec(memory_space=pl.ANY),
            pl.BlockSpec((1, NEW_PAD, KV_WIDTH), seq3),
            pl.BlockSpec((1, NEW_PAD, KV_WIDTH), seq3),
            pl.BlockSpec((1, wk, 2 * KV_WIDTH), seq3),
            pl.BlockSpec((1, NEW_PAD, KV_WIDTH), seq3),
            pl.BlockSpec((1, NEW_PAD, KV_WIDTH), seq3),
            pl.BlockSpec((rows, ncp), const2),
            pl.BlockSpec((rows, FAR_KEYS), const2),
            pl.BlockSpec((rows, NEW_PAD), const2),
            pl.BlockSpec((rows, wk), const2),
            pl.BlockSpec((ncp, nsp), const2),
            pl.BlockSpec((n_far, nsp, FAR_KEYS), lambda b, pt: (0, 0, 0)),
        ],
        out_specs=pl.BlockSpec((1, rows, KV_WIDTH), seq3),
        scratch_shapes=[pltpu.VMEM((KV_SLOTS, FAR_KEYS // cache_sel.shape[-1]) + cache_sel.shape[1:], jnp.float32),
                        pltpu.SemaphoreType.DMA((KV_SLOTS,))],
    )
    out = pl.pallas_call(
        functools.partial(_nsa_sample_kernel, tn=tn, ns=ns),
        grid_spec=grid_spec,
        out_shape=jax.ShapeDtypeStruct((B, rows, KV_WIDTH), jnp.float32),
        compiler_params=pltpu.CompilerParams(dimension_semantics=("arbitrary",),
                                             vmem_limit_bytes=VMEM_LIMIT),
        name="nsa_sample",
    )(page_table, qpad, grow, jnp.pad(kc, padc).astype(bf16), jnp.pad(vc, padc).astype(bf16), cache_sel,
      padn(new_sel[:, :, :KV_WIDTH]), padn(new_sel[:, :, KV_WIDTH:]), cache_win,
      padn(new_win[:, :, :KV_WIDTH]), padn(new_win[:, :, KV_WIDTH:]), bc, bsl, bnew, bw, cover, expand)
    o = out.reshape(B, N_HEADS, tn, 2, HEAD_DIM)
    o = jnp.where(grp[..., None] == 0, o[:, :, :, 0:1], o[:, :, :, 1:2])[:, :, :, 0]
    return jnp.transpose(o, (0, 2, 1, 3)).reshape(B, tn, ATTN_WIDTH)


N_SUB = 2 * PEER_HEADS
HALF_DQ = PEER_DQ // 2
N_PAIRS = PEER_HEADS * PEER_TOPK


def _top_rounds(s, row, n):
    width = s.shape[0]
    k = lax.broadcasted_iota(jnp.int32, (n, s.shape[1]), 0)
    vals = jnp.zeros((n, s.shape[1]), jnp.float32)
    ids = jnp.zeros((n, s.shape[1]), jnp.float32)
    for a in range(n):
        m = jnp.max(s, axis=0, keepdims=True)
        ix = jnp.min(jnp.where(s == m, row, float(width)), axis=0, keepdims=True)
        vals = jnp.where(k == a, m, vals)
        ids = jnp.where(k == a, ix, ids)
        s = jnp.where(row == ix, -jnp.inf, s)
    return vals, ids


def _peer_route_kernel(h_ref, wq_ref, keys_ref, idx_ref, g_ref, q_scr, e_scr, w_scr):
    f32, bf16 = jnp.float32, jnp.bfloat16
    tm = h_ref.shape[0]
    nt = (((1,), (1,)), ((), ()))
    q = jnp.dot(h_ref[...].astype(bf16), wq_ref[...], preferred_element_type=f32)
    for k in range(N_SUB):
        q_scr[k] = q[:, k * HALF_DQ:(k + 1) * HALF_DQ].astype(bf16)
    row = lax.broadcasted_iota(jnp.int32, (N_KEYS, tm), 0).astype(f32)
    cand_blocks, a = [], 0
    while PEER_TOPK // (a + 1) > 1:
        cand_blocks.append((a, a + 1, min(-(-(PEER_TOPK // (a + 1)) // 8) * 8, PEER_TOPK)))
        a += 1
    cand_blocks.append((a, PEER_TOPK, 1))
    row2 = jnp.concatenate(
        [(lax.broadcasted_iota(jnp.int32, (max(a1 - a0, nb), tm), 0) * (1 if nb > 1 else PEER_TOPK)
          + a0 * PEER_TOPK).astype(f32) for a0, a1, nb in cand_blocks], axis=0)
    n_cand = PEER_TOPK * PEER_TOPK

    def head(h, carry):
        s1, i1 = _top_rounds(lax.dot_general(keys_ref[2 * h], q_scr[2 * h], nt, preferred_element_type=f32),
                             row, PEER_TOPK)
        s2, i2 = _top_rounds(lax.dot_general(keys_ref[2 * h + 1], q_scr[2 * h + 1], nt,
                                             preferred_element_type=f32), row, PEER_TOPK)
        cand = jnp.concatenate([s1[a0:a1] + s2[:nb] for a0, a1, nb in cand_blocks], axis=0)
        cidx = jnp.concatenate([i1[a0:a1] * float(N_KEYS) + i2[:nb] for a0, a1, nb in cand_blocks], axis=0)
        k = lax.broadcasted_iota(jnp.int32, (PEER_TOPK, tm), 0)
        top = jnp.zeros((PEER_TOPK, tm), f32)
        eid = jnp.zeros((PEER_TOPK, tm), f32)
        for a in range(PEER_TOPK):
            m = jnp.max(cand, axis=0, keepdims=True)
            pos = jnp.min(jnp.where(cand == m, row2, float(n_cand)), axis=0, keepdims=True)
            hit = row2 == pos
            e = jnp.max(jnp.where(hit, cidx, -1.0), axis=0, keepdims=True)
            top = jnp.where(k == a, m, top)
            eid = jnp.where(k == a, e, eid)
            cand = jnp.where(hit, -jnp.inf, cand)
        ex = jnp.exp(top - top[0:1])
        r0 = pl.multiple_of(h * PEER_TOPK, PEER_TOPK)
        e_scr[pl.ds(r0, PEER_TOPK), :] = eid
        w_scr[pl.ds(r0, PEER_TOPK), :] = ex / jnp.sum(ex, axis=0, keepdims=True)
        return carry

    lax.fori_loop(0, PEER_HEADS, head, 0)
    idx_ref[...] = e_scr[...].T.astype(jnp.int32)
    g_ref[...] = w_scr[...].T


def peer_route(h2, peer_q, peer_keys, tm=128):
    n = h2.shape[0]
    tm = min(tm, n)
    bf16 = jnp.bfloat16
    keys = peer_keys.reshape(N_SUB, N_KEYS, HALF_DQ).astype(bf16)
    return pl.pallas_call(
        _peer_route_kernel,
        grid=(n // tm,),
        in_specs=[pl.BlockSpec((tm, D_MODEL), lambda i: (i, 0)),
                  pl.BlockSpec((D_MODEL, PEER_HEADS * PEER_DQ), lambda i: (0, 0)),
                  pl.BlockSpec((N_SUB, N_KEYS, HALF_DQ), lambda i: (0, 0, 0))],
        out_specs=[pl.BlockSpec((tm, N_PAIRS), lambda i: (i, 0)),
                   pl.BlockSpec((tm, N_PAIRS), lambda i: (i, 0))],
        out_shape=[jax.ShapeDtypeStruct((n, N_PAIRS), jnp.int32),
                   jax.ShapeDtypeStruct((n, N_PAIRS), jnp.float32)],
        scratch_shapes=[pltpu.VMEM((N_SUB, tm, HALF_DQ), bf16),
                        pltpu.VMEM((N_PAIRS, tm), jnp.float32),
                        pltpu.VMEM((N_PAIRS, tm), jnp.float32)],
        compiler_params=pltpu.CompilerParams(dimension_semantics=("arbitrary",),
                                             vmem_limit_bytes=VMEM_LIMIT),
        name="peer_route",
    )(h2, peer_q.astype(bf16), keys)


def _gelu_tanh(x):
    return 0.5 * x * (1.0 + jnp.tanh(math.sqrt(2.0 / math.pi) * (x + 0.044715 * (x * x * x))))


N_SLOTS = 8
DMA_QUEUES = 2
ROW_TILES = 2 * D_MODEL // LANES
U_TILES = D_MODEL // LANES


def _peer_expert_kernel(idx_ref, nxt_ref, x_ref, g_ref, uv_ref, y_ref, *scratch):
    tb = x_ref.shape[0]
    i = pl.program_id(0)
    last = pl.num_programs(0) - 1
    ahead = N_SLOTS - 1
    bufs, sem = scratch[:N_SLOTS], scratch[N_SLOTS]

    def issue(ids, t, slot):
        for k in range(N_PAIRS):
            pltpu.make_async_copy(uv_ref.at[ids[t, k]], bufs[slot].at[k // 8, :, k % 8, :],
                                  sem.at[slot]).start(priority=k % DMA_QUEUES)

    def wait_slot(slot):
        pltpu.make_async_copy(bufs[slot], bufs[slot], sem.at[slot]).wait()

    eye = (lax.broadcasted_iota(jnp.int32, (N_PAIRS, N_PAIRS), 0)
           == lax.broadcasted_iota(jnp.int32, (N_PAIRS, N_PAIRS), 1))

    def compute(t, slot):
        buf = bufs[slot]

        def chunk(c):
            return buf[:, c].reshape(N_PAIRS, LANES)

        x = x_ref[t]
        acc = chunk(0) * x[0:1, :]
        for s in range(1, U_TILES):
            acc = acc + chunk(s) * x[s:s + 1, :]
        a = jnp.sum(acc, axis=1, keepdims=True)
        g_col = jnp.sum(jnp.where(eye, g_ref[pl.ds(t, 1), :], 0.0), axis=1, keepdims=True)
        w = g_col * _gelu_tanh(a)
        y = [jnp.sum(w * chunk(U_TILES + s), axis=0, keepdims=True) for s in range(U_TILES)]
        y_ref[pl.ds(t, 1), :] = jnp.concatenate(y, axis=1)

    @pl.when(i == 0)
    def _():
        for t in range(ahead):
            issue(idx_ref, t, t)

    def group(j, c):
        for u in range(N_SLOTS):
            t = j * N_SLOTS + u
            wait_slot(u)
            issue(idx_ref, t + ahead, (u + ahead) % N_SLOTS)
            compute(t, u)
        return c

    n_groups = tb // N_SLOTS
    lax.fori_loop(0, n_groups - 1, group, 0)
    for u in range(N_SLOTS):
        t = tb - N_SLOTS + u
        wait_slot(u)
        if u == 0:
            issue(idx_ref, t + ahead, ahead)
        else:
            issue(nxt_ref, u - 1, u - 1)
        compute(t, u)

    @pl.when(i == last)
    def _():
        for u in range(ahead):
            wait_slot(u)


def peer_experts(h2, eidx, gate, expert_u, expert_v, tb=64):
    n = h2.shape[0]
    tb = min(tb, n)
    nb = n // tb
    e = expert_u.shape[0]
    uv = jnp.concatenate([expert_u.reshape(e, U_TILES, LANES), expert_v.reshape(e, U_TILES, LANES)], axis=1)
    return pl.pallas_call(
        _peer_expert_kernel,
        grid=(nb,),
        in_specs=[pl.BlockSpec((tb, N_PAIRS), lambda i: (i, 0), memory_space=pltpu.SMEM),
                  pl.BlockSpec((tb, N_PAIRS), lambda i: (jnp.minimum(i + 1, nb - 1), 0), memory_space=pltpu.SMEM),
                  pl.BlockSpec((tb, U_TILES, LANES), lambda i: (i, 0, 0)),
                  pl.BlockSpec((tb, N_PAIRS), lambda i: (i, 0)),
                  pl.BlockSpec(memory_space=pl.ANY)],
        out_specs=pl.BlockSpec((tb, D_MODEL), lambda i: (i, 0)),
        out_shape=jax.ShapeDtypeStruct((n, D_MODEL), jnp.float32),
        scratch_shapes=[pltpu.VMEM((N_PAIRS // 8, ROW_TILES, 8, LANES), jnp.float32)] * N_SLOTS
        + [pltpu.SemaphoreType.DMA((N_SLOTS,))],
        compiler_params=pltpu.CompilerParams(dimension_semantics=("arbitrary",),
                                             vmem_limit_bytes=VMEM_LIMIT),
        name="peer_experts",
    )(eidx, eidx, h2.reshape(n, U_TILES, LANES), gate, uv)


def rms_norm(x, g):
    y = x * lax.rsqrt(jnp.mean(x * x, axis=-1, keepdims=True) + EPS)
    return y * g


def layer_norm(x, g, b):
    mu = jnp.mean(x, axis=-1, keepdims=True)
    var = jnp.mean(jnp.square(x - mu), axis=-1, keepdims=True)
    return (x - mu) * lax.rsqrt(var + EPS) * g + b


def masked_softmax(logits, mask, axis):
    neg = jnp.where(mask, logits, -jnp.inf)
    m = jnp.max(neg, axis=axis, keepdims=True)
    m = jnp.where(jnp.isfinite(m), m, 0.0)
    e = jnp.where(mask, jnp.exp(neg - m), 0.0)
    d = jnp.sum(e, axis=axis, keepdims=True)
    return e / jnp.where(d > 0.0, d, 1.0)


def t5_bucket(dist):
    n = jnp.maximum(dist, 0)
    exact = N_BUCKETS // 2
    nf = jnp.maximum(n, 1).astype(jnp.float32)
    large = exact + (jnp.log(nf / exact) / math.log(MAX_DISTANCE / exact) * (N_BUCKETS - exact)).astype(jnp.int32)
    return jnp.where(n < exact, n, jnp.minimum(large, N_BUCKETS - 1))


def compress(rows, pe, w1, b1, w2, b2):
    B, L = rows.shape[0], rows.shape[1]
    ratio = CMP_LEN // CMP_STRIDE
    n_chunks = L // CMP_STRIDE
    nc = n_chunks - ratio + 1
    r = rows[:, :n_chunks * CMP_STRIDE].reshape(B, n_chunks, CMP_STRIDE, N_KV, HEAD_DIM)
    w1c = w1.reshape(ratio, CMP_STRIDE, HEAD_DIM, CMP_HIDDEN)
    parts = jnp.einsum('bcsgd,jsdh->jbcgh', r, w1c)
    hid = b1 + pe.reshape(-1) @ w1
    for j in range(ratio):
        hid = hid + parts[j, :, j:j + nc]
    out = jax.nn.gelu(hid) @ w2 + b2
    end = jnp.arange(nc, dtype=jnp.int32) * CMP_STRIDE + (CMP_LEN - 1)
    return out, end


def compress_branch(rows, lp, which):
    return compress(rows, lp['cmp_pe'][which], lp['cmp_w1'][which], lp['cmp_b1'][which],
                    lp['cmp_w2'][which], lp['cmp_b2'][which])


def to_blocks(rows):
    B, L = rows.shape[0], rows.shape[1]
    ns = -(-L // SEL_BLOCK)
    rows = jnp.pad(rows, ((0, 0), (0, ns * SEL_BLOCK - L), (0, 0), (0, 0)))
    return jnp.transpose(rows.reshape(B, ns, SEL_BLOCK, N_KV, HEAD_DIM), (0, 3, 1, 2, 4))


def nsa_block(q, t_pos, gates, kc, vc, kc_end, ks_blk, vs_blk, kw, vw, w_pos, rel_bias):
    B = q.shape[0]
    scale = HEAD_DIM ** -0.5
    tbl = rel_bias.astype(jnp.float32).reshape(N_BUCKETS, N_KV, GROUP)
    lc = jnp.einsum('bqgrd,bcgd->bqgrc', q, kc).astype(jnp.float32) * scale
    dc = t_pos[:, None] - kc_end[None, :]
    lc = lc + jnp.transpose(tbl[t5_bucket(dc)], (0, 2, 3, 1))[None]
    pc = masked_softmax(lc, (dc >= 0)[None, :, None, None, :], axis=-1)
    o_cmp = jnp.einsum('bqgrc,bcgd->bqgrd', pc.astype(vc.dtype), vc)
    nc, ns = kc.shape[1], ks_blk.shape[2]
    ci = jnp.arange(nc)[:, None]
    sj = jnp.arange(ns)[None, :]
    cover = ((ci * CMP_STRIDE < (sj + 1) * SEL_BLOCK) & (ci * CMP_STRIDE + CMP_LEN > sj * SEL_BLOCK)).astype(jnp.float32)
    imp = jnp.einsum('bqgrc,cj->bqgj', pc, cover)
    cur = (t_pos // SEL_BLOCK)[:, None]
    forced = (sj == 0) | (sj == cur) | (sj == cur - 1)
    valid = sj * SEL_BLOCK <= t_pos[:, None]
    imp = jnp.where(forced[None, :, None, :], imp + FORCE_SCORE, imp)
    imp = jnp.where(valid[None, :, None, :], imp, -FORCE_SCORE)
    _, idx = lax.top_k(imp, min(TOP_N, ns))
    bi = jnp.arange(B)[:, None, None, None]
    gi = jnp.arange(N_KV)[None, None, :, None]
    ks_g = ks_blk[bi, gi, idx]
    vs_g = vs_blk[bi, gi, idx]
    ls = jnp.einsum('bqgrd,bqgnkd->bqgrnk', q, ks_g).astype(jnp.float32) * scale
    pos = idx[..., None] * SEL_BLOCK + jnp.arange(SEL_BLOCK, dtype=jnp.int32)
    ds = t_pos[None, :, None, None, None] - pos
    tblk = jnp.transpose(tbl, (1, 0, 2))
    bs = tblk[jnp.arange(N_KV)[None, None, :, None, None], t5_bucket(ds)]
    ls = ls + jnp.moveaxis(bs, -1, 3)
    ps = masked_softmax(ls, (ds >= 0)[:, :, :, None], axis=(-2, -1))
    o_sel = jnp.einsum('bqgrnk,bqgnkd->bqgrd', ps.astype(vs_g.dtype), vs_g)
    lw = jnp.einsum('bqgrd,bwgd->bqgrw', q, kw).astype(jnp.float32) * scale
    dw = t_pos[:, None] - w_pos[None, :]
    lw = lw + jnp.transpose(tbl[t5_bucket(dw)], (0, 2, 3, 1))[None]
    mw = (dw >= 0) & (dw < WINDOW) & (w_pos >= 0)[None, :]
    pw = masked_softmax(lw, mw[None, :, None, None, :], axis=-1)
    o_win = jnp.einsum('bqgrw,bwgd->bqgrd', pw.astype(vw.dtype), vw)
    return gates[..., 0:1] * o_cmp + gates[..., 1:2] * o_sel + gates[..., 2:3] * o_win


def nsa_prompt(q, gates, kc_raw, vc_raw, ks, vs, kw, vw, lp, rel_bias):
    B, T = q.shape[0], q.shape[1]
    kc, kc_end = compress_branch(kc_raw, lp, 0)
    kc = rms_norm(kc, lp['k_norm'][0])
    vc, _ = compress_branch(vc_raw, lp, 1)
    flat = lambda a: a.reshape(B, a.shape[1], -1)
    o = nsa_prompt_attention(flat(q), flat(gates), flat(kc), flat(vc), flat(ks), flat(vs), flat(kw), flat(vw),
                             rel_bias)
    new_win = jnp.stack([kw, vw], axis=2)[:, T - min(WINDOW, T):]
    return o, (jnp.stack([kc_raw, vc_raw], axis=2), jnp.stack([ks, vs], axis=2), new_win)


def nsa_sample(q, gates, kc_raw, vc_raw, ks, vs, kw, vw, cache_cmp, cache_sel, cache_win, page_table, lp, rel_bias):
    Bd, Tn = q.shape[0], q.shape[1]
    past = page_table.shape[1] * cache_cmp.shape[1]
    t_pos = past + jnp.arange(Tn, dtype=jnp.int32)
    past_cmp = cache_cmp[page_table].reshape(Bd, past, 2, N_KV, HEAD_DIM)
    kc, kc_end = compress_branch(jnp.concatenate([past_cmp[:, :, 0], kc_raw], axis=1), lp, 0)
    kc = rms_norm(kc, lp['k_norm'][0])
    vc, _ = compress_branch(jnp.concatenate([past_cmp[:, :, 1], vc_raw], axis=1), lp, 1)
    wk = cache_win.shape[1]
    flat = lambda a: a.reshape(Bd, a.shape[1], -1)
    new_sel = jnp.concatenate([flat(ks), flat(vs)], axis=-1)
    new_wkv = jnp.concatenate([flat(kw), flat(vw)], axis=-1)
    o = nsa_sample_attention(flat(q), flat(gates), flat(kc), flat(vc), jnp.transpose(cache_sel, (0, 2, 3, 4, 1)),
                             page_table, new_sel, flat(cache_win), new_wkv, rel_bias)
    win = jnp.concatenate([cache_win, jnp.stack([kw, vw], axis=2)], axis=1)
    new_win = win[:, wk + Tn - min(WINDOW, wk + Tn):]
    return o.reshape(Bd, Tn, ATTN_WIDTH), (jnp.stack([kc_raw, vc_raw], axis=2), jnp.stack([ks, vs], axis=2), new_win)


def conformer_conv(glu_in, hist, lp):
    u = glu_in[..., :CONV_WIDTH] * jax.nn.sigmoid(glu_in[..., CONV_WIDTH:])
    uh = jnp.concatenate([hist.astype(u.dtype), u], axis=1)
    y = lax.conv_general_dilated(uh, lp['conv_w'][:, None, :], window_strides=(1,), padding='VALID',
                                 dimension_numbers=('NWC', 'WIO', 'NWC'), feature_group_count=CONV_WIDTH)
    y = jax.nn.silu(layer_norm(y + lp['conv_b'], lp['conv_ln_g'], lp['conv_ln_b']))
    return y, uh[:, uh.shape[1] - (CONV_K - 1):]


def peer(h, lp):
    shp = h.shape
    xt = h.reshape(-1, D_MODEL)
    n = xt.shape[0]
    eidx, gate = peer_route(xt, lp['peer_q'], lp['peer_keys'])
    y = peer_experts(xt, eidx, gate, lp['expert_u'], lp['expert_v'])
    return y.reshape(shp)


def split_points():
    pts, acc = [], 0
    for s in SPLIT_SIZES[:-1]:
        acc += s
        pts.append(acc)
    return pts


def layer(x, c, conv_hist, lp, rel_bias, cache):
    B, T = x.shape[0], x.shape[1]
    mod = (jax.nn.silu(c) @ lp['w_ada'] + lp['b_ada'])[:, None, :]
    sh1, sc1, gt1, sh2, sc2, gt2 = jnp.split(mod, 6, axis=-1)
    h = rms_norm(x, lp['norm1']) * (1 + sc1) + sh1
    z = matmul(h.reshape(B * T, D_MODEL), lp['w_in']).reshape(B, T, IN_COLS)
    zq, zkc, zvc, zks, zvs, zkw, zvw, zg, zglu = jnp.split(z, split_points(), axis=-1)

    def kv(z):
        return z.reshape(B, T, N_KV, HEAD_DIM)

    q = rms_norm(zq.reshape(B, T, N_KV, GROUP, HEAD_DIM), lp['q_norm'])
    ks = rms_norm(kv(zks), lp['k_norm'][1])
    kw = rms_norm(kv(zkw), lp['k_norm'][2])
    gates = jax.nn.sigmoid(zg.reshape(B, T, N_KV, GROUP, 3))
    if cache is None:
        o_attn, attn_state = nsa_prompt(q, gates, kv(zkc), kv(zvc), ks, kv(zvs), kw, kv(zvw), lp, rel_bias)
    else:
        cache_cmp, cache_sel, cache_win, page_table = cache
        o_attn, attn_state = nsa_sample(q, gates, kv(zkc), kv(zvc), ks, kv(zvs), kw, kv(zvw),
                                        cache_cmp, cache_sel, cache_win, page_table, lp, rel_bias)
    o_conv, conv_state = conformer_conv(zglu, conv_hist, lp)
    cat = jnp.concatenate([rms_norm(o_attn, lp['attn_out_norm']), o_conv * lp['conv_out_scale']], axis=-1)
    mix = matmul(cat.reshape(B * T, D_MODEL), lp['w_out']).reshape(B, T, D_MODEL)
    x = x + gt1 * mix
    h2 = rms_norm(x, lp['norm2']) * (1 + sc2) + sh2
    x = x + gt2 * peer(h2, lp)
    return x, attn_state, conv_state


def kernel(x_prompt, x_sample, c_prompt, c_sample, cache_cmp_kv, cache_sel_kv, cache_win_kv, state_conv, page_table, rel_bias, w_ada, b_ada, norm1, w_in, q_norm, k_norm, cmp_pe, cmp_w1, cmp_b1, cmp_w2, cmp_b2, conv_w, conv_b, conv_ln_g, conv_ln_b, attn_out_norm, conv_out_scale, w_out, norm2, peer_q, peer_keys, expert_u, expert_v):
    lp = {
        'w_ada': w_ada[0], 'b_ada': b_ada[0], 'norm1': norm1[0], 'w_in': w_in[0],
        'q_norm': q_norm[0], 'k_norm': k_norm[0], 'cmp_pe': cmp_pe[0], 'cmp_w1': cmp_w1[0],
        'cmp_b1': cmp_b1[0], 'cmp_w2': cmp_w2[0], 'cmp_b2': cmp_b2[0], 'conv_w': conv_w[0],
        'conv_b': conv_b[0], 'conv_ln_g': conv_ln_g[0], 'conv_ln_b': conv_ln_b[0],
        'attn_out_norm': attn_out_norm[0], 'conv_out_scale': conv_out_scale[0], 'w_out': w_out[0],
        'norm2': norm2[0], 'peer_q': peer_q[0], 'peer_keys': peer_keys[0],
        'expert_u': expert_u[0], 'expert_v': expert_v[0],
    }
    hist0 = jnp.zeros((x_prompt.shape[0], CONV_K - 1, CONV_WIDTH), x_prompt.dtype)
    y_prompt, (a_c, a_s, a_w), cv = layer(x_prompt, c_prompt, hist0, lp, rel_bias, None)
    y_sample, (b_c, b_s, b_w), cvs = layer(x_sample, c_sample, state_conv[0], lp, rel_bias,
                                           (cache_cmp_kv[0], cache_sel_kv[0], cache_win_kv[0], page_table))
    return (y_prompt, y_sample, a_c[None], a_s[None], a_w[None], cv[None],
            b_c[None], b_s[None], b_w[None], cvs[None])
```

```python
import functools
import math

import jax
import jax.numpy as jnp
import numpy as np
from jax import lax
from jax.experimental import pallas as pl
from jax.experimental.pallas import tpu as pltpu

D_MODEL = 1024
HEAD_DIM = 64
ATTN_WIDTH = D_MODEL // 2
N_HEADS = ATTN_WIDTH // HEAD_DIM
N_KV = 2
GROUP = N_HEADS // N_KV
KV_WIDTH = N_KV * HEAD_DIM
CONV_WIDTH = D_MODEL - ATTN_WIDTH
CONV_K = 31
CMP_LEN = 32
CMP_STRIDE = 16
CMP_HIDDEN = 256
SEL_BLOCK = 64
TOP_N = 16
WINDOW = 512
N_BUCKETS = 32
MAX_DISTANCE = 128
Q_BLOCK = 64
PEER_HEADS = 8
PEER_DQ = 256
N_KEYS = 128
PEER_TOPK = 16
PEER_BLOCK = 256
FORCE_SCORE = 1.0e4
EPS = 1e-6
SPLIT_SIZES = (ATTN_WIDTH, KV_WIDTH, KV_WIDTH, KV_WIDTH, KV_WIDTH, KV_WIDTH, KV_WIDTH, 3 * N_HEADS, 2 * CONV_WIDTH)
IN_COLS = sum(SPLIT_SIZES)

LANES = 128
VMEM_LIMIT = 48 * 1024 * 1024


def _matmul_kernel(x_ref, w_ref, o_ref):
    o_ref[...] = jnp.dot(x_ref[...].astype(jnp.bfloat16), w_ref[...],
                         preferred_element_type=jnp.float32)


def matmul(x, w, tm=512):
    m, k = x.shape
    n = w.shape[1]
    n_pad = -(-n // LANES) * LANES
    wb = jnp.pad(w, ((0, 0), (0, n_pad - n))).astype(jnp.bfloat16)
    tm = min(tm, m)
    out = pl.pallas_call(
        _matmul_kernel,
        grid=(m // tm,),
        in_specs=[pl.BlockSpec((tm, k), lambda i: (i, 0)),
                  pl.BlockSpec((k, n_pad), lambda i: (0, 0))],
        out_specs=pl.BlockSpec((tm, n_pad), lambda i: (i, 0)),
        out_shape=jax.ShapeDtypeStruct((m, n_pad), jnp.float32),
        compiler_params=pltpu.CompilerParams(dimension_semantics=("arbitrary",),
                                             vmem_limit_bytes=VMEM_LIMIT),
        name="matmul",
    )(x, wb)
    return out[:, :n]


NEG = -1.0e30
SEL_MASK = 16384.0
FAR_KEYS = 512
NEAR_BLOCKS = 3
WIN_BLOCKS = WINDOW // SEL_BLOCK + 1


def _bucket_np(d):
    n = np.maximum(d, 0)
    exact = N_BUCKETS // 2
    nf = np.maximum(n, 1).astype(np.float32)
    large = exact + (np.log(nf / np.float32(exact)) / np.float32(math.log(MAX_DISTANCE / exact))
                     * np.float32(N_BUCKETS - exact)).astype(np.int32)
    return np.where(n < exact, n, np.minimum(large, N_BUCKETS - 1))


def _bias_table(rel_bias, dist, valid):
    tbl = rel_bias.astype(jnp.float32) - rel_bias[N_BUCKETS - 1].astype(jnp.float32)[None, :]
    b = jnp.transpose(tbl[_bucket_np(dist)], (0, 3, 1, 2))
    return jnp.where(valid[:, None], b, NEG)


def _toeplitz_bias(rel_bias, n_var, tq, m, valid_fn):
    n = n_var * tq
    period = n + m
    k = np.arange(period)
    d = np.where(k < m, -k, period - k)
    g = _bias_table(rel_bias, d[None, None, :], valid_fn(d)[None, None, :])[0, :, 0, :]
    t = jnp.tile(g, (1, n))[:, :n * (period - 1)].reshape(N_HEADS, n, period - 1)[:, :, :m]
    return jnp.transpose(t.reshape(N_HEADS, n_var, tq, m), (1, 0, 2, 3))


def _group_sum(x, grp, t):
    acc = x[grp * GROUP * t:(grp * GROUP + 1) * t]
    for r in range(1, GROUP):
        acc = acc + x[(grp * GROUP + r) * t:(grp * GROUP + r + 1) * t]
    return acc


def _nsa_prompt_kernel(q_ref, g_ref, kc_ref, vc_ref, ks_ref, vs_ref, kw_ref, vw_ref,
                       bc_ref, bsn_ref, bw_ref, cover_ref, e_ref, o_ref, *, tq):
    f32, bf16 = jnp.float32, jnp.bfloat16
    i = pl.program_id(1)
    rows = N_HEADS * tq
    nt = (((1,), (1,)), ((), ()))

    q = q_ref[0]
    lane_half = lax.broadcasted_iota(jnp.int32, (tq, 2 * HEAD_DIM), 1) // HEAD_DIM
    parts = []
    for h in range(N_HEADS):
        blk = q[:, (h // 2) * 2 * HEAD_DIM:(h // 2 + 1) * 2 * HEAD_DIM]
        grp = h // GROUP
        if h % 2 != grp:
            blk = pltpu.roll(blk, HEAD_DIM, 1)
        parts.append(jnp.where(lane_half == grp, blk, 0.0))
    qpad = jnp.concatenate(parts, axis=0).astype(bf16)

    def heads(x):
        return jnp.concatenate([x[:tq]] * GROUP + [x[tq:]] * GROUP, axis=0)

    tbl_c = bc_ref[0].reshape(rows, bc_ref.shape[-1])
    s = lax.dot_general(qpad, kc_ref[0], nt, preferred_element_type=f32) + tbl_c
    ok = tbl_c > 0.5 * NEG
    m = jnp.max(s, axis=1, keepdims=True)
    e = jnp.where(ok, jnp.exp(s - m), 0.0)
    d = jnp.sum(e, axis=1, keepdims=True)
    pc = e / jnp.where(d > 0.0, d, 1.0)
    pc16 = pc.astype(bf16)
    o_cmp = jnp.dot(pc16, vc_ref[0], preferred_element_type=f32)

    imp_h = jnp.dot(pc16, cover_ref[...], preferred_element_type=f32)
    imp = jnp.concatenate([_group_sum(imp_h, grp, tq) for grp in range(N_KV)], axis=0)
    ns = imp.shape[1]
    j = lax.broadcasted_iota(jnp.int32, imp.shape, 1)
    forced = (j == 0) | (j == i) | (j == i - 1)
    imp = jnp.where(forced, imp + FORCE_SCORE, imp)
    imp = jnp.where(j > i, -FORCE_SCORE, imp)
    imp_p = jnp.concatenate([imp[:tq], imp[tq:]], axis=1)
    lane_p = lax.broadcasted_iota(jnp.int32, imp_p.shape, 1)
    j_p = jnp.where(lane_p < ns, lane_p, lane_p - ns)
    cnt = jnp.zeros(imp_p.shape, jnp.int32)
    for jp in range(ns):
        col = jnp.where(lane_p < ns, imp_p[:, jp:jp + 1], imp_p[:, ns + jp:ns + jp + 1])
        ahead = (col > imp_p) | ((col == imp_p) & (j_p > jp))
        cnt = cnt + ahead.astype(jnp.int32)
    sel_p = jnp.where(cnt < TOP_N, 1.0, 0.0)
    sel = jnp.concatenate([sel_p[:, :ns], sel_p[:, ns:]], axis=0) > 0.5

    nb0 = jnp.maximum(i - (NEAR_BLOCKS - 1), 0)
    near = NEAR_BLOCKS * SEL_BLOCK
    k0 = pl.multiple_of(nb0 * SEL_BLOCK, SEL_BLOCK)
    sel_nb0 = jnp.sum(jnp.where((j == nb0) & sel, 1.0, 0.0), axis=1, keepdims=True)
    lane = lax.broadcasted_iota(jnp.int32, (rows, near), 1)
    s = lax.dot_general(qpad, ks_ref[0, pl.ds(k0, near), :], nt, preferred_element_type=f32)
    s = s + bsn_ref[0].reshape(rows, near)
    s = s + jnp.where(lane < SEL_BLOCK, (heads(sel_nb0) - 1.0) * SEL_MASK, 0.0)
    m = jnp.max(s, axis=1, keepdims=True)
    p = jnp.exp(s - m)
    l = jnp.sum(p, axis=1, keepdims=True)
    acc = jnp.dot(p.astype(bf16), vs_ref[0, pl.ds(k0, near), :], preferred_element_type=f32)

    sel_far = heads(jnp.where(sel & (j < nb0), 0.0, -1.0)).astype(bf16)
    n_far = (nb0 * SEL_BLOCK + FAR_KEYS - 1) // FAR_KEYS

    def far_body(c, carry):
        m, l, acc = carry
        c0 = pl.multiple_of(c * FAR_KEYS, FAR_KEYS)
        s = lax.dot_general(qpad, ks_ref[0, pl.ds(c0, FAR_KEYS), :], nt, preferred_element_type=f32)
        s = s + jnp.dot(sel_far, e_ref[c], preferred_element_type=f32)
        m_new = jnp.maximum(m, jnp.max(s, axis=1, keepdims=True))
        a = jnp.exp(m - m_new)
        p = jnp.exp(s - m_new)
        l = a * l + jnp.sum(p, axis=1, keepdims=True)
        acc = a * acc + jnp.dot(p.astype(bf16), vs_ref[0, pl.ds(c0, FAR_KEYS), :],
                                preferred_element_type=f32)
        return m_new, l, acc

    m, l, acc = lax.fori_loop(0, n_far, far_body, (m, l, acc))
    o_sel = acc / l

    wk = WIN_BLOCKS * SEL_BLOCK
    w0 = pl.multiple_of(jnp.maximum(i - (WIN_BLOCKS - 1), 0) * SEL_BLOCK, SEL_BLOCK)
    s = lax.dot_general(qpad, kw_ref[0, pl.ds(w0, wk), :], nt, preferred_element_type=f32)
    s = s + bw_ref[0].reshape(rows, wk)
    m = jnp.max(s, axis=1, keepdims=True)
    p = jnp.exp(s - m)
    l = jnp.sum(p, axis=1, keepdims=True)
    o_win = jnp.dot(p.astype(bf16), vw_ref[0, pl.ds(w0, wk), :], preferred_element_type=f32) / l

    gates = g_ref[0]
    for h in range(N_HEADS):
        grp = h // GROUP
        r0, r1 = h * tq, (h + 1) * tq
        c0, c1 = grp * HEAD_DIM, (grp + 1) * HEAD_DIM
        o_h = (gates[:, 3 * h:3 * h + 1] * o_cmp[r0:r1, c0:c1]
               + gates[:, 3 * h + 1:3 * h + 2] * o_sel[r0:r1, c0:c1]
               + gates[:, 3 * h + 2:3 * h + 3] * o_win[r0:r1, c0:c1])
        o_ref[0, :, h * HEAD_DIM:(h + 1) * HEAD_DIM] = o_h


def nsa_prompt_attention(q, gates, kc, vc, ks, vs, kw, vw, rel_bias):
    B, T = q.shape[0], q.shape[1]
    tq = SEL_BLOCK
    nq = T // tq
    ns = T // SEL_BLOCK
    nc = kc.shape[1]
    ncp = -(-nc // LANES) * LANES
    bf16 = jnp.bfloat16
    scale = HEAD_DIM ** -0.5
    assert T % FAR_KEYS == 0 and T >= WIN_BLOCKS * SEL_BLOCK

    tl = np.arange(tq)[None, :, None]
    r = tq // CMP_STRIDE
    span = r * (nq - 1) + ncp
    y = np.arange(span)[None, None, :]
    dy = tl + CMP_STRIDE * (y - (ncp - 1)) - (CMP_LEN - 1)
    master = _bias_table(rel_bias, dy, dy >= 0)[0][:, :, ::-1]
    pad_col = jnp.arange(ncp) < nc
    bc = jnp.stack([jnp.where(pad_col, master[:, :, r * (nq - 1 - i):r * (nq - 1 - i) + ncp], NEG)
                    for i in range(nq)])
    bsn = _toeplitz_bias(rel_bias, NEAR_BLOCKS, tq, NEAR_BLOCKS * SEL_BLOCK, lambda d: d >= 0)
    bw = _toeplitz_bias(rel_bias, WIN_BLOCKS, tq, WIN_BLOCKS * SEL_BLOCK, lambda d: (d >= 0) & (d < WINDOW))

    ci = np.arange(ncp)[:, None]
    sj = np.arange(ns)[None, :]
    cover = ((ci * CMP_STRIDE < (sj + 1) * SEL_BLOCK) & (ci * CMP_STRIDE + CMP_LEN > sj * SEL_BLOCK)
             & (ci < nc))
    cover = jnp.asarray(cover.astype(np.float32), bf16)
    pos = np.arange(T).reshape(T // FAR_KEYS, 1, FAR_KEYS)
    expand = jnp.asarray(np.where(pos // SEL_BLOCK == np.arange(ns)[None, :, None], SEL_MASK, 0.0)
                         .astype(np.float32), bf16)

    padc = ((0, 0), (0, ncp - nc), (0, 0))
    seq = lambda b, i: (b, 0, 0)
    return pl.pallas_call(
        functools.partial(_nsa_prompt_kernel, tq=tq),
        grid=(B, nq),
        in_specs=[
            pl.BlockSpec((1, tq, ATTN_WIDTH), lambda b, i: (b, i, 0)),
            pl.BlockSpec((1, tq, 3 * N_HEADS), lambda b, i: (b, i, 0)),
            pl.BlockSpec((1, ncp, KV_WIDTH), seq),
            pl.BlockSpec((1, ncp, KV_WIDTH), seq),
            pl.BlockSpec((1, T, KV_WIDTH), seq),
            pl.BlockSpec((1, T, KV_WIDTH), seq),
            pl.BlockSpec((1, T, KV_WIDTH), seq),
            pl.BlockSpec((1, T, KV_WIDTH), seq),
            pl.BlockSpec((1, N_HEADS, tq, ncp), lambda b, i: (i, 0, 0, 0)),
            pl.BlockSpec((1, N_HEADS, tq, NEAR_BLOCKS * SEL_BLOCK),
                         lambda b, i: (jnp.minimum(i, NEAR_BLOCKS - 1), 0, 0, 0)),
            pl.BlockSpec((1, N_HEADS, tq, WIN_BLOCKS * SEL_BLOCK),
                         lambda b, i: (jnp.minimum(i, WIN_BLOCKS - 1), 0, 0, 0)),
            pl.BlockSpec((ncp, ns), lambda b, i: (0, 0)),
            pl.BlockSpec((T // FAR_KEYS, ns, FAR_KEYS), lambda b, i: (0, 0, 0)),
        ],
        out_specs=pl.BlockSpec((1, tq, ATTN_WIDTH), lambda b, i: (b, i, 0)),
        out_shape=jax.ShapeDtypeStruct((B, T, ATTN_WIDTH), jnp.float32),
        compiler_params=pltpu.CompilerParams(dimension_semantics=("arbitrary", "arbitrary"),
                                             vmem_limit_bytes=VMEM_LIMIT),
        name="nsa_prompt",
    )(q * scale, gates, jnp.pad(kc, padc).astype(bf16), jnp.pad(vc, padc).astype(bf16),
      ks.astype(bf16), vs.astype(bf16), kw.astype(bf16), vw.astype(bf16), bc, bsn, bw, cover, expand)


NEW_PAD = 16
KV_SLOTS = 3


def _softmax_parts(s):
    m = jnp.max(s, axis=1, keepdims=True)
    p = jnp.exp(s - m)
    return m, p, jnp.sum(p, axis=1, keepdims=True)


def _nsa_sample_kernel(pt_ref, q_ref, g_ref, kc_ref, vc_ref, cache_ref, nsk_ref, nsv_ref, cw_ref, nwk_ref, nwv_ref,
                       bc_ref, bsl_ref, bnew_ref, bw_ref, cover_ref, e_ref, o_ref, kv_buf, sem, *, tn, ns):
    f32, bf16 = jnp.float32, jnp.bfloat16
    nt = (((1,), (1,)), ((), ()))
    b = pl.program_id(0)
    n_pages, page = pt_ref.shape[1], cache_ref.shape[-1]
    pages_per_chunk = FAR_KEYS // page
    n_far = n_pages // pages_per_chunk
    total = pl.num_programs(0) * n_far

    def fetch(gc):
        seq, first = gc // n_far, (gc % n_far) * pages_per_chunk
        slot = gc % KV_SLOTS
        for i in range(pages_per_chunk):
            pltpu.make_async_copy(cache_ref.at[pt_ref[seq, first + i]], kv_buf.at[slot, i], sem.at[slot]).start()

    @pl.when(b == 0)
    def _():
        for gc in range(KV_SLOTS - 1):
            fetch(gc)
    qpad = q_ref[0]
    cur = ns - 1

    def heads(x):
        return jnp.concatenate([x[:tn]] * GROUP + [x[tn:]] * GROUP, axis=0)

    s = lax.dot_general(qpad, kc_ref[0], nt, preferred_element_type=f32) + bc_ref[...]
    _, e, d = _softmax_parts(s)
    pc = e / d
    pc16 = pc.astype(bf16)
    o_cmp = jnp.dot(pc16, vc_ref[0], preferred_element_type=f32)

    imp_h = jnp.dot(pc16, cover_ref[...], preferred_element_type=f32)
    imp = jnp.concatenate([_group_sum(imp_h, grp, tn) for grp in range(N_KV)], axis=0)
    j = lax.broadcasted_iota(jnp.int32, imp.shape, 1)
    forced = (j == 0) | (j == cur) | (j == cur - 1)
    imp = jnp.where(forced, imp + FORCE_SCORE, imp)
    imp = jnp.where(j > cur, -jnp.inf, imp)
    cnt = jnp.zeros(imp.shape, jnp.int32)
    for jp in range(ns):
        col = imp[:, jp:jp + 1]
        ahead = (col > imp) | ((col == imp) & (j > jp))
        cnt = cnt + ahead.astype(jnp.int32)
    sel = cnt < TOP_N
    sel_far = heads(jnp.where(sel & (j < cur), 0.0, -1.0)).astype(bf16)

    bnew = bnew_ref[...]
    m, p, l = _softmax_parts(lax.dot_general(qpad, nsk_ref[0], nt, preferred_element_type=f32) + bnew)
    acc = jnp.dot(p.astype(bf16), nsv_ref[0], preferred_element_type=f32)
    half = GROUP * tn
    q_g = [qpad[grp * half:(grp + 1) * half, grp * HEAD_DIM:(grp + 1) * HEAD_DIM] for grp in range(N_KV)]
    zero = jnp.zeros((half, HEAD_DIM), f32)

    def far_chunk(c, carry, bias):
        m, l, acc = carry
        gc = b * n_far + c
        slot = gc % KV_SLOTS
        pltpu.make_async_copy(kv_buf.at[slot], kv_buf.at[slot], sem.at[slot]).wait()

        @pl.when(gc + KV_SLOTS - 1 < total)
        def _():
            fetch(gc + KV_SLOTS - 1)

        def plane(kv, grp):
            return jnp.concatenate([kv_buf[slot, i, kv, grp] for i in range(pages_per_chunk)], axis=1).astype(bf16)

        s = jnp.concatenate([jnp.dot(q_g[grp], plane(0, grp), preferred_element_type=f32)
                             for grp in range(N_KV)], axis=0)
        s = s + jnp.dot(sel_far, e_ref[c], preferred_element_type=f32)
        if bias is not None:
            s = s + bias
        m_new = jnp.maximum(m, jnp.max(s, axis=1, keepdims=True))
        a = jnp.exp(m - m_new)
        p = jnp.exp(s - m_new)
        l = a * l + jnp.sum(p, axis=1, keepdims=True)
        p = p.astype(bf16)
        pv = [lax.dot_general(p[grp * half:(grp + 1) * half], plane(1, grp), nt, preferred_element_type=f32)
              for grp in range(N_KV)]
        pv = jnp.concatenate([jnp.concatenate([pv[0], zero], axis=1),
                              jnp.concatenate([zero, pv[1]], axis=1)], axis=0)
        return m_new, l, a * acc + pv

    carry = lax.fori_loop(0, n_far - 1, lambda c, carry: far_chunk(c, carry, None), (m, l, acc))
    m, l, acc = far_chunk(n_far - 1, carry, bsl_ref[...])
    o_sel = acc / l

    cw = cw_ref[0]
    s_c = lax.dot_general(qpad, cw[:, :KV_WIDTH].astype(bf16), nt, preferred_element_type=f32) + bw_ref[...]
    s_n = lax.dot_general(qpad, nwk_ref[0], nt, preferred_element_type=f32) + bnew
    m = jnp.maximum(jnp.max(s_c, axis=1, keepdims=True), jnp.max(s_n, axis=1, keepdims=True))
    p_c = jnp.exp(s_c - m)
    p_n = jnp.exp(s_n - m)
    l = jnp.sum(p_c, axis=1, keepdims=True) + jnp.sum(p_n, axis=1, keepdims=True)
    o_win = (jnp.dot(p_c.astype(bf16), cw[:, KV_WIDTH:].astype(bf16), preferred_element_type=f32)
             + jnp.dot(p_n.astype(bf16), nwv_ref[0], preferred_element_type=f32)) / l

    g = g_ref[0]
    o_ref[0] = g[:, 0:1] * o_cmp + g[:, 1:2] * o_sel + g[:, 2:3] * o_win


def nsa_sample_attention(q, gates, kc, vc, cache_sel, page_table, new_sel, cache_win, new_win, rel_bias):
    B, tn = q.shape[0], q.shape[1]
    past, wk, nc = page_table.shape[1] * cache_sel.shape[-1], cache_win.shape[1], kc.shape[1]
    assert past % FAR_KEYS == 0 and past % SEL_BLOCK == 0 and tn <= NEW_PAD and wk == WINDOW
    ns = past // SEL_BLOCK + 1
    nsp = -(-ns // LANES) * LANES
    ncp = -(-nc // LANES) * LANES
    rows = N_HEADS * tn
    bf16 = jnp.bfloat16
    n_far = past // FAR_KEYS

    tl = np.arange(tn)[:, None]
    flat = lambda t: t[0].reshape(rows, t.shape[-1])
    cc = np.arange(ncp)[None, :]
    dc = past + tl - (cc * CMP_STRIDE + CMP_LEN - 1)
    bc = flat(_bias_table(rel_bias, dc[None], ((dc >= 0) & (cc < nc))[None]))
    dl = tl + FAR_KEYS - np.arange(FAR_KEYS)[None, :]
    bsl = flat(_bias_table(rel_bias, dl[None], (dl >= 0)[None]))
    jn = np.arange(NEW_PAD)[None, :]
    dn = tl - jn
    bnew = flat(_bias_table(rel_bias, dn[None], ((dn >= 0) & (jn < tn))[None]))
    dw = wk + tl - np.arange(wk)[None, :]
    bw = flat(_bias_table(rel_bias, dw[None], ((dw >= 0) & (dw < WINDOW))[None]))

    ci = np.arange(ncp)[:, None]
    sj = np.arange(nsp)[None, :]
    cover = ((ci * CMP_STRIDE < (sj + 1) * SEL_BLOCK) & (ci * CMP_STRIDE + CMP_LEN > sj * SEL_BLOCK)
             & (ci < nc) & (sj < ns))
    cover = jnp.asarray(cover.astype(np.float32), bf16)
    pos = np.arange(past).reshape(n_far, 1, FAR_KEYS)
    expand = jnp.asarray(np.where(pos // SEL_BLOCK == np.arange(nsp)[None, :, None], SEL_MASK, 0.0)
                         .astype(np.float32), bf16)

    qh = jnp.transpose((q * HEAD_DIM ** -0.5).reshape(B, tn, N_HEADS, HEAD_DIM), (0, 2, 1, 3))
    zero = jnp.zeros_like(qh)
    grp = (np.arange(N_HEADS) // GROUP)[None, :, None, None]
    qpad = jnp.where(grp == 0, jnp.concatenate([qh, zero], -1), jnp.concatenate([zero, qh], -1))
    qpad = qpad.reshape(B, rows, KV_WIDTH).astype(bf16)
    grow = jnp.transpose(gates.reshape(B, tn, N_HEADS, 3), (0, 2, 1, 3)).reshape(B, rows, 3)
    padn = lambda a: jnp.pad(a, ((0, 0), (0, NEW_PAD - tn), (0, 0))).astype(bf16)
    padc = ((0, 0), (0, ncp - nc), (0, 0))

    seq3 = lambda b, pt: (b, 0, 0)
    const2 = lambda b, pt: (0, 0)
    grid_spec = pltpu.PrefetchScalarGridSpec(
        num_scalar_prefetch=1,
        grid=(B,),
        in_specs=[
            pl.BlockSpec((1, rows, KV_WIDTH), seq3),
            pl.BlockSpec((1, rows, 3), seq3),
            pl.BlockSpec((1, ncp, KV_WIDTH), seq3),
            pl.BlockSpec((1, ncp, KV_WIDTH), seq3),
            pl.BlockSpec(memory_space=pl.ANY),
            pl.BlockSpec((1, NEW_PAD, KV_WIDTH), seq3),
            pl.BlockSpec((1, NEW_PAD, KV_WIDTH), seq3),
            pl.BlockSpec((1, wk, 2 * KV_WIDTH), seq3),
            pl.BlockSpec((1, NEW_PAD, KV_WIDTH), seq3),
            pl.BlockSpec((1, NEW_PAD, KV_WIDTH), seq3),
            pl.BlockSpec((rows, ncp), const2),
            pl.BlockSpec((rows, FAR_KEYS), const2),
            pl.BlockSpec((rows, NEW_PAD), const2),
            pl.BlockSpec((rows, wk), const2),
            pl.BlockSpec((ncp, nsp), const2),
            pl.BlockSpec((n_far, nsp, FAR_KEYS), lambda b, pt: (0, 0, 0)),
        ],
        out_specs=pl.BlockSpec((1, rows, KV_WIDTH), seq3),
        scratch_shapes=[pltpu.VMEM((KV_SLOTS, FAR_KEYS // cache_sel.shape[-1]) + cache_sel.shape[1:], jnp.float32),
                        pltpu.SemaphoreType.DMA((KV_SLOTS,))],
    )
    out = pl.pallas_call(
        functools.partial(_nsa_sample_kernel, tn=tn, ns=ns),
        grid_spec=grid_spec,
        out_shape=jax.ShapeDtypeStruct((B, rows, KV_WIDTH), jnp.float32),
        compiler_params=pltpu.CompilerParams(dimension_semantics=("arbitrary",),
                                             vmem_limit_bytes=VMEM_LIMIT),
        name="nsa_sample",
    )(page_table, qpad, grow, jnp.pad(kc, padc).astype(bf16), jnp.pad(vc, padc).astype(bf16), cache_sel,
      padn(new_sel[:, :, :KV_WIDTH]), padn(new_sel[:, :, KV_WIDTH:]), cache_win,
      padn(new_win[:, :, :KV_WIDTH]), padn(new_win[:, :, KV_WIDTH:]), bc, bsl, bnew, bw, cover, expand)
    o = out.reshape(B, N_HEADS, tn, 2, HEAD_DIM)
    o = jnp.where(grp[..., None] == 0, o[:, :, :, 0:1], o[:, :, :, 1:2])[:, :, :, 0]
    return jnp.transpose(o, (0, 2, 1, 3)).reshape(B, tn, ATTN_WIDTH)


N_SUB = 2 * PEER_HEADS
HALF_DQ = PEER_DQ // 2
N_PAIRS = PEER_HEADS * PEER_TOPK


def _top_rounds(s, row, n):
    width = s.shape[0]
    k = lax.broadcasted_iota(jnp.int32, (n, s.shape[1]), 0)
    vals = jnp.zeros((n, s.shape[1]), jnp.float32)
    ids = jnp.zeros((n, s.shape[1]), jnp.float32)
    for a in range(n):
        m = jnp.max(s, axis=0, keepdims=True)
        ix = jnp.min(jnp.where(s == m, row, float(width)), axis=0, keepdims=True)
        vals = jnp.where(k == a, m, vals)
        ids = jnp.where(k == a, ix, ids)
        s = jnp.where(row == ix, -jnp.inf, s)
    return vals, ids


def _peer_route_kernel(h_ref, wq_ref, keys_ref, idx_ref, g_ref, q_scr, e_scr, w_scr):
    f32, bf16 = jnp.float32, jnp.bfloat16
    tm = h_ref.shape[0]
    nt = (((1,), (1,)), ((), ()))
    q = jnp.dot(h_ref[...].astype(bf16), wq_ref[...], preferred_element_type=f32)
    for k in range(N_SUB):
        q_scr[k] = q[:, k * HALF_DQ:(k + 1) * HALF_DQ].astype(bf16)
    row = lax.broadcasted_iota(jnp.int32, (N_KEYS, tm), 0).astype(f32)
    cand_blocks, a = [], 0
    while PEER_TOPK // (a + 1) > 1:
        cand_blocks.append((a, a + 1, min(-(-(PEER_TOPK // (a + 1)) // 8) * 8, PEER_TOPK)))
        a += 1
    cand_blocks.append((a, PEER_TOPK, 1))
    row2 = jnp.concatenate(
        [(lax.broadcasted_iota(jnp.int32, (max(a1 - a0, nb), tm), 0) * (1 if nb > 1 else PEER_TOPK)
          + a0 * PEER_TOPK).astype(f32) for a0, a1, nb in cand_blocks], axis=0)
    n_cand = PEER_TOPK * PEER_TOPK

    def head(h, carry):
        s1, i1 = _top_rounds(lax.dot_general(keys_ref[2 * h], q_scr[2 * h], nt, preferred_element_type=f32),
                             row, PEER_TOPK)
        s2, i2 = _top_rounds(lax.dot_general(keys_ref[2 * h + 1], q_scr[2 * h + 1], nt,
                                             preferred_element_type=f32), row, PEER_TOPK)
        cand = jnp.concatenate([s1[a0:a1] + s2[:nb] for a0, a1, nb in cand_blocks], axis=0)
        cidx = jnp.concatenate([i1[a0:a1] * float(N_KEYS) + i2[:nb] for a0, a1, nb in cand_blocks], axis=0)
        k = lax.broadcasted_iota(jnp.int32, (PEER_TOPK, tm), 0)
        top = jnp.zeros((PEER_TOPK, tm), f32)
        eid = jnp.zeros((PEER_TOPK, tm), f32)
        for a in range(PEER_TOPK):
            m = jnp.max(cand, axis=0, keepdims=True)
            pos = jnp.min(jnp.where(cand == m, row2, float(n_cand)), axis=0, keepdims=True)
            hit = row2 == pos
            e = jnp.max(jnp.where(hit, cidx, -1.0), axis=0, keepdims=True)
            top = jnp.where(k == a, m, top)
            eid = jnp.where(k == a, e, eid)
            cand = jnp.where(hit, -jnp.inf, cand)
        ex = jnp.exp(top - top[0:1])
        r0 = pl.multiple_of(h * PEER_TOPK, PEER_TOPK)
        e_scr[pl.ds(r0, PEER_TOPK), :] = eid
        w_scr[pl.ds(r0, PEER_TOPK), :] = ex / jnp.sum(ex, axis=0, keepdims=True)
        return carry

    lax.fori_loop(0, PEER_HEADS, head, 0)
    idx_ref[...] = e_scr[...].T.astype(jnp.int32)
    g_ref[...] = w_scr[...].T


def peer_route(h2, peer_q, peer_keys, tm=256):
    n = h2.shape[0]
    tm = min(tm, n)
    bf16 = jnp.bfloat16
    keys = peer_keys.reshape(N_SUB, N_KEYS, HALF_DQ).astype(bf16)
    return pl.pallas_call(
        _peer_route_kernel,
        grid=(n // tm,),
        in_specs=[pl.BlockSpec((tm, D_MODEL), lambda i: (i, 0)),
                  pl.BlockSpec((D_MODEL, PEER_HEADS * PEER_DQ), lambda i: (0, 0)),
                  pl.BlockSpec((N_SUB, N_KEYS, HALF_DQ), lambda i: (0, 0, 0))],
        out_specs=[pl.BlockSpec((tm, N_PAIRS), lambda i: (i, 0)),
                   pl.BlockSpec((tm, N_PAIRS), lambda i: (i, 0))],
        out_shape=[jax.ShapeDtypeStruct((n, N_PAIRS), jnp.int32),
                   jax.ShapeDtypeStruct((n, N_PAIRS), jnp.float32)],
        scratch_shapes=[pltpu.VMEM((N_SUB, tm, HALF_DQ), bf16),
                        pltpu.VMEM((N_PAIRS, tm), jnp.float32),
                        pltpu.VMEM((N_PAIRS, tm), jnp.float32)],
        compiler_params=pltpu.CompilerParams(dimension_semantics=("arbitrary",),
                                             vmem_limit_bytes=VMEM_LIMIT),
        name="peer_route",
    )(h2, peer_q.astype(bf16), keys)


def _gelu_tanh(x):
    return 0.5 * x * (1.0 + jnp.tanh(math.sqrt(2.0 / math.pi) * (x + 0.044715 * (x * x * x))))


N_SLOTS = 8
DMA_QUEUES = 2
ROW_TILES = 2 * D_MODEL // LANES
U_TILES = D_MODEL // LANES


def _peer_expert_kernel(idx_ref, nxt_ref, x_ref, g_ref, uv_ref, y_ref, *scratch):
    tb = x_ref.shape[0]
    i = pl.program_id(0)
    last = pl.num_programs(0) - 1
    ahead = N_SLOTS - 1
    bufs, sem = scratch[:N_SLOTS], scratch[N_SLOTS]

    def issue(ids, t, slot):
        for k in range(N_PAIRS):
            pltpu.make_async_copy(uv_ref.at[ids[t, k]], bufs[slot].at[k // 8, :, k % 8, :],
                                  sem.at[slot]).start(priority=k % DMA_QUEUES)

    def wait_slot(slot):
        pltpu.make_async_copy(bufs[slot], bufs[slot], sem.at[slot]).wait()

    eye = (lax.broadcasted_iota(jnp.int32, (N_PAIRS, N_PAIRS), 0)
           == lax.broadcasted_iota(jnp.int32, (N_PAIRS, N_PAIRS), 1))

    def compute(t, slot):
        buf = bufs[slot]

        def chunk(c):
            return buf[:, c].reshape(N_PAIRS, LANES)

        x = x_ref[t]
        acc = chunk(0) * x[0:1, :]
        for s in range(1, U_TILES):
            acc = acc + chunk(s) * x[s:s + 1, :]
        a = jnp.sum(acc, axis=1, keepdims=True)
        g_col = jnp.sum(jnp.where(eye, g_ref[pl.ds(t, 1), :], 0.0), axis=1, keepdims=True)
        w = g_col * _gelu_tanh(a)
        y = [jnp.sum(w * chunk(U_TILES + s), axis=0, keepdims=True) for s in range(U_TILES)]
        y_ref[pl.ds(t, 1), :] = jnp.concatenate(y, axis=1)

    @pl.when(i == 0)
    def _():
        for t in range(ahead):
            issue(idx_ref, t, t)

    def group(j, c):
        for u in range(N_SLOTS):
            t = j * N_SLOTS + u
            wait_slot(u)
            issue(idx_ref, t + ahead, (u + ahead) % N_SLOTS)
            compute(t, u)
        return c

    n_groups = tb // N_SLOTS
    lax.fori_loop(0, n_groups - 1, group, 0)
    for u in range(N_SLOTS):
        t = tb - N_SLOTS + u
        wait_slot(u)
        if u == 0:
            issue(idx_ref, t + ahead, ahead)
        else:
            issue(nxt_ref, u - 1, u - 1)
        compute(t, u)

    @pl.when(i == last)
    def _():
        for u in range(ahead):
            wait_slot(u)


def peer_experts(h2, eidx, gate, expert_u, expert_v, tb=64):
    n = h2.shape[0]
    tb = min(tb, n)
    nb = n // tb
    e = expert_u.shape[0]
    uv = jnp.concatenate([expert_u.reshape(e, U_TILES, LANES), expert_v.reshape(e, U_TILES, LANES)], axis=1)
    return pl.pallas_call(
        _peer_expert_kernel,
        grid=(nb,),
        in_specs=[pl.BlockSpec((tb, N_PAIRS), lambda i: (i, 0), memory_space=pltpu.SMEM),
                  pl.BlockSpec((tb, N_PAIRS), lambda i: (jnp.minimum(i + 1, nb - 1), 0), memory_space=pltpu.SMEM),
                  pl.BlockSpec((tb, U_TILES, LANES), lambda i: (i, 0, 0)),
                  pl.BlockSpec((tb, N_PAIRS), lambda i: (i, 0)),
                  pl.BlockSpec(memory_space=pl.ANY)],
        out_specs=pl.BlockSpec((tb, D_MODEL), lambda i: (i, 0)),
        out_shape=jax.ShapeDtypeStruct((n, D_MODEL), jnp.float32),
        scratch_shapes=[pltpu.VMEM((N_PAIRS // 8, ROW_TILES, 8, LANES), jnp.float32)] * N_SLOTS
        + [pltpu.SemaphoreType.DMA((N_SLOTS,))],
        compiler_params=pltpu.CompilerParams(dimension_semantics=("arbitrary",),
                                             vmem_limit_bytes=VMEM_LIMIT),
        name="peer_experts",
    )(eidx, eidx, h2.reshape(n, U_TILES, LANES), gate, uv)


def rms_norm(x, g):
    y = x * lax.rsqrt(jnp.mean(x * x, axis=-1, keepdims=True) + EPS)
    return y * g


def layer_norm(x, g, b):
    mu = jnp.mean(x, axis=-1, keepdims=True)
    var = jnp.mean(jnp.square(x - mu), axis=-1, keepdims=True)
    return (x - mu) * lax.rsqrt(var + EPS) * g + b


def masked_softmax(logits, mask, axis):
    neg = jnp.where(mask, logits, -jnp.inf)
    m = jnp.max(neg, axis=axis, keepdims=True)
    m = jnp.where(jnp.isfinite(m), m, 0.0)
    e = jnp.where(mask, jnp.exp(neg - m), 0.0)
    d = jnp.sum(e, axis=axis, keepdims=True)
    return e / jnp.where(d > 0.0, d, 1.0)


def t5_bucket(dist):
    n = jnp.maximum(dist, 0)
    exact = N_BUCKETS // 2
    nf = jnp.maximum(n, 1).astype(jnp.float32)
    large = exact + (jnp.log(nf / exact) / math.log(MAX_DISTANCE / exact) * (N_BUCKETS - exact)).astype(jnp.int32)
    return jnp.where(n < exact, n, jnp.minimum(large, N_BUCKETS - 1))


def compress(rows, pe, w1, b1, w2, b2):
    B, L = rows.shape[0], rows.shape[1]
    ratio = CMP_LEN // CMP_STRIDE
    n_chunks = L // CMP_STRIDE
    nc = n_chunks - ratio + 1
    r = rows[:, :n_chunks * CMP_STRIDE].reshape(B, n_chunks, CMP_STRIDE, N_KV, HEAD_DIM)
    w1c = w1.reshape(ratio, CMP_STRIDE, HEAD_DIM, CMP_HIDDEN)
    parts = jnp.einsum('bcsgd,jsdh->jbcgh', r, w1c)
    hid = b1 + pe.reshape(-1) @ w1
    for j in range(ratio):
        hid = hid + parts[j, :, j:j + nc]
    out = jax.nn.gelu(hid) @ w2 + b2
    end = jnp.arange(nc, dtype=jnp.int32) * CMP_STRIDE + (CMP_LEN - 1)
    return out, end


def compress_branch(rows, lp, which):
    return compress(rows, lp['cmp_pe'][which], lp['cmp_w1'][which], lp['cmp_b1'][which],
                    lp['cmp_w2'][which], lp['cmp_b2'][which])


def to_blocks(rows):
    B, L = rows.shape[0], rows.shape[1]
    ns = -(-L // SEL_BLOCK)
    rows = jnp.pad(rows, ((0, 0), (0, ns * SEL_BLOCK - L), (0, 0), (0, 0)))
    return jnp.transpose(rows.reshape(B, ns, SEL_BLOCK, N_KV, HEAD_DIM), (0, 3, 1, 2, 4))


def nsa_block(q, t_pos, gates, kc, vc, kc_end, ks_blk, vs_blk, kw, vw, w_pos, rel_bias):
    B = q.shape[0]
    scale = HEAD_DIM ** -0.5
    tbl = rel_bias.astype(jnp.float32).reshape(N_BUCKETS, N_KV, GROUP)
    lc = jnp.einsum('bqgrd,bcgd->bqgrc', q, kc).astype(jnp.float32) * scale
    dc = t_pos[:, None] - kc_end[None, :]
    lc = lc + jnp.transpose(tbl[t5_bucket(dc)], (0, 2, 3, 1))[None]
    pc = masked_softmax(lc, (dc >= 0)[None, :, None, None, :], axis=-1)
    o_cmp = jnp.einsum('bqgrc,bcgd->bqgrd', pc.astype(vc.dtype), vc)
    nc, ns = kc.shape[1], ks_blk.shape[2]
    ci = jnp.arange(nc)[:, None]
    sj = jnp.arange(ns)[None, :]
    cover = ((ci * CMP_STRIDE < (sj + 1) * SEL_BLOCK) & (ci * CMP_STRIDE + CMP_LEN > sj * SEL_BLOCK)).astype(jnp.float32)
    imp = jnp.einsum('bqgrc,cj->bqgj', pc, cover)
    cur = (t_pos // SEL_BLOCK)[:, None]
    forced = (sj == 0) | (sj == cur) | (sj == cur - 1)
    valid = sj * SEL_BLOCK <= t_pos[:, None]
    imp = jnp.where(forced[None, :, None, :], imp + FORCE_SCORE, imp)
    imp = jnp.where(valid[None, :, None, :], imp, -FORCE_SCORE)
    _, idx = lax.top_k(imp, min(TOP_N, ns))
    bi = jnp.arange(B)[:, None, None, None]
    gi = jnp.arange(N_KV)[None, None, :, None]
    ks_g = ks_blk[bi, gi, idx]
    vs_g = vs_blk[bi, gi, idx]
    ls = jnp.einsum('bqgrd,bqgnkd->bqgrnk', q, ks_g).astype(jnp.float32) * scale
    pos = idx[..., None] * SEL_BLOCK + jnp.arange(SEL_BLOCK, dtype=jnp.int32)
    ds = t_pos[None, :, None, None, None] - pos
    tblk = jnp.transpose(tbl, (1, 0, 2))
    bs = tblk[jnp.arange(N_KV)[None, None, :, None, None], t5_bucket(ds)]
    ls = ls + jnp.moveaxis(bs, -1, 3)
    ps = masked_softmax(ls, (ds >= 0)[:, :, :, None], axis=(-2, -1))
    o_sel = jnp.einsum('bqgrnk,bqgnkd->bqgrd', ps.astype(vs_g.dtype), vs_g)
    lw = jnp.einsum('bqgrd,bwgd->bqgrw', q, kw).astype(jnp.float32) * scale
    dw = t_pos[:, None] - w_pos[None, :]
    lw = lw + jnp.transpose(tbl[t5_bucket(dw)], (0, 2, 3, 1))[None]
    mw = (dw >= 0) & (dw < WINDOW) & (w_pos >= 0)[None, :]
    pw = masked_softmax(lw, mw[None, :, None, None, :], axis=-1)
    o_win = jnp.einsum('bqgrw,bwgd->bqgrd', pw.astype(vw.dtype), vw)
    return gates[..., 0:1] * o_cmp + gates[..., 1:2] * o_sel + gates[..., 2:3] * o_win


def nsa_prompt(q, gates, kc_raw, vc_raw, ks, vs, kw, vw, lp, rel_bias):
    B, T = q.shape[0], q.shape[1]
    kc, kc_end = compress_branch(kc_raw, lp, 0)
    kc = rms_norm(kc, lp['k_norm'][0])
    vc, _ = compress_branch(vc_raw, lp, 1)
    flat = lambda a: a.reshape(B, a.shape[1], -1)
    o = nsa_prompt_attention(flat(q), flat(gates), flat(kc), flat(vc), flat(ks), flat(vs), flat(kw), flat(vw),
                             rel_bias)
    new_win = jnp.stack([kw, vw], axis=2)[:, T - min(WINDOW, T):]
    return o, (jnp.stack([kc_raw, vc_raw], axis=2), jnp.stack([ks, vs], axis=2), new_win)


def nsa_sample(q, gates, kc_raw, vc_raw, ks, vs, kw, vw, cache_cmp, cache_sel, cache_win, page_table, lp, rel_bias):
    Bd, Tn = q.shape[0], q.shape[1]
    past = page_table.shape[1] * cache_cmp.shape[1]
    t_pos = past + jnp.arange(Tn, dtype=jnp.int32)
    past_cmp = cache_cmp[page_table].reshape(Bd, past, 2, N_KV, HEAD_DIM)
    kc, kc_end = compress_branch(jnp.concatenate([past_cmp[:, :, 0], kc_raw], axis=1), lp, 0)
    kc = rms_norm(kc, lp['k_norm'][0])
    vc, _ = compress_branch(jnp.concatenate([past_cmp[:, :, 1], vc_raw], axis=1), lp, 1)
    wk = cache_win.shape[1]
    flat = lambda a: a.reshape(Bd, a.shape[1], -1)
    new_sel = jnp.concatenate([flat(ks), flat(vs)], axis=-1)
    new_wkv = jnp.concatenate([flat(kw), flat(vw)], axis=-1)
    o = nsa_sample_attention(flat(q), flat(gates), flat(kc), flat(vc), jnp.transpose(cache_sel, (0, 2, 3, 4, 1)),
                             page_table, new_sel, flat(cache_win), new_wkv, rel_bias)
    win = jnp.concatenate([cache_win, jnp.stack([kw, vw], axis=2)], axis=1)
    new_win = win[:, wk + Tn - min(WINDOW, wk + Tn):]
    return o.reshape(Bd, Tn, ATTN_WIDTH), (jnp.stack([kc_raw, vc_raw], axis=2), jnp.stack([ks, vs], axis=2), new_win)


def conformer_conv(glu_in, hist, lp):
    u = glu_in[..., :CONV_WIDTH] * jax.nn.sigmoid(glu_in[..., CONV_WIDTH:])
    uh = jnp.concatenate([hist.astype(u.dtype), u], axis=1)
    y = lax.conv_general_dilated(uh, lp['conv_w'][:, None, :], window_strides=(1,), padding='VALID',
                                 dimension_numbers=('NWC', 'WIO', 'NWC'), feature_group_count=CONV_WIDTH)
    y = jax.nn.silu(layer_norm(y + lp['conv_b'], lp['conv_ln_g'], lp['conv_ln_b']))
    return y, uh[:, uh.shape[1] - (CONV_K - 1):]


def peer(h, lp):
    shp = h.shape
    xt = h.reshape(-1, D_MODEL)
    n = xt.shape[0]
    eidx, gate = peer_route(xt, lp['peer_q'], lp['peer_keys'])
    y = peer_experts(xt, eidx, gate, lp['expert_u'], lp['expert_v'])
    return y.reshape(shp)


def split_points():
    pts, acc = [], 0
    for s in SPLIT_SIZES[:-1]:
        acc += s
        pts.append(acc)
    return pts


def layer(x, c, conv_hist, lp, rel_bias, cache):
    B, T = x.shape[0], x.shape[1]
    mod = (jax.nn.silu(c) @ lp['w_ada'] + lp['b_ada'])[:, None, :]
    sh1, sc1, gt1, sh2, sc2, gt2 = jnp.split(mod, 6, axis=-1)
    h = rms_norm(x, lp['norm1']) * (1 + sc1) + sh1
    z = matmul(h.reshape(B * T, D_MODEL), lp['w_in']).reshape(B, T, IN_COLS)
    zq, zkc, zvc, zks, zvs, zkw, zvw, zg, zglu = jnp.split(z, split_points(), axis=-1)

    def kv(z):
        return z.reshape(B, T, N_KV, HEAD_DIM)

    q = rms_norm(zq.reshape(B, T, N_KV, GROUP, HEAD_DIM), lp['q_norm'])
    ks = rms_norm(kv(zks), lp['k_norm'][1])
    kw = rms_norm(kv(zkw), lp['k_norm'][2])
    gates = jax.nn.sigmoid(zg.reshape(B, T, N_KV, GROUP, 3))
    if cache is None:
        o_attn, attn_state = nsa_prompt(q, gates, kv(zkc), kv(zvc), ks, kv(zvs), kw, kv(zvw), lp, rel_bias)
    else:
        cache_cmp, cache_sel, cache_win, page_table = cache
        o_attn, attn_state = nsa_sample(q, gates, kv(zkc), kv(zvc), ks, kv(zvs), kw, kv(zvw),
                                        cache_cmp, cache_sel, cache_win, page_table, lp, rel_bias)
    o_conv, conv_state = conformer_conv(zglu, conv_hist, lp)
    cat = jnp.concatenate([rms_norm(o_attn, lp['attn_out_norm']), o_conv * lp['conv_out_scale']], axis=-1)
    mix = matmul(cat.reshape(B * T, D_MODEL), lp['w_out']).reshape(B, T, D_MODEL)
    x = x + gt1 * mix
    h2 = rms_norm(x, lp['norm2']) * (1 + sc2) + sh2
    x = x + gt2 * peer(h2, lp)
    return x, attn_state, conv_state


def kernel(x_prompt, x_sample, c_prompt, c_sample, cache_cmp_kv, cache_sel_kv, cache_win_kv, state_conv, page_table, rel_bias, w_ada, b_ada, norm1, w_in, q_norm, k_norm, cmp_pe, cmp_w1, cmp_b1, cmp_w2, cmp_b2, conv_w, conv_b, conv_ln_g, conv_ln_b, attn_out_norm, conv_out_scale, w_out, norm2, peer_q, peer_keys, expert_u, expert_v):
    lp = {
        'w_ada': w_ada[0], 'b_ada': b_ada[0], 'norm1': norm1[0], 'w_in': w_in[0],
        'q_norm': q_norm[0], 'k_norm': k_norm[0], 'cmp_pe': cmp_pe[0], 'cmp_w1': cmp_w1[0],
        'cmp_b1': cmp_b1[0], 'cmp_w2': cmp_w2[0], 'cmp_b2': cmp_b2[0], 'conv_w': conv_w[0],
        'conv_b': conv_b[0], 'conv_ln_g': conv_ln_g[0], 'conv_ln_b': conv_ln_b[0],
        'attn_out_norm': attn_out_norm[0], 'conv_out_scale': conv_out_scale[0], 'w_out': w_out[0],
        'norm2': norm2[0], 'peer_q': peer_q[0], 'peer_keys': peer_keys[0],
        'expert_u': expert_u[0], 'expert_v': expert_v[0],
    }
    hist0 = jnp.zeros((x_prompt.shape[0], CONV_K - 1, CONV_WIDTH), x_prompt.dtype)
    y_prompt, (a_c, a_s, a_w), cv = layer(x_prompt, c_prompt, hist0, lp, rel_bias, None)
    y_sample, (b_c, b_s, b_w), cvs = layer(x_sample, c_sample, state_conv[0], lp, rel_bias,
                                           (cache_cmp_kv[0], cache_sel_kv[0], cache_win_kv[0], page_table))
    return (y_prompt, y_sample, a_c[None], a_s[None], a_w[None], cv[None],
            b_c[None], b_s[None], b_w[None], cvs[None])
```
